```python
import jax, jax.numpy as jnp
from jax import lax
import numpy as np

D_MODEL = 1024
BATCH = 2
SEQ = 8192
DEPTH = 1
DEC_BATCH = 32
DEC_SEQ = 4
PAST_LEN = 8192
PAGE_SIZE = 128

H_SB = 8
HD_SB = 64
W_SB = H_SB * HD_SB
H_MB = 8
HD_MB = 64
W_MB = H_MB * HD_MB
Q_BLOCK = 128
MOBA_BLOCK = 256
MOBA_TOPK = 3
N_MEM = 256
H_MEM = 4
HD_MEM = 128
W_MEM = H_MEM * HD_MEM
N_GROUPS = 4
EXPERTS_PER_GROUP = 8
N_EXPERTS = N_GROUPS * EXPERTS_PER_GROUP
TOPK_IN_GROUP = 2
D_EXPERT = 256
ROPE_THETA = 10000.0
RMS_EPS = 1e-6
W_IN_COLS = 3 * W_SB + 3 * W_MB + 2 * D_MODEL

kernel_name = 'hybrid_stickbreak_moba_hmoe_step'


def rms_norm(x, g):
    x32 = x.astype(jnp.float32)
    y = x32 * lax.rsqrt(jnp.mean(x32 * x32, axis=-1, keepdims=True) + RMS_EPS)
    return (y * g.astype(jnp.float32)).astype(x.dtype)


def rope(x, pos):
    half = x.shape[-1] // 2
    inv_freq = ROPE_THETA ** (-jnp.arange(half, dtype=jnp.float32) / half)
    ang = pos.astype(jnp.float32)[:, None] * inv_freq[None, :]
    cos = jnp.cos(ang)[None, :, None, :]
    sin = jnp.sin(ang)[None, :, None, :]
    x32 = x.astype(jnp.float32)
    x1, x2 = x32[..., :half], x32[..., half:]
    return jnp.concatenate([x1 * cos - x2 * sin, x2 * cos + x1 * sin], axis=-1).astype(x.dtype)


def mixer_projections(xn, pos, w_in, b_gate):
    B, T, _ = xn.shape
    proj = xn @ w_in
    cuts = [W_SB, 2 * W_SB, 3 * W_SB, 3 * W_SB + W_MB, 3 * W_SB + 2 * W_MB, 3 * W_SB + 3 * W_MB,
            3 * W_SB + 3 * W_MB + D_MODEL]
    q_sb, k_sb, v_sb, q_mb, k_mb, v_mb, gl_sb, gl_mb = jnp.split(proj, cuts, axis=-1)
    sb = lambda t: t.reshape(B, T, H_SB, HD_SB)
    mb = lambda t: t.reshape(B, T, H_MB, HD_MB)
    g_sb = jax.nn.sigmoid((gl_sb + b_gate[:D_MODEL]).astype(jnp.float32)).astype(xn.dtype)
    g_mb = jax.nn.sigmoid((gl_mb + b_gate[D_MODEL:]).astype(jnp.float32)).astype(xn.dtype)
    return (sb(q_sb), sb(k_sb), sb(v_sb), rope(mb(q_mb), pos), rope(mb(k_mb), pos), mb(v_mb), g_sb, g_mb)


def stick_breaking_attend(q, k, v, q_pos, k_pos):
    z = jnp.einsum('bthd,blhd->bhtl', q, k, preferred_element_type=jnp.float32) * (q.shape[-1] ** -0.5)
    causal = (k_pos[None, :] < q_pos[:, None])[None, None]
    log_keep = jnp.where(causal, -jax.nn.softplus(z), 0.0)
    excl_suffix = lax.cumsum(log_keep, axis=3, reverse=True) - log_keep
    a = jnp.where(causal, jnp.exp(jax.nn.log_sigmoid(z) + excl_suffix), 0.0)
    return jnp.einsum('bhtl,blhd->bthd', a.astype(v.dtype), v)


def over_query_blocks(fn, q):
    B, T, H, d = q.shape
    nqb = T // Q_BLOCK
    q_blocks = q.reshape(B, nqb, Q_BLOCK, H, d).transpose(1, 0, 2, 3, 4)
    pos_blocks = jnp.arange(T, dtype=jnp.int32).reshape(nqb, Q_BLOCK)
    out = lax.map(lambda qp: fn(qp[0], qp[1]), (q_blocks, pos_blocks))
    return out.transpose(1, 0, 2, 3, 4).reshape(B, T, H, out.shape[-1])


def moba_blocks(k, v):
    B, L, H, d = k.shape
    nb = -(-L // MOBA_BLOCK)
    pad = nb * MOBA_BLOCK - L
    k = jnp.pad(k, ((0, 0), (0, pad), (0, 0), (0, 0)))
    v = jnp.pad(v, ((0, 0), (0, pad), (0, 0), (0, 0)))
    kb = k.reshape(B, nb, MOBA_BLOCK, H, d).transpose(0, 3, 1, 2, 4)
    vb = v.reshape(B, nb, MOBA_BLOCK, H, d).transpose(0, 3, 1, 2, 4)
    k_mean = jnp.mean(kb.astype(jnp.float32), axis=3)
    return kb, vb, k_mean


def moba_attend(q, q_pos, kb, vb, k_mean):
    B, T, H, d = q.shape
    nb = kb.shape[2]
    qh = q.transpose(0, 2, 1, 3)
    own = q_pos // MOBA_BLOCK
    gate = jnp.einsum('bhtd,bhnd->bhtn', qh.astype(jnp.float32), k_mean)
    eligible = jnp.arange(nb, dtype=jnp.int32)[None, :] < own[:, None]
    gate = jnp.where(eligible, gate, -jnp.inf)
    _, top_idx = lax.top_k(gate, min(MOBA_TOPK, nb))
    sel_valid = top_idx < own[:, None]
    own_idx = jnp.broadcast_to(own[None, None, :, None], top_idx.shape[:3] + (1,)).astype(top_idx.dtype)
    idx = jnp.concatenate([top_idx, own_idx], axis=-1)
    valid = jnp.concatenate([sel_valid, jnp.ones(own_idx.shape, dtype=bool)], axis=-1)
    b_ix = jnp.arange(B)[:, None, None, None]
    h_ix = jnp.arange(H)[None, :, None, None]
    k_g = kb[b_ix, h_ix, idx]
    v_g = vb[b_ix, h_ix, idx]
    s = jnp.einsum('bhtd,bhtjkd->bhtjk', qh, k_g, preferred_element_type=jnp.float32) * (d ** -0.5)
    key_pos = idx[..., None] * MOBA_BLOCK + jnp.arange(MOBA_BLOCK, dtype=idx.dtype)
    mask = valid[..., None] & (key_pos <= q_pos[None, None, :, None, None])
    s = jnp.where(mask, s, -jnp.inf)
    p = jax.nn.softmax(s.reshape(B, H, T, -1), axis=-1).reshape(s.shape)
    return jnp.einsum('bhtjk,bhtjkd->bthd', p.astype(v_g.dtype), v_g)


def gather_pages(cache, page_table):
    pages = cache[page_table]
    B, NP, PS, H, d = pages.shape
    return pages.reshape(B, NP * PS, H, d)


def merge_branches(o_sb, o_mb, g_sb, g_mb, w_out_sb, w_out_mb, w_out):
    B, T = o_sb.shape[:2]
    h = g_sb * (o_sb.reshape(B, T, W_SB) @ w_out_sb) + g_mb * (o_mb.reshape(B, T, W_MB) @ w_out_mb)
    return h @ w_out


def memory_kv(mem, g, w_kv):
    B, M, _ = mem.shape
    k, v = jnp.split(rms_norm(mem, g) @ w_kv, 2, axis=-1)
    return k.reshape(B, M, H_MEM, HD_MEM), v.reshape(B, M, H_MEM, HD_MEM)


def memory_attend(xn, k, v, w_q, w_o):
    B, T, _ = xn.shape
    q = (xn @ w_q).reshape(B, T, H_MEM, HD_MEM)
    s = jnp.einsum('bthd,bmhd->bhtm', q, k, preferred_element_type=jnp.float32) * (HD_MEM ** -0.5)
    p = jax.nn.softmax(s, axis=-1)
    o = jnp.einsum('bhtm,bmhd->bthd', p.astype(v.dtype), v).reshape(B, T, W_MEM)
    return o @ w_o


def hierarchical_moe(xn, w_rg, b_rg, w_re, b_re, w_g, w_u, w_d):
    B, T, D = xn.shape
    xt = xn.reshape(B * T, D)
    g_logits = jnp.einsum('nd,dg->ng', xt, w_rg, preferred_element_type=jnp.float32) + b_rg.astype(jnp.float32)
    g_w, g_idx = lax.top_k(jax.nn.softmax(g_logits, axis=-1), 1)
    e_logits = jnp.einsum('nd,gde->nge', xt, w_re, preferred_element_type=jnp.float32) + b_re.astype(jnp.float32)
    e_logits = jnp.take_along_axis(e_logits, g_idx[:, :, None], axis=1)[:, 0]
    e_w, e_idx = lax.top_k(jax.nn.softmax(e_logits, axis=-1), TOPK_IN_GROUP)
    e_w = e_w / jnp.sum(e_w, axis=-1, keepdims=True)
    expert_id = g_idx * EXPERTS_PER_GROUP + e_idx
    comb = jnp.sum(jax.nn.one_hot(expert_id, N_EXPERTS, dtype=jnp.float32) * (g_w * e_w)[..., None], axis=1)
    h = jax.nn.silu(jnp.einsum('nd,edf->nef', xt, w_g)) * jnp.einsum('nd,edf->nef', xt, w_u)
    h = h * comb[..., None].astype(h.dtype)
    return jnp.einsum('nef,efd->nd', h, w_d).reshape(B, T, D)


def memory_and_ffn(x1, mem_k, mem_v, norm_mem_g, w_q_mem, w_o_mem, norm_ffn_g,
                   w_router_group, b_router_group, w_router_expert, b_router_expert, w_gate_e, w_up_e, w_down_e):
    x2 = x1 + memory_attend(rms_norm(x1, norm_mem_g), mem_k, mem_v, w_q_mem, w_o_mem)
    return x2 + hierarchical_moe(rms_norm(x2, norm_ffn_g), w_router_group, b_router_group,
                                 w_router_expert, b_router_expert, w_gate_e, w_up_e, w_down_e)


def setup_inputs(seed: int = 0) -> dict:
    key = jax.random.key(seed)
    ks = jax.random.split(key, 32)
    f32 = jnp.float32
    nrm = lambda k, shape: jax.random.normal(k, shape, f32)
    dense = lambda k, shape, fan_in: nrm(k, shape) * (fan_in ** -0.5)
    gain = lambda k, n: 1.0 + 0.05 * nrm(k, (n,))
    n_pages = PAST_LEN // PAGE_SIZE
    n_used = DEC_BATCH * n_pages
    n_pool = n_used + n_used // 4
    page_table = jax.random.permutation(ks[0], n_pool)[:n_used].reshape(DEC_BATCH, n_pages).astype(jnp.int32)
    return {
        'x_prompt': nrm(ks[1], (BATCH, SEQ, D_MODEL)),
        'x_sample': nrm(ks[2], (DEC_BATCH, DEC_SEQ, D_MODEL)),
        'mem_prompt': nrm(ks[3], (BATCH, N_MEM, D_MODEL)),
        'cache_sb_k': nrm(ks[4], (n_pool, PAGE_SIZE, H_SB, HD_SB)),
        'cache_sb_v': nrm(ks[5], (n_pool, PAGE_SIZE, H_SB, HD_SB)),
        'cache_mb_k': nrm(ks[6], (n_pool, PAGE_SIZE, H_MB, HD_MB)),
        'cache_mb_v': nrm(ks[7], (n_pool, PAGE_SIZE, H_MB, HD_MB)),
        'cache_mem_k': nrm(ks[8], (DEC_BATCH, N_MEM, H_MEM, HD_MEM)),
        'cache_mem_v': nrm(ks[9], (DEC_BATCH, N_MEM, H_MEM, HD_MEM)),
        'page_table': page_table,
        'norm_mix_g': gain(ks[10], D_MODEL),
        'w_in': dense(ks[11], (D_MODEL, W_IN_COLS), D_MODEL),
        'b_gate': 0.01 * nrm(ks[12], (2 * D_MODEL,)),
        'w_out_sb': dense(ks[13], (W_SB, D_MODEL), W_SB),
        'w_out_mb': dense(ks[14], (W_MB, D_MODEL), W_MB),
        'w_out': dense(ks[15], (D_MODEL, D_MODEL), D_MODEL),
        'norm_mem_g': gain(ks[16], D_MODEL),
        'norm_memsrc_g': gain(ks[17], D_MODEL),
        'w_q_mem': dense(ks[18], (D_MODEL, W_MEM), D_MODEL),
        'w_kv_mem': dense(ks[19], (D_MODEL, 2 * W_MEM), D_MODEL),
        'w_o_mem': dense(ks[20], (W_MEM, D_MODEL), W_MEM),
        'norm_ffn_g': gain(ks[21], D_MODEL),
        'w_router_group': dense(ks[22], (D_MODEL, N_GROUPS), D_MODEL),
        'b_router_group': 0.01 * nrm(ks[23], (N_GROUPS,)),
        'w_router_expert': dense(ks[24], (N_GROUPS, D_MODEL, EXPERTS_PER_GROUP), D_MODEL),
        'b_router_expert': 0.01 * nrm(ks[25], (N_GROUPS, EXPERTS_PER_GROUP)),
        'w_gate_e': dense(ks[26], (N_EXPERTS, D_MODEL, D_EXPERT), D_MODEL),
        'w_up_e': dense(ks[27], (N_EXPERTS, D_MODEL, D_EXPERT), D_MODEL),
        'w_down_e': dense(ks[28], (N_EXPERTS, D_EXPERT, D_MODEL), D_EXPERT),
        'norm_final_g': gain(ks[29], D_MODEL),
    }


def reference(x_prompt, x_sample, mem_prompt, cache_sb_k, cache_sb_v, cache_mb_k, cache_mb_v,
              cache_mem_k, cache_mem_v, page_table,
              norm_mix_g, w_in, b_gate, w_out_sb, w_out_mb, w_out,
              norm_mem_g, norm_memsrc_g, w_q_mem, w_kv_mem, w_o_mem,
              norm_ffn_g, w_router_group, b_router_group, w_router_expert, b_router_expert,
              w_gate_e, w_up_e, w_down_e, norm_final_g):
    tail = lambda x1, mk, mv: memory_and_ffn(x1, mk, mv, norm_mem_g, w_q_mem, w_o_mem, norm_ffn_g,
                                             w_router_group, b_router_group, w_router_expert, b_router_expert,
                                             w_gate_e, w_up_e, w_down_e)

    pos_p = jnp.arange(x_prompt.shape[1], dtype=jnp.int32)
    xn_p = rms_norm(x_prompt, norm_mix_g)
    (q_sb_p, sb_k_prompt, sb_v_prompt, q_mb_p, mb_k_prompt, mb_v_prompt,
     g_sb_p, g_mb_p) = mixer_projections(xn_p, pos_p, w_in, b_gate)
    o_sb_p = over_query_blocks(lambda qb, pb: stick_breaking_attend(qb, sb_k_prompt, sb_v_prompt, pb, pos_p), q_sb_p)
    kb_p, vb_p, km_p = moba_blocks(mb_k_prompt, mb_v_prompt)
    o_mb_p = over_query_blocks(lambda qb, pb: moba_attend(qb, pb, kb_p, vb_p, km_p), q_mb_p)
    x1_p = x_prompt + merge_branches(o_sb_p, o_mb_p, g_sb_p, g_mb_p, w_out_sb, w_out_mb, w_out)
    mem_k_prompt, mem_v_prompt = memory_kv(mem_prompt, norm_memsrc_g, w_kv_mem)
    y_prompt = rms_norm(tail(x1_p, mem_k_prompt, mem_v_prompt), norm_final_g)

    past_len = (PAST_LEN // PAGE_SIZE) * PAGE_SIZE
    n_new = x_sample.shape[1]
    pos_s = past_len + jnp.arange(n_new, dtype=jnp.int32)
    k_pos_s = jnp.arange(past_len + n_new, dtype=jnp.int32)
    xn_s = rms_norm(x_sample, norm_mix_g)
    (q_sb_s, sb_k_sample, sb_v_sample, q_mb_s, mb_k_sample, mb_v_sample,
     g_sb_s, g_mb_s) = mixer_projections(xn_s, pos_s, w_in, b_gate)
    sb_k_all = jnp.concatenate([gather_pages(cache_sb_k, page_table), sb_k_sample], axis=1)
    sb_v_all = jnp.concatenate([gather_pages(cache_sb_v, page_table), sb_v_sample], axis=1)
    o_sb_s = stick_breaking_attend(q_sb_s, sb_k_all, sb_v_all, pos_s, k_pos_s)
    mb_k_all = jnp.concatenate([gather_pages(cache_mb_k, page_table), mb_k_sample], axis=1)
    mb_v_all = jnp.concatenate([gather_pages(cache_mb_v, page_table), mb_v_sample], axis=1)
    kb_s, vb_s, km_s = moba_blocks(mb_k_all, mb_v_all)
    o_mb_s = moba_attend(q_mb_s, pos_s, kb_s, vb_s, km_s)
    x1_s = x_sample + merge_branches(o_sb_s, o_mb_s, g_sb_s, g_mb_s, w_out_sb, w_out_mb, w_out)
    y_sample = rms_norm(tail(x1_s, cache_mem_k, cache_mem_v), norm_final_g)

    return (y_prompt, y_sample, sb_k_prompt, sb_v_prompt, mb_k_prompt, mb_v_prompt, mem_k_prompt, mem_v_prompt,
            sb_k_sample, sb_v_sample, mb_k_sample, mb_v_sample)
```

```python
import functools

import jax
import jax.numpy as jnp
from jax import lax
from jax.experimental import pallas as pl
from jax.experimental.pallas import tpu as pltpu

F32 = jnp.float32
BF16 = jnp.bfloat16
I32 = jnp.int32

D_MODEL = 1024
N_HEADS = 8
HEAD_DIM = 64
W_MIX = N_HEADS * HEAD_DIM
PAGE_SIZE = 128
MOBA_BLOCK = 256
MOBA_TOPK = 3
H_MEM = 4
HD_MEM = 128
W_MEM = H_MEM * HD_MEM
N_GROUPS = 4
EXPERTS_PER_GROUP = 8
N_EXPERTS = N_GROUPS * EXPERTS_PER_GROUP
D_EXPERT = 256
ROPE_THETA = 10000.0
RMS_EPS = 1e-6
W_IN_COLS = 6 * W_MIX + 2 * D_MODEL

LANES = 128
HEADS_PER_STEP = 4
QUAD = HEADS_PER_STEP * HEAD_DIM
SB_DEAD = -160.0
NEG_INF = float("-inf")
MIB = 1024 * 1024


def _cparams(semantics, vmem_mib):
    return pltpu.CompilerParams(dimension_semantics=semantics, vmem_limit_bytes=vmem_mib * MIB)


def _rms(x, g):
    ms = jnp.mean(x * x, axis=-1, keepdims=True)
    return (x * lax.rsqrt(ms + RMS_EPS)) * g


def _dot(a, b):
    return jnp.dot(a, b, preferred_element_type=F32)


def _dot_nt(a, b):
    return lax.dot_general(a, b, (((1,), (1,)), ((), ())), preferred_element_type=F32)


def _split_bf16(x):
    hi = x.astype(BF16)
    lo = (x - hi.astype(F32)).astype(BF16)
    return hi, lo


def _later_keys(n):
    return (lax.broadcasted_iota(I32, (n, n), 0) > lax.broadcasted_iota(I32, (n, n), 1)).astype(BF16)


def _stick_breaking_tile(z, later, carried, valid):
    l1p = jnp.log1p(jnp.exp(-jnp.abs(z)))
    log_keep = -(jnp.maximum(z, 0.0) + l1p)
    if valid is not None:
        log_keep = jnp.where(valid, log_keep, 0.0)
    hi, lo = _split_bf16(log_keep)
    inner = _dot(hi, later) + _dot(lo, later)
    a = jnp.exp(jnp.minimum(z, 0.0) - l1p + carried + inner)
    if valid is not None:
        a = jnp.where(valid, a, 0.0)
    return a, carried + inner[:, 0:1] + log_keep[:, 0:1]


def _proj_kernel(x_ref, g_ref, w_ref, bg_ref, cos_ref, sin_ref,
                 qsb_ref, ksb_ref, vsb_ref, qmb_ref, kmb_ref, vmb_ref, gate_ref,
                 ksb16_ref, vsb16_ref, kmb16_ref, vmb16_ref, kmean_ref):
    xb = _rms(x_ref[...], g_ref[...]).astype(BF16)

    def seg(lo, width):
        return _dot(xb, w_ref[:, lo:lo + width])

    lane = lax.broadcasted_iota(I32, (1, W_MIX), 1)
    first_half = (lane % HEAD_DIM) < (HEAD_DIM // 2)
    cos = cos_ref[...]
    sin = sin_ref[...]

    def rope(t):
        partner = jnp.where(first_half, pltpu.roll(t, W_MIX - HEAD_DIM // 2, 1),
                            pltpu.roll(t, HEAD_DIM // 2, 1))
        return t * cos + partner * sin

    scale = HEAD_DIM ** -0.5
    qsb_ref[...] = (seg(0, W_MIX) * scale).astype(BF16)
    k = seg(W_MIX, W_MIX)
    ksb_ref[...] = k
    ksb16_ref[...] = k.astype(BF16)
    v = seg(2 * W_MIX, W_MIX)
    vsb_ref[...] = v
    vsb16_ref[...] = v.astype(BF16)
    qmb_ref[...] = (rope(seg(3 * W_MIX, W_MIX)) * scale).astype(BF16)
    k = rope(seg(4 * W_MIX, W_MIX))
    kmb_ref[...] = k
    kmb16_ref[...] = k.astype(BF16)
    kmean_ref[0] = jnp.mean(k, axis=0, keepdims=True)
    v = seg(5 * W_MIX, W_MIX)
    vmb_ref[...] = v
    vmb16_ref[...] = v.astype(BF16)
    gl = seg(6 * W_MIX, 2 * D_MODEL) + bg_ref[...]
    gate_ref[...] = 1.0 / (1.0 + jnp.exp(-gl))


def _proj(x, g, w_in16, b_gate, cos, sin, tm):
    n = x.shape[0]
    nt = n // tm
    n_rope = cos.shape[0] // tm
    row = lambda i: (i, 0)
    const = lambda i: (0, 0)
    out_shape = (
        [jax.ShapeDtypeStruct((n, W_MIX), BF16)] + [jax.ShapeDtypeStruct((n, W_MIX), F32)] * 2
        + [jax.ShapeDtypeStruct((n, W_MIX), BF16)] + [jax.ShapeDtypeStruct((n, W_MIX), F32)] * 2
        + [jax.ShapeDtypeStruct((n, 2 * D_MODEL), F32)]
        + [jax.ShapeDtypeStruct((n, W_MIX), BF16)] * 4
        + [jax.ShapeDtypeStruct((nt, 1, W_MIX), F32)])
    out_specs = ([pl.BlockSpec((tm, W_MIX), row)] * 6 + [pl.BlockSpec((tm, 2 * D_MODEL), row)]
                 + [pl.BlockSpec((tm, W_MIX), row)] * 4 + [pl.BlockSpec((1, 1, W_MIX), lambda i: (i, 0, 0))])
    return pl.pallas_call(
        _proj_kernel,
        grid=(nt,),
        in_specs=[pl.BlockSpec((tm, D_MODEL), row),
                  pl.BlockSpec((1, D_MODEL), const),
                  pl.BlockSpec((D_MODEL, W_IN_COLS), const),
                  pl.BlockSpec((1, 2 * D_MODEL), const),
                  pl.BlockSpec((tm, W_MIX), lambda i: (i % n_rope, 0)),
                  pl.BlockSpec((tm, W_MIX), lambda i: (i % n_rope, 0))],
        out_specs=out_specs,
        out_shape=out_shape,
        compiler_params=_cparams(("parallel",), 48),
        name="proj",
    )(x, g, w_in16, b_gate, cos, sin)


def _head_masks(width):
    lane_head = lax.broadcasted_iota(I32, (1, width), 1) // HEAD_DIM
    return [lane_head == h for h in range(HEADS_PER_STEP)]


def _sb_prompt_kernel(q_ref, k_ref, v_ref, o_ref, *, tq):
    i = pl.program_id(2)
    q = q_ref[...]
    masks = _head_masks(QUAD)
    zero16 = jnp.zeros_like(q)
    qs = [jnp.where(m, q, zero16) for m in masks]
    row = lax.broadcasted_iota(I32, (tq, tq), 0)
    col = lax.broadcasted_iota(I32, (tq, tq), 1)
    later = _later_keys(tq)

    def block(kb, s_run, acc):
        kblk = k_ref[pl.ds(pl.multiple_of(kb * tq, tq), tq), :]
        vblk = v_ref[pl.ds(pl.multiple_of(kb * tq, tq), tq), :]
        causal = jnp.logical_or(kb < i, col < row)
        new_s = []
        for h in range(HEADS_PER_STEP):
            a, s_new = _stick_breaking_tile(_dot_nt(qs[h], kblk), later, s_run[h], causal)
            pv = _dot(a.astype(BF16), vblk)
            acc = jnp.where(masks[h], acc + pv, acc)
            new_s.append(s_new)
        return new_s, acc

    def cond(carry):
        kb, alive = carry[0], carry[1]
        return jnp.logical_and(kb >= 0, alive > 0)

    def body(carry):
        kb, _, s0, s1, s2, s3, acc = carry
        new_s, acc = block(kb, [s0, s1, s2, s3], acc)
        top = jnp.maximum(jnp.maximum(new_s[0], new_s[1]), jnp.maximum(new_s[2], new_s[3]))
        alive = (jnp.max(top) > SB_DEAD).astype(I32)
        return (kb - 1, alive, new_s[0], new_s[1], new_s[2], new_s[3], acc)

    zs = jnp.zeros((tq, 1), F32)
    init = (i, jnp.int32(1), zs, zs, zs, zs, jnp.zeros((tq, QUAD), F32))
    out = lax.while_loop(cond, body, init)
    o_ref[...] = out[-1].astype(o_ref.dtype)


def _sb_prompt(q16, k16, v16, batch, seq, tq):
    nq = seq // tq
    ng = W_MIX // QUAD
    return pl.pallas_call(
        functools.partial(_sb_prompt_kernel, tq=tq),
        grid=(batch, ng, nq),
        in_specs=[pl.BlockSpec((tq, QUAD), lambda b, g, i: (b * nq + i, g)),
                  pl.BlockSpec((seq, QUAD), lambda b, g, i: (b, g)),
                  pl.BlockSpec((seq, QUAD), lambda b, g, i: (b, g))],
        out_specs=pl.BlockSpec((tq, QUAD), lambda b, g, i: (b * nq + i, g)),
        out_shape=jax.ShapeDtypeStruct((batch * seq, W_MIX), BF16),
        compiler_params=_cparams(("parallel", "parallel", "arbitrary"), 48),
        name="sb_prompt",
    )(q16, k16, v16)


def _first_argmax(x, lane, width):
    mx = jnp.max(x, axis=1, keepdims=True)
    idx = jnp.min(jnp.where(x == mx, lane, width), axis=1, keepdims=True)
    return mx, idx


def _top_blocks(gate, eligible, blk):
    g = jnp.where(eligible, gate, NEG_INF)
    sel = jnp.zeros(gate.shape, jnp.bool_)
    for _ in range(MOBA_TOPK):
        _, first = _first_argmax(g, blk, gate.shape[1])
        pick = blk == first
        sel = jnp.logical_or(sel, jnp.logical_and(pick, eligible))
        g = jnp.where(pick, NEG_INF, g)
    return sel


def _moba_prompt_kernel(q_ref, k_ref, v_ref, km_ref, o_ref, *, tq):
    i = pl.program_id(2)
    q = q_ref[...]
    masks = _head_masks(QUAD)
    zero16 = jnp.zeros_like(q)
    qs = [jnp.where(m, q, zero16) for m in masks]
    nb = km_ref.shape[0]
    km_hi, km_lo = _split_bf16(km_ref[...])
    blk = lax.broadcasted_iota(I32, (1, nb), 1)
    sel_f = []
    for h in range(HEADS_PER_STEP):
        gate = _dot_nt(qs[h], km_hi) + _dot_nt(qs[h], km_lo)
        sel_f.append(_top_blocks(gate, blk < i, blk).astype(F32))
    row = lax.broadcasted_iota(I32, (tq, tq), 0)
    col = lax.broadcasted_iota(I32, (tq, tq), 1)

    def load(kb):
        start = pl.multiple_of(kb * tq, tq)
        return k_ref[pl.ds(start, tq), :], v_ref[pl.ds(start, tq), :]

    kblk, vblk = load(i)
    m_run, l_run = [], []
    acc = jnp.zeros((tq, QUAD), F32)
    for h in range(HEADS_PER_STEP):
        s = jnp.where(col <= row, _dot_nt(qs[h], kblk), NEG_INF)
        m = jnp.max(s, axis=1, keepdims=True)
        p = jnp.exp(s - m)
        m_run.append(m)
        l_run.append(jnp.sum(p, axis=1, keepdims=True))
        acc = jnp.where(masks[h], _dot(p.astype(BF16), vblk), acc)

    def body(kb, carry):
        m_run, l_run, acc = list(carry[0:4]), list(carry[4:8]), carry[8]
        kblk, vblk = load(kb)
        new_m, new_l = [], []
        for h in range(HEADS_PER_STEP):
            chosen = jnp.sum(jnp.where(blk == kb, sel_f[h], 0.0), axis=1, keepdims=True) > 0.0
            s = jnp.where(chosen, _dot_nt(qs[h], kblk), NEG_INF)
            m = jnp.maximum(m_run[h], jnp.max(s, axis=1, keepdims=True))
            alpha = jnp.exp(m_run[h] - m)
            p = jnp.exp(s - m)
            new_m.append(m)
            new_l.append(alpha * l_run[h] + jnp.sum(p, axis=1, keepdims=True))
            acc = jnp.where(masks[h], alpha * acc + _dot(p.astype(BF16), vblk), acc)
        return tuple(new_m) + tuple(new_l) + (acc,)

    out = lax.fori_loop(0, i, body, tuple(m_run) + tuple(l_run) + (acc,))
    l_run, acc = out[4:8], out[8]
    inv = jnp.zeros((tq, QUAD), F32)
    for h in range(HEADS_PER_STEP):
        inv = jnp.where(masks[h], 1.0 / l_run[h], inv)
    o_ref[...] = (acc * inv).astype(o_ref.dtype)


def _moba_prompt(q16, k16, v16, kmean, batch, seq):
    tq = MOBA_BLOCK
    nq = seq // tq
    ng = W_MIX // QUAD
    return pl.pallas_call(
        functools.partial(_moba_prompt_kernel, tq=tq),
        grid=(batch, ng, nq),
        in_specs=[pl.BlockSpec((tq, QUAD), lambda b, g, i: (b * nq + i, g)),
                  pl.BlockSpec((seq, QUAD), lambda b, g, i: (b, g)),
                  pl.BlockSpec((seq, QUAD), lambda b, g, i: (b, g)),
                  pl.BlockSpec((nq, QUAD), lambda b, g, i: (b, g))],
        out_specs=pl.BlockSpec((tq, QUAD), lambda b, g, i: (b * nq + i, g)),
        out_shape=jax.ShapeDtypeStruct((batch * seq, W_MIX), BF16),
        compiler_params=_cparams(("parallel", "parallel", "arbitrary"), 48),
        name="moba_prompt",
    )(q16, k16, v16, kmean)


def _mem_kv_kernel(mem_ref, g_ref, w_ref, k_ref, v_ref):
    kv = _dot(_rms(mem_ref[...], g_ref[...]).astype(BF16), w_ref[...])
    k_ref[...] = kv[:, :W_MEM]
    v_ref[...] = kv[:, W_MEM:]


def _mem_kv(mem, g, w_kv16):
    n = mem.shape[0]
    return pl.pallas_call(
        _mem_kv_kernel,
        out_shape=[jax.ShapeDtypeStruct((n, W_MEM), F32)] * 2,
        name="mem_kv",
    )(mem, g, w_kv16)


def _dot3(x, w_hi, w_lo):
    x_hi, x_lo = _split_bf16(x)
    return _dot(x_hi, w_hi) + (_dot(x_hi, w_lo) + _dot(x_lo, w_hi))


def _post_kernel(x_ref, osb_ref, omb_ref, gate_ref, wsb_ref, wmb_ref, wo_ref,
                 gmem_ref, wq_ref, mk_ref, mv_ref, wom_ref, gffn_ref,
                 wrg_hi_ref, wrg_lo_ref, brg_ref, wre_hi_ref, wre_lo_ref, bre_ref,
                 x2_ref, xn_ref, comb_ref, *, rows_per_mem, keys_per_mem):
    gate = gate_ref[...]
    h = gate[:, :D_MODEL] * _dot(osb_ref[...], wsb_ref[...]) + gate[:, D_MODEL:] * _dot(omb_ref[...], wmb_ref[...])
    x1 = x_ref[...] + _dot(h.astype(BF16), wo_ref[...])

    q = (_dot(_rms(x1, gmem_ref[...]).astype(BF16), wq_ref[...]) * (HD_MEM ** -0.5)).astype(BF16)
    heads = []
    if rows_per_mem is not None:
        shape = (x1.shape[0], mk_ref.shape[1])
        same_mem = (lax.broadcasted_iota(I32, shape, 0) // rows_per_mem
                    == lax.broadcasted_iota(I32, shape, 1) // keys_per_mem)
    for hh in range(H_MEM):
        sl = slice(hh * HD_MEM, (hh + 1) * HD_MEM)
        s = _dot_nt(q[:, sl], mk_ref[0, :, sl])
        if rows_per_mem is not None:
            s = jnp.where(same_mem, s, NEG_INF)
        p = jnp.exp(s - jnp.max(s, axis=1, keepdims=True))
        p = p / jnp.sum(p, axis=1, keepdims=True)
        heads.append(_dot(p.astype(BF16), mv_ref[0, :, sl]))
    o = jnp.concatenate(heads, axis=1)
    x2 = x1 + _dot(o.astype(BF16), wom_ref[...])
    x2_ref[...] = x2

    xn = _rms(x2, gffn_ref[...])
    xn_ref[...] = xn.astype(BF16)

    lane = lax.broadcasted_iota(I32, (1, LANES), 1)
    gl = jnp.where(lane < N_GROUPS, _dot3(xn, wrg_hi_ref[...], wrg_lo_ref[...]) + brg_ref[...], NEG_INF)
    g_max, g_idx = _first_argmax(gl, lane, LANES)
    g_w = 1.0 / jnp.sum(jnp.exp(gl - g_max), axis=1, keepdims=True)
    el = _dot3(xn, wre_hi_ref[...], wre_lo_ref[...]) + bre_ref[...]
    el = jnp.where((lane // EXPERTS_PER_GROUP) == g_idx, el, NEG_INF)
    e_max, i1 = _first_argmax(el, lane, LANES)
    e_sum = jnp.sum(jnp.exp(el - e_max), axis=1, keepdims=True)
    el2 = jnp.where(lane == i1, NEG_INF, el)
    e_max2, i2 = _first_argmax(el2, lane, LANES)
    w1 = 1.0 / e_sum
    w2 = jnp.exp(e_max2 - e_max) / e_sum
    norm = w1 + w2
    comb_ref[...] = jnp.where(lane == i1, g_w * (w1 / norm), 0.0) + jnp.where(lane == i2, g_w * (w2 / norm), 0.0)


def _post(x, osb, omb, gate, mem_k16, mem_v16, rows_per_batch, tm, w, rows_per_mem=None, keys_per_mem=None):
    n = x.shape[0]
    nt = n // tm
    tiles_per_batch = rows_per_batch // tm
    row = lambda i: (i, 0)
    const = lambda i: (0, 0)
    mem = lambda i: (i // tiles_per_batch, 0, 0)
    n_mem = mem_k16.shape[1]
    full = lambda a: pl.BlockSpec(a.shape, const)
    return pl.pallas_call(
        functools.partial(_post_kernel, rows_per_mem=rows_per_mem, keys_per_mem=keys_per_mem),
        grid=(nt,),
        in_specs=[pl.BlockSpec((tm, D_MODEL), row), pl.BlockSpec((tm, W_MIX), row), pl.BlockSpec((tm, W_MIX), row),
                  pl.BlockSpec((tm, 2 * D_MODEL), row),
                  full(w["w_out_sb"]), full(w["w_out_mb"]), full(w["w_out"]),
                  full(w["norm_mem_g"]), full(w["w_q_mem"]),
                  pl.BlockSpec((1, n_mem, W_MEM), mem), pl.BlockSpec((1, n_mem, W_MEM), mem),
                  full(w["w_o_mem"]), full(w["norm_ffn_g"]),
                  full(w["wrg_hi"]), full(w["wrg_lo"]), full(w["brg"]),
                  full(w["wre_hi"]), full(w["wre_lo"]), full(w["bre"])],
        out_specs=[pl.BlockSpec((tm, D_MODEL), row), pl.BlockSpec((tm, D_MODEL), row), pl.BlockSpec((tm, LANES), row)],
        out_shape=[jax.ShapeDtypeStruct((n, D_MODEL), F32), jax.ShapeDtypeStruct((n, D_MODEL), BF16),
                   jax.ShapeDtypeStruct((n, LANES), F32)],
        compiler_params=_cparams(("parallel",), 48),
        name="post",
    )(x, osb, omb, gate, w["w_out_sb"], w["w_out_mb"], w["w_out"], w["norm_mem_g"], w["w_q_mem"],
      mem_k16, mem_v16, w["w_o_mem"], w["norm_ffn_g"],
      w["wrg_hi"], w["wrg_lo"], w["brg"], w["wre_hi"], w["wre_lo"], w["bre"])


def _moe_kernel(xn_ref, comb_ref, x2_ref, wg_ref, wu_ref, wd_ref, gfin_ref, y_ref, acc_ref):
    e = pl.program_id(1)

    @pl.when(e == 0)
    def _():
        acc_ref[...] = jnp.zeros_like(acc_ref)

    xn = xn_ref[...]
    hg = _dot(xn, wg_ref[0])
    hu = _dot(xn, wu_ref[0])
    lane = lax.broadcasted_iota(I32, (1, LANES), 1)
    weight = jnp.sum(jnp.where(lane == e, comb_ref[...], 0.0), axis=1, keepdims=True)
    hidden = (hg / (1.0 + jnp.exp(-hg))) * hu * weight
    acc_ref[...] += _dot(hidden.astype(BF16), wd_ref[0])

    @pl.when(e == pl.num_programs(1) - 1)
    def _():
        y_ref[...] = _rms(x2_ref[...] + acc_ref[...], gfin_ref[...])


def _moe(xn16, comb, x2, wg16, wu16, wd16, gfin, tm):
    n = xn16.shape[0]
    nt = n // tm
    row = lambda i, e: (i, 0)
    exp_w = lambda i, e: (e, 0, 0)
    return pl.pallas_call(
        _moe_kernel,
        grid=(nt, N_EXPERTS),
        in_specs=[pl.BlockSpec((tm, D_MODEL), row), pl.BlockSpec((tm, LANES), row), pl.BlockSpec((tm, D_MODEL), row),
                  pl.BlockSpec((1, D_MODEL, D_EXPERT), exp_w), pl.BlockSpec((1, D_MODEL, D_EXPERT), exp_w),
                  pl.BlockSpec((1, D_EXPERT, D_MODEL), exp_w),
                  pl.BlockSpec((1, D_MODEL), lambda i, e: (0, 0))],
        out_specs=pl.BlockSpec((tm, D_MODEL), row),
        out_shape=jax.ShapeDtypeStruct((n, D_MODEL), F32),
        scratch_shapes=[pltpu.VMEM((tm, D_MODEL), F32)],
        compiler_params=_cparams(("parallel", "arbitrary"), 48),
        name="moe",
    )(xn16, comb, x2, wg16, wu16, wd16, gfin)


def _own_head_block(full, n_new):
    row_head = lax.broadcasted_iota(I32, (full.shape[0], 1), 0) // n_new
    out = jnp.zeros((full.shape[0], HEAD_DIM), F32)
    for h in range(N_HEADS):
        out = jnp.where(row_head == h, full[:, h * HEAD_DIM:(h + 1) * HEAD_DIM], out)
    return out


def _sb_sample_kernel(pt_ref, q_ref, kn_ref, vn_ref, kc_ref, vc_ref, o_ref, run_ref, acc_ref, alive_ref, *, n_new):
    p = pl.program_id(1)
    q = q_ref[0]
    rows = q.shape[0]
    later = _later_keys(PAGE_SIZE)

    @pl.when(p == 0)
    def _():
        slot = lax.broadcasted_iota(I32, (rows, PAGE_SIZE), 1)
        tok = lax.broadcasted_iota(I32, (rows, PAGE_SIZE), 0) % n_new
        a, run = _stick_breaking_tile(_dot(q, kn_ref[0].astype(BF16)), later, jnp.zeros((rows, 1), F32), slot < tok)
        acc_ref[...] = _dot_nt(a.astype(BF16), vn_ref[0].astype(BF16))
        run_ref[...] = run
        alive_ref[0] = 1

    @pl.when(alive_ref[0] > 0)
    def _():
        z = _dot(q, kc_ref[0].reshape(W_MIX, PAGE_SIZE).astype(BF16))
        a, run = _stick_breaking_tile(z, later, run_ref[...], None)
        acc_ref[...] += _dot_nt(a.astype(BF16), vc_ref[0].reshape(W_MIX, PAGE_SIZE).astype(BF16))
        run_ref[...] = run
        alive_ref[0] = (jnp.max(run) > SB_DEAD).astype(I32)

    @pl.when(p == pl.num_programs(1) - 1)
    def _():
        o_ref[0] = _own_head_block(acc_ref[...], n_new)


def _sample_specs(rows, index_page):
    per_b = lambda b, p, pt: (b, 0, 0)
    return ([pl.BlockSpec((1, rows, W_MIX), per_b),
             pl.BlockSpec((1, W_MIX, PAGE_SIZE), per_b),
             pl.BlockSpec((1, W_MIX, PAGE_SIZE), per_b),
             pl.BlockSpec((1, N_HEADS, HEAD_DIM, PAGE_SIZE), index_page),
             pl.BlockSpec((1, N_HEADS, HEAD_DIM, PAGE_SIZE), index_page)],
            pl.BlockSpec((1, rows, HEAD_DIM), per_b))


def _sb_sample(page_table, q_bd, kn_t, vn_t, cache_kt, cache_vt, n_new):
    nb, n_pages = page_table.shape
    rows = N_HEADS * n_new
    in_specs, out_spec = _sample_specs(rows, lambda b, p, pt: (pt[b, n_pages - 1 - p], 0, 0, 0))
    grid_spec = pltpu.PrefetchScalarGridSpec(
        num_scalar_prefetch=1, grid=(nb, n_pages), in_specs=in_specs, out_specs=out_spec,
        scratch_shapes=[pltpu.VMEM((rows, 1), F32), pltpu.VMEM((rows, W_MIX), F32), pltpu.SMEM((1,), I32)])
    return pl.pallas_call(
        functools.partial(_sb_sample_kernel, n_new=n_new),
        grid_spec=grid_spec,
        out_shape=jax.ShapeDtypeStruct((nb, rows, HEAD_DIM), F32),
        compiler_params=_cparams(("parallel", "arbitrary"), 32),
        name="sb_sample",
    )(page_table, q_bd, kn_t, vn_t, cache_kt, cache_vt)


def _moba_sample_kernel(pt_ref, q_ref, kn_ref, vn_ref, kc_ref, vc_ref, o_ref,
                        pm_ref, pl_ref, pg_ref, po_ref, *, n_new):
    p = pl.program_id(1)
    n_pages = pl.num_programs(1)
    q = q_ref[0]
    rows = q.shape[0]
    page_lane = lax.broadcasted_iota(I32, (1, LANES), 1)

    @pl.when(p == 0)
    def _():
        pm_ref[...] = jnp.full(pm_ref.shape, NEG_INF, F32)
        pl_ref[...] = jnp.zeros_like(pl_ref)
        pg_ref[...] = jnp.zeros_like(pg_ref)

    z = _dot(q, kc_ref[0].reshape(W_MIX, PAGE_SIZE).astype(BF16))
    m = jnp.max(z, axis=1, keepdims=True)
    pe = jnp.exp(z - m)
    here = page_lane == p
    pm_ref[...] = jnp.where(here, m, pm_ref[...])
    pl_ref[...] = jnp.where(here, jnp.sum(pe, axis=1, keepdims=True), pl_ref[...])
    pg_ref[...] = jnp.where(here, jnp.sum(z, axis=1, keepdims=True), pg_ref[...])
    po_ref[p] = _dot_nt(pe.astype(BF16), vc_ref[0].reshape(W_MIX, PAGE_SIZE).astype(BF16))

    @pl.when(p == n_pages - 1)
    def _():
        pages_per_block = MOBA_BLOCK // PAGE_SIZE
        g = pg_ref[...]
        gate = g + pltpu.roll(g, LANES - 1, 1)
        first_page = jnp.logical_and(page_lane % pages_per_block == 0, page_lane < n_pages)
        sel_f = _top_blocks(gate, first_page, page_lane).astype(F32)
        sel_pages = (sel_f + pltpu.roll(sel_f, 1, 1)) > 0.0

        slot = lax.broadcasted_iota(I32, (rows, PAGE_SIZE), 1)
        tok = lax.broadcasted_iota(I32, (rows, PAGE_SIZE), 0) % n_new
        zn = jnp.where(slot <= tok, _dot(q, kn_ref[0].astype(BF16)), NEG_INF)
        pm = pm_ref[...]
        m_all = jnp.maximum(jnp.max(jnp.where(sel_pages, pm, NEG_INF), axis=1, keepdims=True),
                            jnp.max(zn, axis=1, keepdims=True))
        pn = jnp.exp(zn - m_all)
        w = jnp.where(sel_pages, jnp.exp(pm - m_all), 0.0)
        total = jnp.sum(w * pl_ref[...], axis=1, keepdims=True) + jnp.sum(pn, axis=1, keepdims=True)
        acc = _dot_nt(pn.astype(BF16), vn_ref[0].astype(BF16))
        for pg in range(po_ref.shape[0]):
            acc = acc + w[:, pg:pg + 1] * po_ref[pg]
        o_ref[0] = _own_head_block(acc, n_new) / total


def _moba_sample(page_table, q_bd, kn_t, vn_t, cache_kt, cache_vt, n_new):
    nb, n_pages = page_table.shape
    rows = N_HEADS * n_new
    in_specs, out_spec = _sample_specs(rows, lambda b, p, pt: (pt[b, p], 0, 0, 0))
    grid_spec = pltpu.PrefetchScalarGridSpec(
        num_scalar_prefetch=1, grid=(nb, n_pages), in_specs=in_specs, out_specs=out_spec,
        scratch_shapes=[pltpu.VMEM((rows, LANES), F32), pltpu.VMEM((rows, LANES), F32),
                        pltpu.VMEM((rows, LANES), F32), pltpu.VMEM((n_pages, rows, W_MIX), F32)])
    return pl.pallas_call(
        functools.partial(_moba_sample_kernel, n_new=n_new),
        grid_spec=grid_spec,
        out_shape=jax.ShapeDtypeStruct((nb, rows, HEAD_DIM), F32),
        compiler_params=_cparams(("parallel", "arbitrary"), 32),
        name="moba_sample",
    )(page_table, q_bd, kn_t, vn_t, cache_kt, cache_vt)


def _rope_tables(pos):
    half = HEAD_DIM // 2
    inv_freq = ROPE_THETA ** (-jnp.arange(half, dtype=F32) / half)
    ang = pos.astype(F32)[:, None] * inv_freq[None, :]
    cos = jnp.cos(ang)
    sin = jnp.sin(ang)
    return (jnp.tile(jnp.concatenate([cos, cos], axis=1), (1, N_HEADS)),
            jnp.tile(jnp.concatenate([-sin, sin], axis=1), (1, N_HEADS)))


def _pad_lanes(a):
    return jnp.pad(a, ((0, 0), (0, LANES - a.shape[1])))


def _prepare_weights(norm_mix_g, w_in, b_gate, w_out_sb, w_out_mb, w_out, norm_mem_g, norm_memsrc_g, w_q_mem,
                     w_kv_mem, w_o_mem, norm_ffn_g, w_router_group, b_router_group, w_router_expert,
                     b_router_expert, w_gate_e, w_up_e, w_down_e, norm_final_g):
    row = lambda v: v.reshape(1, -1).astype(F32)
    wrg_hi, wrg_lo = _split_bf16(_pad_lanes(w_router_group))
    w_re = w_router_expert.transpose(1, 0, 2).reshape(D_MODEL, N_EXPERTS)
    wre_hi, wre_lo = _split_bf16(_pad_lanes(w_re))
    return dict(
        norm_mix_g=row(norm_mix_g), w_in=w_in.astype(BF16), b_gate=row(b_gate),
        w_out_sb=w_out_sb.astype(BF16), w_out_mb=w_out_mb.astype(BF16), w_out=w_out.astype(BF16),
        norm_mem_g=row(norm_mem_g), norm_memsrc_g=row(norm_memsrc_g), w_q_mem=w_q_mem.astype(BF16),
        w_kv_mem=w_kv_mem.astype(BF16), w_o_mem=w_o_mem.astype(BF16), norm_ffn_g=row(norm_ffn_g),
        wrg_hi=wrg_hi, wrg_lo=wrg_lo, brg=_pad_lanes(row(b_router_group)),
        wre_hi=wre_hi, wre_lo=wre_lo, bre=_pad_lanes(row(b_router_expert)),
        w_gate_e=w_gate_e.astype(BF16), w_up_e=w_up_e.astype(BF16), w_down_e=w_down_e.astype(BF16),
        norm_final_g=row(norm_final_g))


def _tail(x, osb, omb, gate, mem_k16, mem_v16, rows_per_batch, w, tm_post, tm_moe, **mem_mask):
    x2, xn16, comb = _post(x, osb, omb, gate, mem_k16, mem_v16, rows_per_batch, tm_post, w, **mem_mask)
    return _moe(xn16, comb, x2, w["w_gate_e"], w["w_up_e"], w["w_down_e"], w["norm_final_g"], tm_moe)


def kernel(x_prompt, x_sample, mem_prompt, cache_sb_k, cache_sb_v, cache_mb_k, cache_mb_v, cache_mem_k, cache_mem_v, page_table, norm_mix_g, w_in, b_gate, w_out_sb, w_out_mb, w_out, norm_mem_g, norm_memsrc_g, w_q_mem, w_kv_mem, w_o_mem, norm_ffn_g, w_router_group, b_router_group, w_router_expert, b_router_expert, w_gate_e, w_up_e, w_down_e, norm_final_g):
    w = _prepare_weights(norm_mix_g, w_in, b_gate, w_out_sb, w_out_mb, w_out, norm_mem_g, norm_memsrc_g, w_q_mem,
                         w_kv_mem, w_o_mem, norm_ffn_g, w_router_group, b_router_group, w_router_expert,
                         b_router_expert, w_gate_e, w_up_e, w_down_e, norm_final_g)
    batch, seq, _ = x_prompt.shape
    dec_batch, n_new, _ = x_sample.shape
    n_mem = mem_prompt.shape[1]
    n_pages = page_table.shape[1]
    past_len = n_pages * PAGE_SIZE
    assert seq % MOBA_BLOCK == 0 and past_len % MOBA_BLOCK == 0 and n_new <= PAGE_SIZE and n_pages <= LANES
    heads = lambda t, b, s: t.reshape(b, s, N_HEADS, HEAD_DIM)

    xp = x_prompt.reshape(batch * seq, D_MODEL)
    cos_p, sin_p = _rope_tables(jnp.arange(seq, dtype=I32))
    (qsb, ksb, vsb, qmb, kmb, vmb, gate, ksb16, vsb16, kmb16, vmb16, kmean) = _proj(
        xp, w["norm_mix_g"], w["w_in"], w["b_gate"], cos_p, sin_p, MOBA_BLOCK)
    osb = _sb_prompt(qsb, ksb16, vsb16, batch, seq, MOBA_BLOCK)
    omb = _moba_prompt(qmb, kmb16, vmb16, kmean.reshape(batch * seq // MOBA_BLOCK, W_MIX), batch, seq)
    mem_k, mem_v = _mem_kv(mem_prompt.reshape(batch * n_mem, D_MODEL), w["norm_memsrc_g"], w["w_kv_mem"])
    y_prompt = _tail(xp, osb, omb, gate, mem_k.astype(BF16).reshape(batch, n_mem, W_MEM),
                     mem_v.astype(BF16).reshape(batch, n_mem, W_MEM), seq, w, 512, 1024)

    rows_s = dec_batch * n_new
    xs = x_sample.reshape(rows_s, D_MODEL)
    cos_s, sin_s = _rope_tables(past_len + (jnp.arange(rows_s, dtype=I32) % n_new))
    (qsb_s, ksb_s, vsb_s, qmb_s, kmb_s, vmb_s, gate_s, _, _, _, _, _) = _proj(
        xs, w["norm_mix_g"], w["w_in"], w["b_gate"], cos_s, sin_s, rows_s)

    def block_diagonal(q16):
        q_cols = heads(q16, dec_batch, n_new).transpose(0, 2, 1, 3).reshape(dec_batch, N_HEADS * n_new, HEAD_DIM)
        own = (jnp.arange(N_HEADS * n_new)[:, None] // n_new) == (jnp.arange(W_MIX)[None, :] // HEAD_DIM)
        return jnp.where(own[None], jnp.tile(q_cols, (1, 1, N_HEADS)), jnp.zeros((), q16.dtype))

    def new_page(t):
        t = t.reshape(dec_batch, n_new, W_MIX).transpose(0, 2, 1)
        return jnp.pad(t, ((0, 0), (0, 0), (0, PAGE_SIZE - n_new)))

    def token_rows(o):
        o = o.reshape(dec_batch, N_HEADS, n_new, HEAD_DIM).transpose(0, 2, 1, 3)
        return o.reshape(rows_s, W_MIX).astype(BF16)

    pages = lambda c: c.transpose(0, 2, 3, 1)
    osb_s = token_rows(_sb_sample(page_table, block_diagonal(qsb_s), new_page(ksb_s), new_page(vsb_s),
                                  pages(cache_sb_k), pages(cache_sb_v), n_new))
    omb_s = token_rows(_moba_sample(page_table, block_diagonal(qmb_s), new_page(kmb_s), new_page(vmb_s),
                                    pages(cache_mb_k), pages(cache_mb_v), n_new))
    y_sample = _tail(xs, osb_s, omb_s, gate_s,
                     cache_mem_k.astype(BF16).reshape(1, dec_batch * n_mem, W_MEM),
                     cache_mem_v.astype(BF16).reshape(1, dec_batch * n_mem, W_MEM),
                     rows_s, w, rows_s, rows_s, rows_per_mem=n_new, keys_per_mem=n_mem)

    mem_heads = lambda t: t.reshape(batch, n_mem, H_MEM, HD_MEM)
    return (y_prompt.reshape(batch, seq, D_MODEL), y_sample.reshape(dec_batch, n_new, D_MODEL),
            heads(ksb, batch, seq), heads(vsb, batch, seq), heads(kmb, batch, seq), heads(vmb, batch, seq),
            mem_heads(mem_k), mem_heads(mem_v),
            heads(ksb_s, dec_batch, n_new), heads(vsb_s, dec_batch, n_new),
            heads(kmb_s, dec_batch, n_new), heads(vmb_s, dec_batch, n_new))
```

```python
import functools

import jax
import jax.numpy as jnp
from jax import lax
from jax.experimental import pallas as pl
from jax.experimental.pallas import tpu as pltpu

F32 = jnp.float32
BF16 = jnp.bfloat16
I32 = jnp.int32

D_MODEL = 1024
N_HEADS = 8
HEAD_DIM = 64
W_MIX = N_HEADS * HEAD_DIM
PAGE_SIZE = 128
MOBA_BLOCK = 256
MOBA_TOPK = 3
H_MEM = 4
HD_MEM = 128
W_MEM = H_MEM * HD_MEM
N_GROUPS = 4
EXPERTS_PER_GROUP = 8
N_EXPERTS = N_GROUPS * EXPERTS_PER_GROUP
D_EXPERT = 256
ROPE_THETA = 10000.0
RMS_EPS = 1e-6
W_IN_COLS = 6 * W_MIX + 2 * D_MODEL

LOG2_E = 1.4426950408889634
LANES = 128
BF16_ROWS = 16
HEADS_PER_STEP = 4
QUAD = HEADS_PER_STEP * HEAD_DIM
SB_DEAD = -160.0
SB_TILE = 128
SB_TILES_PER_STEP = 2
NEG_INF = float("-inf")
MIB = 1024 * 1024


def _cparams(semantics, vmem_mib):
    return pltpu.CompilerParams(dimension_semantics=semantics, vmem_limit_bytes=vmem_mib * MIB)


def _rms(x, g):
    ms = jnp.mean(x * x, axis=-1, keepdims=True)
    return (x * lax.rsqrt(ms + RMS_EPS)) * g


def _dot(a, b):
    return jnp.dot(a, b, preferred_element_type=F32)


def _dot_nt(a, b):
    return lax.dot_general(a, b, (((1,), (1,)), ((), ())), preferred_element_type=F32)


def _split_bf16(x):
    hi = x.astype(BF16)
    lo = (x - hi.astype(F32)).astype(BF16)
    return hi, lo


def _later_keys(n):
    return (lax.broadcasted_iota(I32, (n, n), 0) > lax.broadcasted_iota(I32, (n, n), 1)).astype(BF16)


def _stick_breaking_tile(z, later, carried, valid):
    l1p = jnp.log1p(jnp.exp(-jnp.abs(z)))
    log_keep = -(jnp.maximum(z, 0.0) + l1p)
    if valid is not None:
        log_keep = jnp.where(valid, log_keep, 0.0)
    hi, lo = _split_bf16(log_keep)
    inner = _dot(hi, later) + _dot(lo, later)
    a = jnp.exp(jnp.minimum(z, 0.0) - l1p + carried + inner)
    if valid is not None:
        a = jnp.where(valid, a, 0.0)
    return a, carried + inner[:, 0:1] + log_keep[:, 0:1]


def _proj_kernel(x_ref, g_ref, w_ref, bg_ref, cos_ref, sin_ref, qsb_ref, qmb_ref, gate_ref, *kv_refs, head_major):
    xb = _rms(x_ref[...], g_ref[...]).astype(BF16)

    def seg(lo, width):
        return _dot(xb, w_ref[:, lo:lo + width])

    lane = lax.broadcasted_iota(I32, (1, W_MIX), 1)
    first_half = (lane % HEAD_DIM) < (HEAD_DIM // 2)
    cos = cos_ref[...]
    sin = sin_ref[...]

    def rope(t):
        partner = jnp.where(first_half, pltpu.roll(t, W_MIX - HEAD_DIM // 2, 1),
                            pltpu.roll(t, HEAD_DIM // 2, 1))
        return t * cos + partner * sin

    scale = HEAD_DIM ** -0.5
    qsb_ref[...] = (seg(0, W_MIX) * scale).astype(BF16)
    qmb_ref[...] = (rope(seg(3 * W_MIX, W_MIX)) * (scale * LOG2_E)).astype(BF16)
    gl = seg(6 * W_MIX, 2 * D_MODEL) + bg_ref[...]
    gate_ref[...] = 1.0 / (1.0 + jnp.exp(-gl))
    ksb = seg(W_MIX, W_MIX)
    vsb = seg(2 * W_MIX, W_MIX)
    kmb = rope(seg(4 * W_MIX, W_MIX))
    vmb = seg(5 * W_MIX, W_MIX)
    if not head_major:
        for ref, t in zip(kv_refs, (ksb, vsb, kmb, vmb)):
            ref[...] = t
        return
    ksb_t_ref, vsb_t_ref, kmb_t_ref, vmb_t_ref, ksb16_ref, vsb16_t_ref, kmb16_ref, vmb16_t_ref, kmean_ref = kv_refs
    ksb_t_ref[0] = ksb.T
    vsb_t = vsb.T
    vsb_t_ref[0] = vsb_t
    kmb_t_ref[0] = kmb.T
    vmb_t = vmb.T
    vmb_t_ref[0] = vmb_t
    ksb16_ref[...] = ksb.astype(BF16)
    vsb16_t_ref[0] = vsb_t.astype(BF16)
    kmb16_ref[...] = kmb.astype(BF16)
    vmb16_t_ref[0] = vmb_t.astype(BF16)
    kmean_ref[0] = jnp.mean(kmb, axis=0, keepdims=True)


def _proj(x, g, w_in16, b_gate, cos, sin, tm, rows_per_batch=None):
    n = x.shape[0]
    nt = n // tm
    n_rope = cos.shape[0] // tm
    row = lambda i: (i, 0)
    const = lambda i: (0, 0)
    head_major = rows_per_batch is not None
    out_shape = [jax.ShapeDtypeStruct((n, W_MIX), BF16)] * 2 + [jax.ShapeDtypeStruct((n, 2 * D_MODEL), F32)]
    out_specs = [pl.BlockSpec((tm, W_MIX), row)] * 2 + [pl.BlockSpec((tm, 2 * D_MODEL), row)]
    if head_major:
        tiles = rows_per_batch // tm
        n_batch = n // rows_per_batch
        t_spec = pl.BlockSpec((1, W_MIX, tm), lambda i: (i // tiles, 0, i % tiles))
        rows16 = jax.ShapeDtypeStruct((n, W_MIX), BF16)
        t16 = jax.ShapeDtypeStruct((n_batch, W_MIX, rows_per_batch), BF16)
        out_shape += ([jax.ShapeDtypeStruct((n_batch, W_MIX, rows_per_batch), F32)] * 4
                      + [rows16, t16, rows16, t16, jax.ShapeDtypeStruct((nt, 1, W_MIX), F32)])
        r_spec = pl.BlockSpec((tm, W_MIX), row)
        out_specs += ([t_spec] * 4 + [r_spec, t_spec, r_spec, t_spec,
                                      pl.BlockSpec((1, 1, W_MIX), lambda i: (i, 0, 0))])
    else:
        out_shape += [jax.ShapeDtypeStruct((n, W_MIX), F32)] * 4
        out_specs += [pl.BlockSpec((tm, W_MIX), row)] * 4
    return pl.pallas_call(
        functools.partial(_proj_kernel, head_major=head_major),
        grid=(nt,),
        in_specs=[pl.BlockSpec((tm, D_MODEL), row),
                  pl.BlockSpec((1, D_MODEL), const),
                  pl.BlockSpec((D_MODEL, W_IN_COLS), const),
                  pl.BlockSpec((1, 2 * D_MODEL), const),
                  pl.BlockSpec((tm, W_MIX), lambda i: (i % n_rope, 0)),
                  pl.BlockSpec((tm, W_MIX), lambda i: (i % n_rope, 0))],
        out_specs=out_specs,
        out_shape=out_shape,
        compiler_params=_cparams(("parallel",), 48),
        name="proj",
    )(x, g, w_in16, b_gate, cos, sin)


def _head_masks(width):
    lane_head = lax.broadcasted_iota(I32, (1, width), 1) // HEAD_DIM
    return [lane_head == h for h in range(HEADS_PER_STEP)]


def _sb_prompt_kernel(q_ref, k_ref, vt_ref, o_ref, acc_ref, *, tq):
    i = pl.program_id(2)
    masks = _head_masks(QUAD)
    tiles = range(SB_TILES_PER_STEP)
    chains = [(t, h) for t in tiles for h in range(HEADS_PER_STEP)]
    q = [q_ref[t * tq:(t + 1) * tq, :] for t in tiles]
    qs = {(t, h): jnp.where(masks[h], q[t], jnp.zeros_like(q[t])) for t, h in chains}
    key = lax.broadcasted_iota(I32, (tq, tq), 0)
    qry = lax.broadcasted_iota(I32, (tq, tq), 1)
    after = (qry > key).astype(BF16)

    def block(step, dead, diagonal):
        kblk, vtblk = [], []
        for t in tiles:
            kb = i * SB_TILES_PER_STEP + t - step
            if not diagonal:
                dead = {c: (jnp.where(kb < 0, jnp.inf, d) if c[0] == t else d) for c, d in dead.items()}
                kb = jnp.maximum(kb, 0)
            start = pl.multiple_of(kb * tq, tq)
            kblk.append(k_ref[pl.ds(start, tq), :])
            vtblk.append(vt_ref[0, :, pl.ds(start, tq)])
        scores = {c: _dot_nt(kblk[c[0]], qs[c]) for c in chains}
        softplus, parts = {}, {}
        for c in chains:
            z = scores[c]
            sp = jnp.maximum(z, 0.0) + jnp.log(1.0 + jnp.exp(-jnp.abs(z)))
            if diagonal:
                sp = jnp.where(key < qry, sp, 0.0)
            softplus[c] = sp
            parts[c] = _split_bf16(sp)
        inner = {c: _dot(after, parts[c][0]) + _dot(after, parts[c][1]) for c in chains}
        new_dead = {}
        for c in chains:
            t, h = c
            a = jnp.exp(scores[c] - softplus[c] - (dead[c] + inner[c]))
            if diagonal:
                a = jnp.where(key < qry, a, 0.0)
            rows, cols = slice(h * HEAD_DIM, (h + 1) * HEAD_DIM), slice(t * tq, (t + 1) * tq)
            acc_ref[rows, cols] += _dot(vtblk[t][rows, :], a.astype(BF16))
            new_dead[c] = dead[c] + inner[c][0:1] + softplus[c][0:1]
        return new_dead

    def alive(dead):
        low = functools.reduce(jnp.minimum, [dead[c] for c in chains])
        return (jnp.min(low) < -SB_DEAD).astype(I32)

    acc_ref[...] = jnp.zeros_like(acc_ref)
    dead = block(0, {c: jnp.zeros((1, tq), F32) for c in chains}, True)
    newest = i * SB_TILES_PER_STEP + SB_TILES_PER_STEP - 1

    def cond(carry):
        return jnp.logical_and(carry[0] <= newest, carry[1] > 0)

    def body(carry):
        dead = block(carry[0], dict(zip(chains, carry[2:])), False)
        return (carry[0] + 1, alive(dead)) + tuple(dead[c] for c in chains)

    lax.while_loop(cond, body, (jnp.int32(1), alive(dead)) + tuple(dead[c] for c in chains))
    o_ref[...] = acc_ref[...].T.astype(o_ref.dtype)


def _sb_prompt(q16, k16, vt16, batch, seq, tq):
    rows = tq * SB_TILES_PER_STEP
    nq = seq // rows
    ng = W_MIX // QUAD
    return pl.pallas_call(
        functools.partial(_sb_prompt_kernel, tq=tq),
        grid=(batch, ng, nq),
        in_specs=[pl.BlockSpec((rows, QUAD), lambda b, g, i: (b * nq + i, g)),
                  pl.BlockSpec((seq, QUAD), lambda b, g, i: (b, g)),
                  pl.BlockSpec((1, QUAD, seq), lambda b, g, i: (b, g, 0))],
        out_specs=pl.BlockSpec((rows, QUAD), lambda b, g, i: (b * nq + i, g)),
        out_shape=jax.ShapeDtypeStruct((batch * seq, W_MIX), BF16),
        scratch_shapes=[pltpu.VMEM((QUAD, rows), F32)],
        compiler_params=_cparams(("parallel", "parallel", "arbitrary"), 48),
        name="sb_prompt",
    )(q16, k16, vt16)


def _first_argmax(x, lane, width):
    mx = jnp.max(x, axis=1, keepdims=True)
    idx = jnp.min(jnp.where(x == mx, lane, width), axis=1, keepdims=True)
    return mx, idx


def _top_blocks(gate, eligible, blk):
    g = jnp.where(eligible, gate, NEG_INF)
    sel = jnp.zeros(gate.shape, jnp.bool_)
    for _ in range(MOBA_TOPK):
        _, first = _first_argmax(g, blk, gate.shape[1])
        pick = blk == first
        sel = jnp.logical_or(sel, jnp.logical_and(pick, eligible))
        g = jnp.where(pick, NEG_INF, g)
    return sel


def _top_block_rows(gate, eligible, blk):
    n = gate.shape[0]
    g = jnp.where(eligible, gate, NEG_INF)
    sel = jnp.zeros(gate.shape, jnp.bool_)
    for _ in range(MOBA_TOPK):
        mx = jnp.max(g, axis=0, keepdims=True)
        first = jnp.min(jnp.where(g == mx, blk, n), axis=0, keepdims=True)
        pick = blk == first
        sel = jnp.logical_or(sel, jnp.logical_and(pick, eligible))
        g = jnp.where(pick, NEG_INF, g)
    return sel


def _moba_prompt_kernel(q_ref, k_ref, vt_ref, km_ref, o_ref, sel_ref, acc_ref, *, tq):
    i = pl.program_id(2)
    q = q_ref[...]
    masks = _head_masks(QUAD)
    zero16 = jnp.zeros_like(q)
    qs = [jnp.where(m, q, zero16) for m in masks]
    nb = km_ref.shape[0]
    km_hi, km_lo = _split_bf16(km_ref[...])
    blk = lax.broadcasted_iota(I32, (nb, 1), 0)
    for h in range(HEADS_PER_STEP):
        gate = _dot_nt(km_hi, qs[h]) + _dot_nt(km_lo, qs[h])
        sel_ref[h] = _top_block_rows(gate, blk < i, blk).astype(F32)
    key = lax.broadcasted_iota(I32, (tq, tq), 0)
    qry = lax.broadcasted_iota(I32, (tq, tq), 1)

    def load(kb):
        start = pl.multiple_of(kb * tq, tq)
        return k_ref[pl.ds(start, tq), :], vt_ref[0, :, pl.ds(start, tq)]

    def head_rows(h):
        return slice(h * HEAD_DIM, (h + 1) * HEAD_DIM)

    def weighted_values(vtblk, h, p16):
        ones = jnp.ones((BF16_ROWS, vtblk.shape[1]), BF16)
        out = _dot(jnp.concatenate([vtblk[head_rows(h), :], ones], axis=0), p16)
        return out[:HEAD_DIM], out[HEAD_DIM:HEAD_DIM + 1]

    kblk, vtblk = load(i)
    scores = [jnp.where(key <= qry, _dot_nt(kblk, qs[h]), NEG_INF) for h in range(HEADS_PER_STEP)]
    m_run = [jnp.max(s, axis=0, keepdims=True) for s in scores]
    probs = [jnp.exp2(s - m).astype(BF16) for s, m in zip(scores, m_run)]
    l_run = []
    for h in range(HEADS_PER_STEP):
        pv, psum = weighted_values(vtblk, h, probs[h])
        acc_ref[head_rows(h), :] = pv
        l_run.append(psum)

    def body(pair, carry):
        m_run, l_run = carry[0:4], carry[4:8]
        start = pl.multiple_of(pair * (2 * tq), 2 * tq)
        kblk = k_ref[pl.ds(start, 2 * tq), :]
        vtblk = vt_ref[0, :, pl.ds(start, 2 * tq)]
        scores = [_dot_nt(kblk, qs[h]) for h in range(HEADS_PER_STEP)]
        new_m, alphas, probs = [], [], []
        for h in range(HEADS_PER_STEP):
            halves = (scores[h][:tq], scores[h][tq:])
            chosen = [sel_ref[h, pl.ds(2 * pair + j, 1), :] > 0.0 for j in range(2)]
            m = m_run[h]
            for s, c in zip(halves, chosen):
                m = jnp.where(c, jnp.maximum(m, jnp.max(s, axis=0, keepdims=True)), m)
            new_m.append(m)
            alphas.append(jnp.exp2(m_run[h] - m))
            probs.append(jnp.concatenate(
                [jnp.exp2(s - jnp.where(c, m, jnp.inf)).astype(BF16) for s, c in zip(halves, chosen)], axis=0))
        new_l = []
        for h in range(HEADS_PER_STEP):
            pv, psum = weighted_values(vtblk, h, probs[h])
            acc_ref[head_rows(h), :] = alphas[h] * acc_ref[head_rows(h), :] + pv
            new_l.append(alphas[h] * l_run[h] + psum)
        return tuple(new_m) + tuple(new_l)

    out = lax.fori_loop(0, (i + 1) // 2, body, tuple(m_run) + tuple(l_run))
    for h in range(HEADS_PER_STEP):
        acc_ref[head_rows(h), :] = acc_ref[head_rows(h), :] * (1.0 / out[4 + h])
    o_ref[...] = acc_ref[...].T.astype(o_ref.dtype)


def _moba_prompt(q16, k16, vt16, kmean, batch, seq):
    tq = MOBA_BLOCK
    nq = seq // tq
    ng = W_MIX // QUAD
    return pl.pallas_call(
        functools.partial(_moba_prompt_kernel, tq=tq),
        grid=(batch, ng, nq),
        in_specs=[pl.BlockSpec((tq, QUAD), lambda b, g, i: (b * nq + i, g)),
                  pl.BlockSpec((seq, QUAD), lambda b, g, i: (b, g)),
                  pl.BlockSpec((1, QUAD, seq), lambda b, g, i: (b, g, 0)),
                  pl.BlockSpec((nq, QUAD), lambda b, g, i: (b, g))],
        out_specs=pl.BlockSpec((tq, QUAD), lambda b, g, i: (b * nq + i, g)),
        out_shape=jax.ShapeDtypeStruct((batch * seq, W_MIX), BF16),
        scratch_shapes=[pltpu.VMEM((HEADS_PER_STEP, nq, tq), F32), pltpu.VMEM((QUAD, tq), F32)],
        compiler_params=_cparams(("parallel", "parallel", "arbitrary"), 48),
        name="moba_prompt",
    )(q16, k16, vt16, kmean)


def _mem_kv_kernel(mem_ref, g_ref, w_ref, k_ref, v_ref):
    kv = _dot(_rms(mem_ref[...], g_ref[...]).astype(BF16), w_ref[...])
    k_ref[...] = kv[:, :W_MEM]
    v_ref[...] = kv[:, W_MEM:]


def _mem_kv(mem, g, w_kv16):
    n = mem.shape[0]
    return pl.pallas_call(
        _mem_kv_kernel,
        out_shape=[jax.ShapeDtypeStruct((n, W_MEM), F32)] * 2,
        name="mem_kv",
    )(mem, g, w_kv16)


def _dot3(x, w_hi, w_lo):
    x_hi, x_lo = _split_bf16(x)
    return _dot(x_hi, w_hi) + (_dot(x_hi, w_lo) + _dot(x_lo, w_hi))


def _post_kernel(x_ref, osb_ref, omb_ref, gate_ref, wsb_ref, wmb_ref, wo_ref,
                 gmem_ref, wq_ref, mk_ref, mv_ref, wom_ref, gffn_ref,
                 wrg_hi_ref, wrg_lo_ref, brg_ref, wre_hi_ref, wre_lo_ref, bre_ref,
                 x2_ref, xn_ref, comb_ref, *, rows_per_mem, keys_per_mem):
    gate = gate_ref[...]
    h = gate[:, :D_MODEL] * _dot(osb_ref[...], wsb_ref[...]) + gate[:, D_MODEL:] * _dot(omb_ref[...], wmb_ref[...])
    x1 = x_ref[...] + _dot(h.astype(BF16), wo_ref[...])

    q = (_dot(_rms(x1, gmem_ref[...]).astype(BF16), wq_ref[...]) * (HD_MEM ** -0.5)).astype(BF16)
    heads = []
    if rows_per_mem is not None:
        shape = (x1.shape[0], mk_ref.shape[1])
        same_mem = (lax.broadcasted_iota(I32, shape, 0) // rows_per_mem
                    == lax.broadcasted_iota(I32, shape, 1) // keys_per_mem)
    for hh in range(H_MEM):
        sl = slice(hh * HD_MEM, (hh + 1) * HD_MEM)
        s = _dot_nt(q[:, sl], mk_ref[0, :, sl])
        if rows_per_mem is not None:
            s = jnp.where(same_mem, s, NEG_INF)
        p = jnp.exp(s - jnp.max(s, axis=1, keepdims=True))
        p = p / jnp.sum(p, axis=1, keepdims=True)
        heads.append(_dot(p.astype(BF16), mv_ref[0, :, sl]))
    o = jnp.concatenate(heads, axis=1)
    x2 = x1 + _dot(o.astype(BF16), wom_ref[...])
    x2_ref[...] = x2

    xn = _rms(x2, gffn_ref[...])
    xn_ref[...] = xn.astype(BF16)

    lane = lax.broadcasted_iota(I32, (1, LANES), 1)
    gl = jnp.where(lane < N_GROUPS, _dot3(xn, wrg_hi_ref[...], wrg_lo_ref[...]) + brg_ref[...], NEG_INF)
    g_max, g_idx = _first_argmax(gl, lane, LANES)
    g_w = 1.0 / jnp.sum(jnp.exp(gl - g_max), axis=1, keepdims=True)
    el = _dot3(xn, wre_hi_ref[...], wre_lo_ref[...]) + bre_ref[...]
    el = jnp.where((lane // EXPERTS_PER_GROUP) == g_idx, el, NEG_INF)
    e_max, i1 = _first_argmax(el, lane, LANES)
    e_sum = jnp.sum(jnp.exp(el - e_max), axis=1, keepdims=True)
    el2 = jnp.where(lane == i1, NEG_INF, el)
    e_max2, i2 = _first_argmax(el2, lane, LANES)
    w1 = 1.0 / e_sum
    w2 = jnp.exp(e_max2 - e_max) / e_sum
    norm = w1 + w2
    comb_ref[...] = jnp.where(lane == i1, g_w * (w1 / norm), 0.0) + jnp.where(lane == i2, g_w * (w2 / norm), 0.0)


def _post(x, osb, omb, gate, mem_k16, mem_v16, rows_per_batch, tm, w, rows_per_mem=None, keys_per_mem=None):
    n = x.shape[0]
    nt = n // tm
    tiles_per_batch = rows_per_batch // tm
    row = lambda i: (i, 0)
    const = lambda i: (0, 0)
    mem = lambda i: (i // tiles_per_batch, 0, 0)
    n_mem = mem_k16.shape[1]
    full = lambda a: pl.BlockSpec(a.shape, const)
    return pl.pallas_call(
        functools.partial(_post_kernel, rows_per_mem=rows_per_mem, keys_per_mem=keys_per_mem),
        grid=(nt,),
        in_specs=[pl.BlockSpec((tm, D_MODEL), row), pl.BlockSpec((tm, W_MIX), row), pl.BlockSpec((tm, W_MIX), row),
                  pl.BlockSpec((tm, 2 * D_MODEL), row),
                  full(w["w_out_sb"]), full(w["w_out_mb"]), full(w["w_out"]),
                  full(w["norm_mem_g"]), full(w["w_q_mem"]),
                  pl.BlockSpec((1, n_mem, W_MEM), mem), pl.BlockSpec((1, n_mem, W_MEM), mem),
                  full(w["w_o_mem"]), full(w["norm_ffn_g"]),
                  full(w["wrg_hi"]), full(w["wrg_lo"]), full(w["brg"]),
                  full(w["wre_hi"]), full(w["wre_lo"]), full(w["bre"])],
        out_specs=[pl.BlockSpec((tm, D_MODEL), row), pl.BlockSpec((tm, D_MODEL), row), pl.BlockSpec((tm, LANES), row)],
        out_shape=[jax.ShapeDtypeStruct((n, D_MODEL), F32), jax.ShapeDtypeStruct((n, D_MODEL), BF16),
                   jax.ShapeDtypeStruct((n, LANES), F32)],
        compiler_params=_cparams(("parallel",), 48),
        name="post",
    )(x, osb, omb, gate, w["w_out_sb"], w["w_out_mb"], w["w_out"], w["norm_mem_g"], w["w_q_mem"],
      mem_k16, mem_v16, w["w_o_mem"], w["norm_ffn_g"],
      w["wrg_hi"], w["wrg_lo"], w["brg"], w["wre_hi"], w["wre_lo"], w["bre"])


def _moe_kernel(xn_ref, comb_ref, x2_ref, wg_ref, wu_ref, wd_ref, gfin_ref, y_ref, acc_ref):
    e = pl.program_id(1)

    @pl.when(e == 0)
    def _():
        acc_ref[...] = jnp.zeros_like(acc_ref)

    xn = xn_ref[...]
    hg = _dot(xn, wg_ref[0])
    hu = _dot(xn, wu_ref[0])
    lane = lax.broadcasted_iota(I32, (1, LANES), 1)
    weight = jnp.sum(jnp.where(lane == e, comb_ref[...], 0.0), axis=1, keepdims=True)
    hidden = (hg / (1.0 + jnp.exp(-hg))) * hu * weight
    acc_ref[...] += _dot(hidden.astype(BF16), wd_ref[0])

    @pl.when(e == pl.num_programs(1) - 1)
    def _():
        y_ref[...] = _rms(x2_ref[...] + acc_ref[...], gfin_ref[...])


def _moe(xn16, comb, x2, wg16, wu16, wd16, gfin, tm):
    n = xn16.shape[0]
    nt = n // tm
    row = lambda i, e: (i, 0)
    exp_w = lambda i, e: (e, 0, 0)
    return pl.pallas_call(
        _moe_kernel,
        grid=(nt, N_EXPERTS),
        in_specs=[pl.BlockSpec((tm, D_MODEL), row), pl.BlockSpec((tm, LANES), row), pl.BlockSpec((tm, D_MODEL), row),
                  pl.BlockSpec((1, D_MODEL, D_EXPERT), exp_w), pl.BlockSpec((1, D_MODEL, D_EXPERT), exp_w),
                  pl.BlockSpec((1, D_EXPERT, D_MODEL), exp_w),
                  pl.BlockSpec((1, D_MODEL), lambda i, e: (0, 0))],
        out_specs=pl.BlockSpec((tm, D_MODEL), row),
        out_shape=jax.ShapeDtypeStruct((n, D_MODEL), F32),
        scratch_shapes=[pltpu.VMEM((tm, D_MODEL), F32)],
        compiler_params=_cparams(("parallel", "arbitrary"), 48),
        name="moe",
    )(xn16, comb, x2, wg16, wu16, wd16, gfin)


def _own_head_block(full, n_new):
    row_head = lax.broadcasted_iota(I32, (full.shape[0], 1), 0) // n_new
    out = jnp.zeros((full.shape[0], HEAD_DIM), F32)
    for h in range(N_HEADS):
        out = jnp.where(row_head == h, full[:, h * HEAD_DIM:(h + 1) * HEAD_DIM], out)
    return out


def _sb_sample_kernel(pt_ref, q_ref, kn_ref, vn_ref, kc_hbm, vc_hbm, o_ref, kbuf, vbuf, sem, *, n_new, n_pages):
    b = pl.program_id(0)
    q = q_ref[0]
    rows = q.shape[0]
    later = _later_keys(PAGE_SIZE)

    def page_copies(p, slot):
        page = pt_ref[b, p]
        return (pltpu.make_async_copy(kc_hbm.at[page], kbuf.at[slot], sem.at[0, slot]),
                pltpu.make_async_copy(vc_hbm.at[page], vbuf.at[slot], sem.at[1, slot]))

    def start(p, slot):
        for c in page_copies(p, slot):
            c.start()

    def wait(p, slot):
        for c in page_copies(p, slot):
            c.wait()

    slot_of = lambda p: (n_pages - 1 - p) % 2
    start(n_pages - 1, 0)

    key_slot = lax.broadcasted_iota(I32, (rows, PAGE_SIZE), 1)
    tok = lax.broadcasted_iota(I32, (rows, PAGE_SIZE), 0) % n_new
    a, run = _stick_breaking_tile(_dot(q, kn_ref[0].astype(BF16)), later, jnp.zeros((rows, 1), F32), key_slot < tok)
    acc = _dot_nt(a.astype(BF16), vn_ref[0].astype(BF16))

    def cond(carry):
        p, alive = carry[0], carry[1]
        return jnp.logical_and(p >= 0, alive > 0)

    def body(carry):
        p, _, run, acc = carry
        slot = slot_of(p)

        @pl.when(p > 0)
        def _():
            start(p - 1, 1 - slot)

        wait(p, slot)
        z = _dot(q, kbuf[slot].reshape(W_MIX, PAGE_SIZE).astype(BF16))
        a, run = _stick_breaking_tile(z, later, run, None)
        acc = acc + _dot_nt(a.astype(BF16), vbuf[slot].reshape(W_MIX, PAGE_SIZE).astype(BF16))
        alive = (jnp.max(run) > SB_DEAD).astype(I32)
        return (p - 1, alive, run, acc)

    p_next, _, _, acc = lax.while_loop(cond, body, (jnp.int32(n_pages - 1), jnp.int32(1), run, acc))

    @pl.when(p_next >= 0)
    def _():
        wait(p_next, slot_of(p_next))

    o_ref[0] = _own_head_block(acc, n_new)


def _sb_sample(page_table, q_bd, kn_t, vn_t, cache_kt, cache_vt, n_new):
    nb, n_pages = page_table.shape
    rows = N_HEADS * n_new
    per_b = lambda b, pt: (b, 0, 0)
    page_buffers = pltpu.VMEM((2, N_HEADS, HEAD_DIM, PAGE_SIZE), F32)
    grid_spec = pltpu.PrefetchScalarGridSpec(
        num_scalar_prefetch=1, grid=(nb,),
        in_specs=[pl.BlockSpec((1, rows, W_MIX), per_b),
                  pl.BlockSpec((1, W_MIX, PAGE_SIZE), per_b),
                  pl.BlockSpec((1, W_MIX, PAGE_SIZE), per_b),
                  pl.BlockSpec(memory_space=pl.ANY),
                  pl.BlockSpec(memory_space=pl.ANY)],
        out_specs=pl.BlockSpec((1, rows, HEAD_DIM), per_b),
        scratch_shapes=[page_buffers, page_buffers, pltpu.SemaphoreType.DMA((2, 2))])
    return pl.pallas_call(
        functools.partial(_sb_sample_kernel, n_new=n_new, n_pages=n_pages),
        grid_spec=grid_spec,
        out_shape=jax.ShapeDtypeStruct((nb, rows, HEAD_DIM), F32),
        compiler_params=_cparams(("arbitrary",), 32),
        name="sb_sample",
    )(page_table, q_bd, kn_t, vn_t, cache_kt, cache_vt)


PAGES_PER_BLOCK = MOBA_BLOCK // PAGE_SIZE
MOBA_SAMPLE_BLOCKS_PER_STEP = 2


def _moba_sample_kernel(pt_ref, q_ref, kn_ref, vn_ref, *refs, n_new):
    pages_per_step = MOBA_SAMPLE_BLOCKS_PER_STEP * PAGES_PER_BLOCK
    k_refs, v_refs = refs[:pages_per_step], refs[pages_per_step:2 * pages_per_step]
    o_ref, pm_ref, pl_ref, pg_ref, po_ref = refs[2 * pages_per_step:]
    step = pl.program_id(1)
    n_blocks = pl.num_programs(1) * MOBA_SAMPLE_BLOCKS_PER_STEP
    q = q_ref[0]
    rows = q.shape[0]
    block_lane = lax.broadcasted_iota(I32, (1, LANES), 1)

    @pl.when(step == 0)
    def _():
        pm_ref[...] = jnp.full(pm_ref.shape, NEG_INF, F32)
        pl_ref[...] = jnp.zeros_like(pl_ref)
        pg_ref[...] = jnp.zeros_like(pg_ref)

    def block_pages(page_refs, j):
        pages = [r[0].reshape(W_MIX, PAGE_SIZE).astype(BF16)
                 for r in page_refs[j * PAGES_PER_BLOCK:(j + 1) * PAGES_PER_BLOCK]]
        return jnp.concatenate(pages, axis=1)

    pm, pl_, pg = pm_ref[...], pl_ref[...], pg_ref[...]
    for j in range(MOBA_SAMPLE_BLOCKS_PER_STEP):
        blk = step * MOBA_SAMPLE_BLOCKS_PER_STEP + j
        z = _dot(q, block_pages(k_refs, j))
        m = jnp.max(z, axis=1, keepdims=True)
        pe = jnp.exp2(z - m)
        here = block_lane == blk
        pm = jnp.where(here, m, pm)
        pl_ = jnp.where(here, jnp.sum(pe, axis=1, keepdims=True), pl_)
        pg = jnp.where(here, jnp.sum(z, axis=1, keepdims=True), pg)
        po_ref[blk] = _dot_nt(pe.astype(BF16), block_pages(v_refs, j))
    pm_ref[...], pl_ref[...], pg_ref[...] = pm, pl_, pg

    @pl.when(step == pl.num_programs(1) - 1)
    def _():
        chosen = _top_blocks(pg, block_lane < n_blocks, block_lane)
        slot = lax.broadcasted_iota(I32, (rows, PAGE_SIZE), 1)
        tok = lax.broadcasted_iota(I32, (rows, PAGE_SIZE), 0) % n_new
        zn = jnp.where(slot <= tok, _dot(q, kn_ref[0].astype(BF16)), NEG_INF)
        m_all = jnp.maximum(jnp.max(jnp.where(chosen, pm, NEG_INF), axis=1, keepdims=True),
                            jnp.max(zn, axis=1, keepdims=True))
        pn = jnp.exp2(zn - m_all)
        w = jnp.where(chosen, jnp.exp2(pm - m_all), 0.0)
        total = jnp.sum(w * pl_, axis=1, keepdims=True) + jnp.sum(pn, axis=1, keepdims=True)
        acc = _dot_nt(pn.astype(BF16), vn_ref[0].astype(BF16))
        for n in range(po_ref.shape[0]):
            acc = acc + w[:, n:n + 1] * po_ref[n]
        o_ref[0] = _own_head_block(acc, n_new) / total


def _moba_sample(page_table, q_bd, kn_t, vn_t, cache_kt, cache_vt, n_new):
    nb, n_pages = page_table.shape
    rows = N_HEADS * n_new
    pages_per_step = MOBA_SAMPLE_BLOCKS_PER_STEP * PAGES_PER_BLOCK
    n_steps = n_pages // pages_per_step
    per_b = lambda b, s, pt: (b, 0, 0)
    page_spec = lambda j: pl.BlockSpec((1, N_HEADS, HEAD_DIM, PAGE_SIZE),
                                       lambda b, s, pt: (pt[b, s * pages_per_step + j], 0, 0, 0))
    page_specs = [page_spec(j) for j in range(pages_per_step)]
    grid_spec = pltpu.PrefetchScalarGridSpec(
        num_scalar_prefetch=1, grid=(nb, n_steps),
        in_specs=[pl.BlockSpec((1, rows, W_MIX), per_b),
                  pl.BlockSpec((1, W_MIX, PAGE_SIZE), per_b),
                  pl.BlockSpec((1, W_MIX, PAGE_SIZE), per_b)] + page_specs + page_specs,
        out_specs=pl.BlockSpec((1, rows, HEAD_DIM), per_b),
        scratch_shapes=[pltpu.VMEM((rows, LANES), F32), pltpu.VMEM((rows, LANES), F32),
                        pltpu.VMEM((rows, LANES), F32),
                        pltpu.VMEM((n_pages // PAGES_PER_BLOCK, rows, W_MIX), F32)])
    return pl.pallas_call(
        functools.partial(_moba_sample_kernel, n_new=n_new),
        grid_spec=grid_spec,
        out_shape=jax.ShapeDtypeStruct((nb, rows, HEAD_DIM), F32),
        compiler_params=_cparams(("parallel", "arbitrary"), 32),
        name="moba_sample",
    )(page_table, q_bd, kn_t, vn_t, *([cache_kt] * pages_per_step), *([cache_vt] * pages_per_step))


def _rope_tables(pos):
    half = HEAD_DIM // 2
    inv_freq = ROPE_THETA ** (-jnp.arange(half, dtype=F32) / half)
    ang = pos.astype(F32)[:, None] * inv_freq[None, :]
    cos = jnp.cos(ang)
    sin = jnp.sin(ang)
    return (jnp.tile(jnp.concatenate([cos, cos], axis=1), (1, N_HEADS)),
            jnp.tile(jnp.concatenate([-sin, sin], axis=1), (1, N_HEADS)))


def _pad_lanes(a):
    return jnp.pad(a, ((0, 0), (0, LANES - a.shape[1])))


def _prepare_weights(norm_mix_g, w_in, b_gate, w_out_sb, w_out_mb, w_out, norm_mem_g, norm_memsrc_g, w_q_mem,
                     w_kv_mem, w_o_mem, norm_ffn_g, w_router_group, b_router_group, w_router_expert,
                     b_router_expert, w_gate_e, w_up_e, w_down_e, norm_final_g):
    row = lambda v: v.reshape(1, -1).astype(F32)
    wrg_hi, wrg_lo = _split_bf16(_pad_lanes(w_router_group))
    w_re = w_router_expert.transpose(1, 0, 2).reshape(D_MODEL, N_EXPERTS)
    wre_hi, wre_lo = _split_bf16(_pad_lanes(w_re))
    return dict(
        norm_mix_g=row(norm_mix_g), w_in=w_in.astype(BF16), b_gate=row(b_gate),
        w_out_sb=w_out_sb.astype(BF16), w_out_mb=w_out_mb.astype(BF16), w_out=w_out.astype(BF16),
        norm_mem_g=row(norm_mem_g), norm_memsrc_g=row(norm_memsrc_g), w_q_mem=w_q_mem.astype(BF16),
        w_kv_mem=w_kv_mem.astype(BF16), w_o_mem=w_o_mem.astype(BF16), norm_ffn_g=row(norm_ffn_g),
        wrg_hi=wrg_hi, wrg_lo=wrg_lo, brg=_pad_lanes(row(b_router_group)),
        wre_hi=wre_hi, wre_lo=wre_lo, bre=_pad_lanes(row(b_router_expert)),
        w_gate_e=w_gate_e.astype(BF16), w_up_e=w_up_e.astype(BF16), w_down_e=w_down_e.astype(BF16),
        norm_final_g=row(norm_final_g))


def _tail(x, osb, omb, gate, mem_k16, mem_v16, rows_per_batch, w, tm_post, tm_moe, **mem_mask):
    x2, xn16, comb = _post(x, osb, omb, gate, mem_k16, mem_v16, rows_per_batch, tm_post, w, **mem_mask)
    return _moe(xn16, comb, x2, w["w_gate_e"], w["w_up_e"], w["w_down_e"], w["norm_final_g"], tm_moe)


def kernel(x_prompt, x_sample, mem_prompt, cache_sb_k, cache_sb_v, cache_mb_k, cache_mb_v, cache_mem_k, cache_mem_v, page_table, norm_mix_g, w_in, b_gate, w_out_sb, w_out_mb, w_out, norm_mem_g, norm_memsrc_g, w_q_mem, w_kv_mem, w_o_mem, norm_ffn_g, w_router_group, b_router_group, w_router_expert, b_router_expert, w_gate_e, w_up_e, w_down_e, norm_final_g):
    w = _prepare_weights(norm_mix_g, w_in, b_gate, w_out_sb, w_out_mb, w_out, norm_mem_g, norm_memsrc_g, w_q_mem,
                         w_kv_mem, w_o_mem, norm_ffn_g, w_router_group, b_router_group, w_router_expert,
                         b_router_expert, w_gate_e, w_up_e, w_down_e, norm_final_g)
    batch, seq, _ = x_prompt.shape
    dec_batch, n_new, _ = x_sample.shape
    n_mem = mem_prompt.shape[1]
    n_pages = page_table.shape[1]
    past_len = n_pages * PAGE_SIZE
    assert seq % MOBA_BLOCK == 0 and n_new <= PAGE_SIZE and n_pages // PAGES_PER_BLOCK <= LANES
    assert n_pages % (MOBA_SAMPLE_BLOCKS_PER_STEP * PAGES_PER_BLOCK) == 0
    heads = lambda t, b, s: t.reshape(b, s, N_HEADS, HEAD_DIM)

    xp = x_prompt.reshape(batch * seq, D_MODEL)
    cos_p, sin_p = _rope_tables(jnp.arange(seq, dtype=I32))
    (qsb, qmb, gate, ksb_t, vsb_t, kmb_t, vmb_t, ksb16, vsb16_t, kmb16, vmb16_t, kmean) = _proj(
        xp, w["norm_mix_g"], w["w_in"], w["b_gate"], cos_p, sin_p, MOBA_BLOCK, rows_per_batch=seq)
    osb = _sb_prompt(qsb, ksb16, vsb16_t, batch, seq, SB_TILE)
    omb = _moba_prompt(qmb, kmb16, vmb16_t, kmean.reshape(batch * seq // MOBA_BLOCK, W_MIX), batch, seq)
    mem_k, mem_v = _mem_kv(mem_prompt.reshape(batch * n_mem, D_MODEL), w["norm_memsrc_g"], w["w_kv_mem"])
    y_prompt = _tail(xp, osb, omb, gate, mem_k.astype(BF16).reshape(batch, n_mem, W_MEM),
                     mem_v.astype(BF16).reshape(batch, n_mem, W_MEM), seq, w, 512, 1024)

    rows_s = dec_batch * n_new
    xs = x_sample.reshape(rows_s, D_MODEL)
    cos_s, sin_s = _rope_tables(past_len + (jnp.arange(rows_s, dtype=I32) % n_new))
    (qsb_s, qmb_s, gate_s, ksb_s, vsb_s, kmb_s, vmb_s) = _proj(
        xs, w["norm_mix_g"], w["w_in"], w["b_gate"], cos_s, sin_s, rows_s)

    def block_diagonal(q16):
        q_cols = heads(q16, dec_batch, n_new).transpose(0, 2, 1, 3).reshape(dec_batch, N_HEADS * n_new, HEAD_DIM)
        own = (jnp.arange(N_HEADS * n_new)[:, None] // n_new) == (jnp.arange(W_MIX)[None, :] // HEAD_DIM)
        return jnp.where(own[None], jnp.tile(q_cols, (1, 1, N_HEADS)), jnp.zeros((), q16.dtype))

    def new_page(t):
        t = t.reshape(dec_batch, n_new, W_MIX).transpose(0, 2, 1)
        return jnp.pad(t, ((0, 0), (0, 0), (0, PAGE_SIZE - n_new)))

    def token_rows(o):
        o = o.reshape(dec_batch, N_HEADS, n_new, HEAD_DIM).transpose(0, 2, 1, 3)
        return o.reshape(rows_s, W_MIX).astype(BF16)

    pages = lambda c: c.transpose(0, 2, 3, 1)
    osb_s = token_rows(_sb_sample(page_table, block_diagonal(qsb_s), new_page(ksb_s), new_page(vsb_s),
                                  pages(cache_sb_k), pages(cache_sb_v), n_new))
    omb_s = token_rows(_moba_sample(page_table, block_diagonal(qmb_s), new_page(kmb_s), new_page(vmb_s),
                                    pages(cache_mb_k), pages(cache_mb_v), n_new))
    y_sample = _tail(xs, osb_s, omb_s, gate_s,
                     cache_mem_k.astype(BF16).reshape(1, dec_batch * n_mem, W_MEM),
                     cache_mem_v.astype(BF16).reshape(1, dec_batch * n_mem, W_MEM),
                     rows_s, w, rows_s, rows_s, rows_per_mem=n_new, keys_per_mem=n_mem)

    mem_heads = lambda t: t.reshape(batch, n_mem, H_MEM, HD_MEM)
    from_head_major = lambda t: t.reshape(batch, N_HEADS, HEAD_DIM, seq).transpose(0, 3, 1, 2)
    return (y_prompt.reshape(batch, seq, D_MODEL), y_sample.reshape(dec_batch, n_new, D_MODEL),
            from_head_major(ksb_t), from_head_major(vsb_t), from_head_major(kmb_t), from_head_major(vmb_t),
            mem_heads(mem_k), mem_heads(mem_v),
            heads(ksb_s, dec_batch, n_new), heads(vsb_s, dec_batch, n_new),
            heads(kmb_s, dec_batch, n_new), heads(vmb_s, dec_batch, n_new))
```

```python
import functools

import jax
import jax.numpy as jnp
from jax import lax
from jax.experimental import pallas as pl
from jax.experimental.pallas import tpu as pltpu

F32 = jnp.float32
BF16 = jnp.bfloat16
I32 = jnp.int32

D_MODEL = 1024
N_HEADS = 8
HEAD_DIM = 64
W_MIX = N_HEADS * HEAD_DIM
PAGE_SIZE = 128
MOBA_BLOCK = 256
MOBA_TOPK = 3
H_MEM = 4
HD_MEM = 128
W_MEM = H_MEM * HD_MEM
N_GROUPS = 4
EXPERTS_PER_GROUP = 8
N_EXPERTS = N_GROUPS * EXPERTS_PER_GROUP
D_EXPERT = 256
ROPE_THETA = 10000.0
RMS_EPS = 1e-6
W_IN_COLS = 6 * W_MIX + 2 * D_MODEL

LOG2_E = 1.4426950408889634
LANES = 128
BF16_ROWS = 16
HEADS_PER_STEP = 4
QUAD = HEADS_PER_STEP * HEAD_DIM
SB_DEAD = -160.0
MOBA_BLOCKS_PER_ITER = 2
SB_TILE = 128
SB_TILES_PER_STEP = 4
NEG_INF = float("-inf")
MIB = 1024 * 1024


def _cparams(semantics, vmem_mib):
    return pltpu.CompilerParams(dimension_semantics=semantics, vmem_limit_bytes=vmem_mib * MIB)


def _rms(x, g):
    ms = jnp.mean(x * x, axis=-1, keepdims=True)
    return (x * lax.rsqrt(ms + RMS_EPS)) * g


def _dot(a, b):
    return jnp.dot(a, b, preferred_element_type=F32)


def _dot_nt(a, b):
    return lax.dot_general(a, b, (((1,), (1,)), ((), ())), preferred_element_type=F32)


def _split_bf16(x):
    hi = x.astype(BF16)
    lo = (x - hi.astype(F32)).astype(BF16)
    return hi, lo


def _later_keys(n):
    return (lax.broadcasted_iota(I32, (n, n), 0) > lax.broadcasted_iota(I32, (n, n), 1)).astype(BF16)


def _stick_breaking_tile(z, later, carried, valid):
    l1p = jnp.log1p(jnp.exp(-jnp.abs(z)))
    log_keep = -(jnp.maximum(z, 0.0) + l1p)
    if valid is not None:
        log_keep = jnp.where(valid, log_keep, 0.0)
    hi, lo = _split_bf16(log_keep)
    inner = _dot(hi, later) + _dot(lo, later)
    a = jnp.exp(jnp.minimum(z, 0.0) - l1p + carried + inner)
    if valid is not None:
        a = jnp.where(valid, a, 0.0)
    return a, carried + inner[:, 0:1] + log_keep[:, 0:1]


def _proj_kernel(x_ref, g_ref, w_ref, bg_ref, cos_ref, sin_ref, qsb_ref, qmb_ref, gate_ref, *kv_refs, head_major):
    xb = _rms(x_ref[...], g_ref[...]).astype(BF16)

    def seg(lo, width):
        return _dot(xb, w_ref[:, lo:lo + width])

    lane = lax.broadcasted_iota(I32, (1, W_MIX), 1)
    first_half = (lane % HEAD_DIM) < (HEAD_DIM // 2)
    cos = jnp.concatenate([cos_ref[...]] * (W_MIX // LANES), axis=1)
    sin = jnp.concatenate([sin_ref[...]] * (W_MIX // LANES), axis=1)

    def rope(t):
        partner = jnp.where(first_half, pltpu.roll(t, W_MIX - HEAD_DIM // 2, 1),
                            pltpu.roll(t, HEAD_DIM // 2, 1))
        return t * cos + partner * sin

    scale = HEAD_DIM ** -0.5
    qsb_ref[...] = (seg(0, W_MIX) * scale).astype(BF16)
    qmb_ref[...] = (rope(seg(3 * W_MIX, W_MIX)) * (scale * LOG2_E)).astype(BF16)
    gl = seg(6 * W_MIX, 2 * D_MODEL) + bg_ref[...]
    gate_ref[...] = 1.0 / (1.0 + jnp.exp(-gl))
    ksb = seg(W_MIX, W_MIX)
    vsb = seg(2 * W_MIX, W_MIX)
    kmb = rope(seg(4 * W_MIX, W_MIX))
    vmb = seg(5 * W_MIX, W_MIX)
    if not head_major:
        for ref, t in zip(kv_refs, (ksb, vsb, kmb, vmb)):
            ref[...] = t
        return
    ksb_t_ref, vsb_t_ref, kmb_t_ref, vmb_t_ref, ksb16_ref, vsb16_t_ref, kmb16_ref, vmb16_t_ref, kmean_ref = kv_refs
    ksb_t_ref[0] = ksb.T
    vsb_t = vsb.T
    vsb_t_ref[0] = vsb_t
    kmb_t_ref[0] = kmb.T
    vmb_t = vmb.T
    vmb_t_ref[0] = vmb_t
    ksb16_ref[...] = ksb.astype(BF16)
    vsb16_t_ref[0] = vsb_t.astype(BF16)
    kmb16_ref[...] = kmb.astype(BF16)
    vmb16_t_ref[0] = vmb_t.astype(BF16)
    kmean_ref[0] = jnp.mean(kmb, axis=0, keepdims=True)


def _proj(x, g, w_in16, b_gate, cos, sin, tm, rows_per_batch=None):
    n = x.shape[0]
    nt = n // tm
    n_rope = cos.shape[0] // tm
    row = lambda i: (i, 0)
    const = lambda i: (0, 0)
    head_major = rows_per_batch is not None
    out_shape = [jax.ShapeDtypeStruct((n, W_MIX), BF16)] * 2 + [jax.ShapeDtypeStruct((n, 2 * D_MODEL), F32)]
    out_specs = [pl.BlockSpec((tm, W_MIX), row)] * 2 + [pl.BlockSpec((tm, 2 * D_MODEL), row)]
    if head_major:
        tiles = rows_per_batch // tm
        n_batch = n // rows_per_batch
        t_spec = pl.BlockSpec((1, W_MIX, tm), lambda i: (i // tiles, 0, i % tiles))
        rows16 = jax.ShapeDtypeStruct((n, W_MIX), BF16)
        t16 = jax.ShapeDtypeStruct((n_batch, W_MIX, rows_per_batch), BF16)
        out_shape += ([jax.ShapeDtypeStruct((n_batch, W_MIX, rows_per_batch), F32)] * 4
                      + [rows16, t16, rows16, t16, jax.ShapeDtypeStruct((nt, 1, W_MIX), F32)])
        r_spec = pl.BlockSpec((tm, W_MIX), row)
        out_specs += ([t_spec] * 4 + [r_spec, t_spec, r_spec, t_spec,
                                      pl.BlockSpec((1, 1, W_MIX), lambda i: (i, 0, 0))])
    else:
        out_shape += [jax.ShapeDtypeStruct((n, W_MIX), F32)] * 4
        out_specs += [pl.BlockSpec((tm, W_MIX), row)] * 4
    return pl.pallas_call(
        functools.partial(_proj_kernel, head_major=head_major),
        grid=(nt,),
        in_specs=[pl.BlockSpec((tm, D_MODEL), row),
                  pl.BlockSpec((1, D_MODEL), const),
                  pl.BlockSpec((D_MODEL, W_IN_COLS), const),
                  pl.BlockSpec((1, 2 * D_MODEL), const),
                  pl.BlockSpec((tm, LANES), lambda i: (i % n_rope, 0)),
                  pl.BlockSpec((tm, LANES), lambda i: (i % n_rope, 0))],
        out_specs=out_specs,
        out_shape=out_shape,
        compiler_params=_cparams(("parallel",), 48),
        name="proj",
    )(x, g, w_in16, b_gate, cos, sin)


def _head_masks(width):
    lane_head = lax.broadcasted_iota(I32, (1, width), 1) // HEAD_DIM
    return [lane_head == h for h in range(HEADS_PER_STEP)]


def _sb_prompt_kernel(q_ref, k_ref, vt_ref, o_ref, acc_ref, *, tq):
    i = pl.program_id(2)
    masks = _head_masks(QUAD)
    tiles = range(SB_TILES_PER_STEP)
    chains = [(t, h) for t in tiles for h in range(HEADS_PER_STEP)]
    q = [q_ref[t * tq:(t + 1) * tq, :] for t in tiles]
    qs = {(t, h): jnp.where(masks[h], q[t], jnp.zeros_like(q[t])) for t, h in chains}
    key = lax.broadcasted_iota(I32, (tq, tq), 0)
    qry = lax.broadcasted_iota(I32, (tq, tq), 1)
    after = (qry > key).astype(BF16)

    def block(step, dead, diagonal):
        kblk, vtblk = [], []
        for t in tiles:
            kb = i * SB_TILES_PER_STEP + t - step
            if not diagonal:
                dead = {c: (jnp.where(kb < 0, jnp.inf, d) if c[0] == t else d) for c, d in dead.items()}
                kb = jnp.maximum(kb, 0)
            start = pl.multiple_of(kb * tq, tq)
            kblk.append(k_ref[pl.ds(start, tq), :])
            vtblk.append(vt_ref[0, :, pl.ds(start, tq)])
        scores = {c: _dot_nt(kblk[c[0]], qs[c]) for c in chains}
        softplus, parts = {}, {}
        for c in chains:
            z = scores[c]
            sp = jnp.maximum(z, 0.0) + jnp.log(1.0 + jnp.exp(-jnp.abs(z)))
            if diagonal:
                sp = jnp.where(key < qry, sp, 0.0)
            softplus[c] = sp
            parts[c] = _split_bf16(sp)
        inner = {c: _dot(after, parts[c][0]) + _dot(after, parts[c][1]) for c in chains}
        new_dead = {}
        for c in chains:
            t, h = c
            a = jnp.exp(scores[c] - softplus[c] - (dead[c] + inner[c]))
            if diagonal:
                a = jnp.where(key < qry, a, 0.0)
            rows, cols = slice(h * HEAD_DIM, (h + 1) * HEAD_DIM), slice(t * tq, (t + 1) * tq)
            acc_ref[rows, cols] += _dot(vtblk[t][rows, :], a.astype(BF16))
            new_dead[c] = dead[c] + inner[c][0:1] + softplus[c][0:1]
        return new_dead

    def alive(dead):
        low = functools.reduce(jnp.minimum, [dead[c] for c in chains])
        return (jnp.min(low) < -SB_DEAD).astype(I32)

    acc_ref[...] = jnp.zeros_like(acc_ref)
    dead = block(0, {c: jnp.zeros((1, tq), F32) for c in chains}, True)
    newest = i * SB_TILES_PER_STEP + SB_TILES_PER_STEP - 1

    def cond(carry):
        return jnp.logical_and(carry[0] <= newest, carry[1] > 0)

    def body(carry):
        dead = block(carry[0], dict(zip(chains, carry[2:])), False)
        return (carry[0] + 1, alive(dead)) + tuple(dead[c] for c in chains)

    lax.while_loop(cond, body, (jnp.int32(1), alive(dead)) + tuple(dead[c] for c in chains))
    o_ref[...] = acc_ref[...].T.astype(o_ref.dtype)


def _sb_prompt(q16, k16, vt16, batch, seq, tq):
    rows = tq * SB_TILES_PER_STEP
    nq = seq // rows
    ng = W_MIX // QUAD
    return pl.pallas_call(
        functools.partial(_sb_prompt_kernel, tq=tq),
        grid=(batch, ng, nq),
        in_specs=[pl.BlockSpec((rows, QUAD), lambda b, g, i: (b * nq + i, g)),
                  pl.BlockSpec((seq, QUAD), lambda b, g, i: (b, g)),
                  pl.BlockSpec((1, QUAD, seq), lambda b, g, i: (b, g, 0))],
        out_specs=pl.BlockSpec((rows, QUAD), lambda b, g, i: (b * nq + i, g)),
        out_shape=jax.ShapeDtypeStruct((batch * seq, W_MIX), BF16),
        scratch_shapes=[pltpu.VMEM((QUAD, rows), F32)],
        compiler_params=_cparams(("parallel", "parallel", "arbitrary"), 48),
        name="sb_prompt",
    )(q16, k16, vt16)


def _first_argmax(x, lane, width):
    mx = jnp.max(x, axis=1, keepdims=True)
    idx = jnp.min(jnp.where(x == mx, lane, width), axis=1, keepdims=True)
    return mx, idx


def _top_blocks(gate, eligible, blk):
    g = jnp.where(eligible, gate, NEG_INF)
    sel = jnp.zeros(gate.shape, jnp.bool_)
    for _ in range(MOBA_TOPK):
        _, first = _first_argmax(g, blk, gate.shape[1])
        pick = blk == first
        sel = jnp.logical_or(sel, jnp.logical_and(pick, eligible))
        g = jnp.where(pick, NEG_INF, g)
    return sel


def _top_block_rows(gate, eligible, blk):
    n = gate.shape[0]
    g = jnp.where(eligible, gate, NEG_INF)
    sel = jnp.zeros(gate.shape, jnp.bool_)
    for _ in range(MOBA_TOPK):
        mx = jnp.max(g, axis=0, keepdims=True)
        first = jnp.min(jnp.where(g == mx, blk, n), axis=0, keepdims=True)
        pick = blk == first
        sel = jnp.logical_or(sel, jnp.logical_and(pick, eligible))
        g = jnp.where(pick, NEG_INF, g)
    return sel


def _moba_prompt_kernel(q_ref, k_ref, vt_ref, km_ref, o_ref, sel_ref, acc_ref, *, tq):
    i = pl.program_id(2)
    q = q_ref[...]
    masks = _head_masks(QUAD)
    zero16 = jnp.zeros_like(q)
    qs = [jnp.where(m, q, zero16) for m in masks]
    nb = km_ref.shape[0]
    km_hi, km_lo = _split_bf16(km_ref[...])
    blk = lax.broadcasted_iota(I32, (nb, 1), 0)
    for h in range(HEADS_PER_STEP):
        gate = _dot_nt(km_hi, qs[h]) + _dot_nt(km_lo, qs[h])
        sel_ref[h] = _top_block_rows(gate, blk < i, blk).astype(F32)
    key = lax.broadcasted_iota(I32, (tq, tq), 0)
    qry = lax.broadcasted_iota(I32, (tq, tq), 1)

    def load(kb):
        start = pl.multiple_of(kb * tq, tq)
        return k_ref[pl.ds(start, tq), :], vt_ref[0, :, pl.ds(start, tq)]

    def head_rows(h):
        return slice(h * HEAD_DIM, (h + 1) * HEAD_DIM)

    def weighted_values(vtblk, h, p16):
        ones = jnp.ones((BF16_ROWS, vtblk.shape[1]), BF16)
        out = _dot(jnp.concatenate([vtblk[head_rows(h), :], ones], axis=0), p16)
        return out[:HEAD_DIM], out[HEAD_DIM:HEAD_DIM + 1]

    kblk, vtblk = load(i)
    scores = [jnp.where(key <= qry, _dot_nt(kblk, qs[h]), NEG_INF) for h in range(HEADS_PER_STEP)]
    m_run = [jnp.max(s, axis=0, keepdims=True) for s in scores]
    probs = [jnp.exp2(s - m).astype(BF16) for s, m in zip(scores, m_run)]
    l_run = []
    for h in range(HEADS_PER_STEP):
        pv, psum = weighted_values(vtblk, h, probs[h])
        acc_ref[head_rows(h), :] = pv
        l_run.append(psum)

    span = MOBA_BLOCKS_PER_ITER

    def body(it, carry):
        m_run, l_run = carry[0:4], carry[4:8]
        start = pl.multiple_of(it * (span * tq), span * tq)
        kblk = k_ref[pl.ds(start, span * tq), :]
        vtblk = vt_ref[0, :, pl.ds(start, span * tq)]
        new_m, new_l = [None] * HEADS_PER_STEP, [None] * HEADS_PER_STEP
        scores, alphas, probs = {}, {}, {}

        def score(h):
            scores[h] = _dot_nt(kblk, qs[h])

        def softmax(h):
            parts = [scores[h][j * tq:(j + 1) * tq] for j in range(span)]
            chosen = [sel_ref[h, pl.ds(span * it + j, 1), :] > 0.0 for j in range(span)]
            m = m_run[h]
            for s, c in zip(parts, chosen):
                m = jnp.where(c, jnp.maximum(m, jnp.max(s, axis=0, keepdims=True)), m)
            new_m[h] = m
            alphas[h] = jnp.exp2(m_run[h] - m)
            probs[h] = jnp.concatenate(
                [jnp.exp2(s - jnp.where(c, m, jnp.inf)).astype(BF16) for s, c in zip(parts, chosen)], axis=0)

        def values(h):
            pv, psum = weighted_values(vtblk, h, probs[h])
            acc_ref[head_rows(h), :] = alphas[h] * acc_ref[head_rows(h), :] + pv
            new_l[h] = alphas[h] * l_run[h] + psum

        for stage in (score, softmax, values):
            for h in range(HEADS_PER_STEP):
                stage(h)
        return tuple(new_m) + tuple(new_l)

    out = lax.fori_loop(0, (i + span - 1) // span, body, tuple(m_run) + tuple(l_run))
    for h in range(HEADS_PER_STEP):
        acc_ref[head_rows(h), :] = acc_ref[head_rows(h), :] * (1.0 / out[4 + h])
    o_ref[...] = acc_ref[...].T.astype(o_ref.dtype)


def _moba_prompt(q16, k16, vt16, kmean, batch, seq):
    tq = MOBA_BLOCK
    nq = seq // tq
    ng = W_MIX // QUAD
    return pl.pallas_call(
        functools.partial(_moba_prompt_kernel, tq=tq),
        grid=(batch, ng, nq),
        in_specs=[pl.BlockSpec((tq, QUAD), lambda b, g, i: (b * nq + i, g)),
                  pl.BlockSpec((seq, QUAD), lambda b, g, i: (b, g)),
                  pl.BlockSpec((1, QUAD, seq), lambda b, g, i: (b, g, 0)),
                  pl.BlockSpec((nq, QUAD), lambda b, g, i: (b, g))],
        out_specs=pl.BlockSpec((tq, QUAD), lambda b, g, i: (b * nq + i, g)),
        out_shape=jax.ShapeDtypeStruct((batch * seq, W_MIX), BF16),
        scratch_shapes=[pltpu.VMEM((HEADS_PER_STEP, nq, tq), F32), pltpu.VMEM((QUAD, tq), F32)],
        compiler_params=_cparams(("parallel", "parallel", "arbitrary"), 48),
        name="moba_prompt",
    )(q16, k16, vt16, kmean)


def _mem_kv_kernel(mem_ref, g_ref, w_ref, k_ref, v_ref):
    kv = _dot(_rms(mem_ref[...], g_ref[...]).astype(BF16), w_ref[...])
    k_ref[...] = kv[:, :W_MEM]
    v_ref[...] = kv[:, W_MEM:]


def _mem_kv(mem, g, w_kv16):
    n = mem.shape[0]
    return pl.pallas_call(
        _mem_kv_kernel,
        out_shape=[jax.ShapeDtypeStruct((n, W_MEM), F32)] * 2,
        name="mem_kv",
    )(mem, g, w_kv16)


def _dot3(x, w_hi, w_lo):
    x_hi, x_lo = _split_bf16(x)
    return _dot(x_hi, w_hi) + (_dot(x_hi, w_lo) + _dot(x_lo, w_hi))


def _post_kernel(x_ref, osb_ref, omb_ref, gate_ref, wsb_ref, wmb_ref, wo_ref,
                 gmem_ref, wq_ref, mk_ref, mv_ref, wom_ref, gffn_ref,
                 wrg_hi_ref, wrg_lo_ref, brg_ref, wre_hi_ref, wre_lo_ref, bre_ref,
                 x2_ref, xn_ref, comb_ref, count_ref, *, rows_per_mem, keys_per_mem):
    gate = gate_ref[...]
    h = gate[:, :D_MODEL] * _dot(osb_ref[...], wsb_ref[...]) + gate[:, D_MODEL:] * _dot(omb_ref[...], wmb_ref[...])
    x1 = x_ref[...] + _dot(h.astype(BF16), wo_ref[...])

    q = (_dot(_rms(x1, gmem_ref[...]).astype(BF16), wq_ref[...]) * (HD_MEM ** -0.5)).astype(BF16)
    heads = []
    if rows_per_mem is not None:
        shape = (x1.shape[0], mk_ref.shape[1])
        same_mem = (lax.broadcasted_iota(I32, shape, 0) // rows_per_mem
                    == lax.broadcasted_iota(I32, shape, 1) // keys_per_mem)
    for hh in range(H_MEM):
        sl = slice(hh * HD_MEM, (hh + 1) * HD_MEM)
        s = _dot_nt(q[:, sl], mk_ref[0, :, sl])
        if rows_per_mem is not None:
            s = jnp.where(same_mem, s, NEG_INF)
        p = jnp.exp(s - jnp.max(s, axis=1, keepdims=True))
        p = p / jnp.sum(p, axis=1, keepdims=True)
        heads.append(_dot(p.astype(BF16), mv_ref[0, :, sl]))
    o = jnp.concatenate(heads, axis=1)
    x2 = x1 + _dot(o.astype(BF16), wom_ref[...])
    x2_ref[...] = x2

    xn = _rms(x2, gffn_ref[...])
    xn16 = xn.astype(BF16)

    lane = lax.broadcasted_iota(I32, (1, LANES), 1)
    gl = jnp.where(lane < N_GROUPS, _dot3(xn, wrg_hi_ref[...], wrg_lo_ref[...]) + brg_ref[...], NEG_INF)
    g_max, g_idx = _first_argmax(gl, lane, LANES)
    g_w = 1.0 / jnp.sum(jnp.exp(gl - g_max), axis=1, keepdims=True)
    el = _dot3(xn, wre_hi_ref[...], wre_lo_ref[...]) + bre_ref[...]
    el = jnp.where((lane // EXPERTS_PER_GROUP) == g_idx, el, NEG_INF)
    e_max, i1 = _first_argmax(el, lane, LANES)
    e_sum = jnp.sum(jnp.exp(el - e_max), axis=1, keepdims=True)
    el2 = jnp.where(lane == i1, NEG_INF, el)
    e_max2, i2 = _first_argmax(el2, lane, LANES)
    w1 = 1.0 / e_sum
    w2 = jnp.exp(e_max2 - e_max) / e_sum
    norm = w1 + w2
    comb = jnp.where(lane == i1, g_w * (w1 / norm), 0.0) + jnp.where(lane == i2, g_w * (w2 / norm), 0.0)
    xn_ref[...] = xn16
    comb_ref[...] = jnp.where(lane == GROUP_ID_LANE, g_idx.astype(F32), comb)
    count_ref[0] = jnp.sum((lane == g_idx).astype(F32), axis=0, keepdims=True)


GROUP_ID_LANE = N_EXPERTS
GROUP_RUN_ALIGN = 16
SORT_ROWS_PAD = 128
EXPERT_TILE = 512


def _run_starts(dst_ref, k):
    starts = [jnp.int32(0)]
    for g in range(N_GROUPS):
        starts.append(starts[-1] + (dst_ref[k + 1, g] - dst_ref[k, g]))
    return starts


def _scatter_kernel(dst_ref, xn_ref, comb_ref, xs_zero, cs_zero, pos_ref, xs_hbm, cs_hbm, local_x, local_c, sem):
    del xs_zero, cs_zero
    k = pl.program_id(0)
    last = pl.num_programs(0) - 1
    tm = xn_ref.shape[0]
    sorted_rows = tm + SORT_ROWS_PAD
    lane = lax.broadcasted_iota(I32, (1, LANES), 1)
    routed = comb_ref[...]
    g_idx = routed[:, GROUP_ID_LANE:GROUP_ID_LANE + 1].astype(I32)
    comb = jnp.where(lane < N_EXPERTS, routed, 0.0)

    def window_copies(g, src_row, dst_row):
        src = pl.ds(pl.multiple_of(src_row, GROUP_RUN_ALIGN), tm)
        dst = pl.ds(pl.multiple_of(dst_row, GROUP_RUN_ALIGN), tm)
        return (pltpu.make_async_copy(local_x.at[src], xs_hbm.at[dst], sem.at[0, g]),
                pltpu.make_async_copy(local_c.at[src], cs_hbm.at[dst], sem.at[1, g]))

    def wait_windows():
        for g in range(N_GROUPS):
            for c in window_copies(g, 0, 0):
                c.wait()

    @pl.when(k == 0)
    def _():
        local_x[...] = jnp.zeros_like(local_x)
        local_c[...] = jnp.zeros_like(local_c)

    onehot = (lane == g_idx).astype(BF16)
    earlier = (lax.broadcasted_iota(I32, (tm, tm), 0) > lax.broadcasted_iota(I32, (tm, tm), 1)).astype(BF16)
    before = _dot(earlier, onehot)
    rank = jnp.sum(jnp.where(lane == g_idx, before, 0.0), axis=1, keepdims=True).astype(I32)
    run_start = _run_starts(dst_ref, k)
    pos = rank
    for g in range(N_GROUPS):
        pos = pos + jnp.where(g_idx == g, run_start[g], 0)
    pos_ref[...] = jnp.broadcast_to(pos, pos_ref.shape)

    place_t = (lax.broadcasted_iota(I32, (tm, sorted_rows), 1) == pos).astype(F32)
    place = place_t.T.astype(BF16)
    sorted_x = _dot(place, xn_ref[...]).astype(BF16)
    own = jnp.zeros_like(comb)
    for g in range(N_GROUPS):
        shifted = comb if g == 0 else pltpu.roll(comb, LANES - g * EXPERTS_PER_GROUP, 1)
        own = jnp.where(g_idx == g, shifted, own)
    c_hi = own.astype(BF16)
    c_mid, c_lo = _split_bf16(own - c_hi.astype(F32))
    sorted_c = _dot(place, c_hi) + (_dot(place, c_mid) + _dot(place, c_lo))

    @pl.when(k > 0)
    def _():
        wait_windows()

    local_x[0:sorted_rows, :] = sorted_x
    local_c[0:sorted_rows, :] = sorted_c
    for g in range(N_GROUPS):
        for c in window_copies(g, run_start[g], dst_ref[k, g]):
            c.start()

    @pl.when(k == last)
    def _():
        wait_windows()


def _group_layout(counts, n, nt):
    rows = (counts + (GROUP_RUN_ALIGN - 1)) // GROUP_RUN_ALIGN * GROUP_RUN_ALIGN
    rel = jnp.concatenate([jnp.zeros((1, N_GROUPS), I32), jnp.cumsum(rows, axis=0)], axis=0)
    tiles = (rel[nt] + EXPERT_TILE - 1) // EXPERT_TILE + 1
    ends = jnp.cumsum(tiles)
    dst = (rel + ((ends - tiles) * EXPERT_TILE)[None, :]).astype(I32)
    n_steps = -(-(n + nt * N_GROUPS * (GROUP_RUN_ALIGN - 1)) // EXPERT_TILE) + 2 * N_GROUPS
    step = jnp.arange(n_steps, dtype=I32)
    group = jnp.sum(step[:, None] >= ends[None, :], axis=1).astype(I32)
    live = (group < N_GROUPS).astype(I32)
    return dst, n_steps * EXPERT_TILE, jnp.minimum(group, N_GROUPS - 1), live


def _scatter(dst, xn16, comb, total_rows):
    n = xn16.shape[0]
    tm = EXPERT_TILE
    row = lambda i, dst: (i, 0)
    any_spec = pl.BlockSpec(memory_space=pl.ANY)
    local_rows = tm + SORT_ROWS_PAD + tm
    grid_spec = pltpu.PrefetchScalarGridSpec(
        num_scalar_prefetch=1, grid=(n // tm,),
        in_specs=[pl.BlockSpec((tm, D_MODEL), row), pl.BlockSpec((tm, LANES), row), any_spec, any_spec],
        out_specs=[pl.BlockSpec((tm, LANES), row), any_spec, any_spec],
        scratch_shapes=[pltpu.VMEM((local_rows, D_MODEL), BF16), pltpu.VMEM((local_rows, LANES), F32),
                        pltpu.SemaphoreType.DMA((2, N_GROUPS))])
    return pl.pallas_call(
        _scatter_kernel,
        grid_spec=grid_spec,
        out_shape=[jax.ShapeDtypeStruct((n, LANES), I32), jax.ShapeDtypeStruct((total_rows, D_MODEL), BF16),
                   jax.ShapeDtypeStruct((total_rows, LANES), F32)],
        input_output_aliases={3: 1, 4: 2},
        compiler_params=_cparams(("arbitrary",), 32),
        name="scatter",
    )(dst, xn16, comb, jnp.zeros((total_rows, D_MODEL), BF16), jnp.zeros((total_rows, LANES), F32))


def _post(x, osb, omb, gate, mem_k16, mem_v16, rows_per_batch, tm, w, rows_per_mem=None, keys_per_mem=None):
    n = x.shape[0]
    nt = n // tm
    tiles_per_batch = rows_per_batch // tm
    row = lambda i: (i, 0)
    const = lambda i: (0, 0)
    mem = lambda i: (i // tiles_per_batch, 0, 0)
    n_mem = mem_k16.shape[1]
    full = lambda a: pl.BlockSpec(a.shape, const)
    out_specs = [pl.BlockSpec((tm, D_MODEL), row), pl.BlockSpec((tm, D_MODEL), row),
                 pl.BlockSpec((tm, LANES), row), pl.BlockSpec((1, 1, LANES), lambda i: (i, 0, 0))]
    out_shape = [jax.ShapeDtypeStruct((n, D_MODEL), F32), jax.ShapeDtypeStruct((n, D_MODEL), BF16),
                 jax.ShapeDtypeStruct((n, LANES), F32), jax.ShapeDtypeStruct((nt, 1, LANES), F32)]
    return pl.pallas_call(
        functools.partial(_post_kernel, rows_per_mem=rows_per_mem, keys_per_mem=keys_per_mem),
        grid=(nt,),
        in_specs=[pl.BlockSpec((tm, D_MODEL), row), pl.BlockSpec((tm, W_MIX), row), pl.BlockSpec((tm, W_MIX), row),
                  pl.BlockSpec((tm, 2 * D_MODEL), row),
                  full(w["w_out_sb"]), full(w["w_out_mb"]), full(w["w_out"]),
                  full(w["norm_mem_g"]), full(w["w_q_mem"]),
                  pl.BlockSpec((1, n_mem, W_MEM), mem), pl.BlockSpec((1, n_mem, W_MEM), mem),
                  full(w["w_o_mem"]), full(w["norm_ffn_g"]),
                  full(w["wrg_hi"]), full(w["wrg_lo"]), full(w["brg"]),
                  full(w["wre_hi"]), full(w["wre_lo"]), full(w["bre"])],
        out_specs=out_specs,
        out_shape=out_shape,
        compiler_params=_cparams(("parallel",), 48),
        name="post",
    )(x, osb, omb, gate, w["w_out_sb"], w["w_out_mb"], w["w_out"], w["norm_mem_g"], w["w_q_mem"],
      mem_k16, mem_v16, w["w_o_mem"], w["norm_ffn_g"],
      w["wrg_hi"], w["wrg_lo"], w["brg"], w["wre_hi"], w["wre_lo"], w["bre"])


def _moe_kernel(xn_ref, comb_ref, x2_ref, wg_ref, wu_ref, wd_ref, gfin_ref, y_ref, acc_ref):
    e = pl.program_id(1)

    @pl.when(e == 0)
    def _():
        acc_ref[...] = jnp.zeros_like(acc_ref)

    xn = xn_ref[...]
    hg = _dot(xn, wg_ref[0])
    hu = _dot(xn, wu_ref[0])
    lane = lax.broadcasted_iota(I32, (1, LANES), 1)
    weight = jnp.sum(jnp.where(lane == e, comb_ref[...], 0.0), axis=1, keepdims=True)
    hidden = (hg / (1.0 + jnp.exp(-hg))) * hu * weight
    acc_ref[...] += _dot(hidden.astype(BF16), wd_ref[0])

    @pl.when(e == pl.num_programs(1) - 1)
    def _():
        y_ref[...] = _rms(x2_ref[...] + acc_ref[...], gfin_ref[...])


def _moe(xn16, comb, x2, wg16, wu16, wd16, gfin, tm):
    n = xn16.shape[0]
    nt = n // tm
    row = lambda i, e: (i, 0)
    exp_w = lambda i, e: (e, 0, 0)
    return pl.pallas_call(
        _moe_kernel,
        grid=(nt, N_EXPERTS),
        in_specs=[pl.BlockSpec((tm, D_MODEL), row), pl.BlockSpec((tm, LANES), row), pl.BlockSpec((tm, D_MODEL), row),
                  pl.BlockSpec((1, D_MODEL, D_EXPERT), exp_w), pl.BlockSpec((1, D_MODEL, D_EXPERT), exp_w),
                  pl.BlockSpec((1, D_EXPERT, D_MODEL), exp_w),
                  pl.BlockSpec((1, D_MODEL), lambda i, e: (0, 0))],
        out_specs=pl.BlockSpec((tm, D_MODEL), row),
        out_shape=jax.ShapeDtypeStruct((n, D_MODEL), F32),
        scratch_shapes=[pltpu.VMEM((tm, D_MODEL), F32)],
        compiler_params=_cparams(("parallel", "arbitrary"), 48),
        name="moe",
    )(xn16, comb, x2, wg16, wu16, wd16, gfin)


def _group_experts_kernel(grp_ref, live_ref, xs_ref, cs_ref, wg_ref, wu_ref, wd_ref, ys_ref, acc_ref):
    j, e = pl.program_id(0), pl.program_id(1)

    @pl.when(live_ref[j] > 0)
    def _():
        @pl.when(e == 0)
        def _():
            acc_ref[...] = jnp.zeros_like(acc_ref)

        xs = xs_ref[...]
        hg = _dot(xs, wg_ref[0, 0])
        hu = _dot(xs, wu_ref[0, 0])
        lane = lax.broadcasted_iota(I32, (1, LANES), 1)
        weight = jnp.sum(jnp.where(lane == e, cs_ref[...], 0.0), axis=1, keepdims=True)
        hidden = (hg / (1.0 + jnp.exp(-hg))) * hu * weight
        acc_ref[...] += _dot(hidden.astype(BF16), wd_ref[0, 0])

        @pl.when(e == EXPERTS_PER_GROUP - 1)
        def _():
            ys_ref[...] = acc_ref[...]

    @pl.when(jnp.logical_and(live_ref[j] == 0, e == 0))
    def _():
        ys_ref[...] = jnp.zeros_like(ys_ref)


def _group_experts(xs, cs, tile_group, tile_live, wg16, wu16, wd16):
    n_steps = tile_group.shape[0]
    by_group = lambda a: a.reshape(N_GROUPS, EXPERTS_PER_GROUP, *a.shape[1:])
    rows = lambda j, e, grp, live: (j, 0)
    expert = lambda j, e, grp, live: (grp[j], e * live[j] + (EXPERTS_PER_GROUP - 1) * (1 - live[j]), 0, 0)
    grid_spec = pltpu.PrefetchScalarGridSpec(
        num_scalar_prefetch=2, grid=(n_steps, EXPERTS_PER_GROUP),
        in_specs=[pl.BlockSpec((EXPERT_TILE, D_MODEL), rows), pl.BlockSpec((EXPERT_TILE, LANES), rows),
                  pl.BlockSpec((1, 1, D_MODEL, D_EXPERT), expert), pl.BlockSpec((1, 1, D_MODEL, D_EXPERT), expert),
                  pl.BlockSpec((1, 1, D_EXPERT, D_MODEL), expert)],
        out_specs=pl.BlockSpec((EXPERT_TILE, D_MODEL), rows),
        scratch_shapes=[pltpu.VMEM((EXPERT_TILE, D_MODEL), F32)])
    return pl.pallas_call(
        _group_experts_kernel,
        grid_spec=grid_spec,
        out_shape=jax.ShapeDtypeStruct((xs.shape[0], D_MODEL), F32),
        compiler_params=_cparams(("arbitrary", "arbitrary"), 48),
        name="group_experts",
    )(tile_group, tile_live, xs, cs, by_group(wg16), by_group(wu16), by_group(wd16))


def _gather_norm_kernel(dst_ref, pos_ref, x2_ref, gfin_ref, ys_hbm, y_ref, stage, local, sem):
    k = pl.program_id(0)
    tm = x2_ref.shape[0]
    sorted_rows = tm + SORT_ROWS_PAD
    slot = k % 2

    def window_copies(step, into):
        return [pltpu.make_async_copy(
            ys_hbm.at[pl.ds(pl.multiple_of(dst_ref[step, g], GROUP_RUN_ALIGN), tm)],
            stage.at[into, g], sem.at[into, g]) for g in range(N_GROUPS)]

    @pl.when(k == 0)
    def _():
        local[...] = jnp.zeros_like(local)
        for c in window_copies(0, 0):
            c.start()

    @pl.when(k + 1 < pl.num_programs(0))
    def _():
        for c in window_copies(k + 1, 1 - slot):
            c.start()

    for c in window_copies(k, slot):
        c.wait()
    run_start = _run_starts(dst_ref, k)
    for g in range(N_GROUPS):
        local[pl.ds(pl.multiple_of(run_start[g], GROUP_RUN_ALIGN), tm), :] = stage[slot, g]
    local[pl.ds(pl.multiple_of(run_start[N_GROUPS], GROUP_RUN_ALIGN), tm), :] = jnp.zeros((tm, local.shape[1]), F32)

    pick = (lax.broadcasted_iota(I32, (tm, sorted_rows), 1) == pos_ref[:, 0:1]).astype(BF16)
    hi, lo = _split_bf16(local[0:sorted_rows, :])
    y_ref[...] = _rms(x2_ref[...] + (_dot(pick, hi) + _dot(pick, lo)), gfin_ref[...])


def _gather_norm(dst, pos, x2, gfin, ys):
    n = x2.shape[0]
    tm = EXPERT_TILE
    nt = n // tm
    row = lambda i, offs: (i, 0)
    grid_spec = pltpu.PrefetchScalarGridSpec(
        num_scalar_prefetch=1, grid=(nt,),
        in_specs=[pl.BlockSpec((tm, LANES), row), pl.BlockSpec((tm, D_MODEL), row),
                  pl.BlockSpec((1, D_MODEL), lambda i, offs: (0, 0)), pl.BlockSpec(memory_space=pl.ANY)],
        out_specs=pl.BlockSpec((tm, D_MODEL), row),
        scratch_shapes=[pltpu.VMEM((2, N_GROUPS, tm, D_MODEL), F32),
                        pltpu.VMEM((tm + SORT_ROWS_PAD + tm, D_MODEL), F32),
                        pltpu.SemaphoreType.DMA((2, N_GROUPS))])
    return pl.pallas_call(
        _gather_norm_kernel,
        grid_spec=grid_spec,
        out_shape=jax.ShapeDtypeStruct((n, D_MODEL), F32),
        compiler_params=_cparams(("arbitrary",), 56),
        name="gather_norm",
    )(dst, pos, x2, gfin, ys)


def _own_head_block(full, n_new):
    row_head = lax.broadcasted_iota(I32, (full.shape[0], 1), 0) // n_new
    out = jnp.zeros((full.shape[0], HEAD_DIM), F32)
    for h in range(N_HEADS):
        out = jnp.where(row_head == h, full[:, h * HEAD_DIM:(h + 1) * HEAD_DIM], out)
    return out


def _sb_sample_kernel(pt_ref, q_ref, kn_ref, vn_ref, kc_hbm, vc_hbm, o_ref, kbuf, vbuf, sem, *, n_new, n_pages):
    b = pl.program_id(0)
    q = q_ref[0]
    rows = q.shape[0]
    later = _later_keys(PAGE_SIZE)

    def page_copies(p, slot):
        page = pt_ref[b, p]
        return (pltpu.make_async_copy(kc_hbm.at[page], kbuf.at[slot], sem.at[0, slot]),
                pltpu.make_async_copy(vc_hbm.at[page], vbuf.at[slot], sem.at[1, slot]))

    def start(p, slot):
        for c in page_copies(p, slot):
            c.start()

    def wait(p, slot):
        for c in page_copies(p, slot):
            c.wait()

    slot_of = lambda p: (n_pages - 1 - p) % 2
    start(n_pages - 1, 0)

    key_slot = lax.broadcasted_iota(I32, (rows, PAGE_SIZE), 1)
    tok = lax.broadcasted_iota(I32, (rows, PAGE_SIZE), 0) % n_new
    a, run = _stick_breaking_tile(_dot(q, kn_ref[0].astype(BF16)), later, jnp.zeros((rows, 1), F32), key_slot < tok)
    acc = _dot_nt(a.astype(BF16), vn_ref[0].astype(BF16))

    def cond(carry):
        p, alive = carry[0], carry[1]
        return jnp.logical_and(p >= 0, alive > 0)

    def body(carry):
        p, _, run, acc = carry
        slot = slot_of(p)

        @pl.when(p > 0)
        def _():
            start(p - 1, 1 - slot)

        wait(p, slot)
        z = _dot(q, kbuf[slot].reshape(W_MIX, PAGE_SIZE).astype(BF16))
        a, run = _stick_breaking_tile(z, later, run, None)
        acc = acc + _dot_nt(a.astype(BF16), vbuf[slot].reshape(W_MIX, PAGE_SIZE).astype(BF16))
        alive = (jnp.max(run) > SB_DEAD).astype(I32)
        return (p - 1, alive, run, acc)

    p_next, _, _, acc = lax.while_loop(cond, body, (jnp.int32(n_pages - 1), jnp.int32(1), run, acc))

    @pl.when(p_next >= 0)
    def _():
        wait(p_next, slot_of(p_next))

    o_ref[0] = _own_head_block(acc, n_new)


def _sb_sample(page_table, q_bd, kn_t, vn_t, cache_kt, cache_vt, n_new):
    nb, n_pages = page_table.shape
    rows = N_HEADS * n_new
    per_b = lambda b, pt: (b, 0, 0)
    page_buffers = pltpu.VMEM((2, N_HEADS, HEAD_DIM, PAGE_SIZE), F32)
    grid_spec = pltpu.PrefetchScalarGridSpec(
        num_scalar_prefetch=1, grid=(nb,),
        in_specs=[pl.BlockSpec((1, rows, W_MIX), per_b),
                  pl.BlockSpec((1, W_MIX, PAGE_SIZE), per_b),
                  pl.BlockSpec((1, W_MIX, PAGE_SIZE), per_b),
                  pl.BlockSpec(memory_space=pl.ANY),
                  pl.BlockSpec(memory_space=pl.ANY)],
        out_specs=pl.BlockSpec((1, rows, HEAD_DIM), per_b),
        scratch_shapes=[page_buffers, page_buffers, pltpu.SemaphoreType.DMA((2, 2))])
    return pl.pallas_call(
        functools.partial(_sb_sample_kernel, n_new=n_new, n_pages=n_pages),
        grid_spec=grid_spec,
        out_shape=jax.ShapeDtypeStruct((nb, rows, HEAD_DIM), F32),
        compiler_params=_cparams(("arbitrary",), 32),
        name="sb_sample",
    )(page_table, q_bd, kn_t, vn_t, cache_kt, cache_vt)


PAGES_PER_BLOCK = MOBA_BLOCK // PAGE_SIZE
MOBA_SAMPLE_BLOCKS_PER_STEP = 4


def _moba_sample_kernel(pt_ref, q_ref, kn_ref, vn_ref, *refs, n_new):
    pages_per_step = MOBA_SAMPLE_BLOCKS_PER_STEP * PAGES_PER_BLOCK
    k_refs, v_refs = refs[:pages_per_step], refs[pages_per_step:2 * pages_per_step]
    o_ref, pm_ref, pl_ref, pg_ref, po_ref = refs[2 * pages_per_step:]
    step = pl.program_id(1)
    n_blocks = pl.num_programs(1) * MOBA_SAMPLE_BLOCKS_PER_STEP
    q = q_ref[0]
    rows = q.shape[0]
    block_lane = lax.broadcasted_iota(I32, (1, LANES), 1)

    @pl.when(step == 0)
    def _():
        pm_ref[...] = jnp.full(pm_ref.shape, NEG_INF, F32)
        pl_ref[...] = jnp.zeros_like(pl_ref)
        pg_ref[...] = jnp.zeros_like(pg_ref)

    def block_pages(page_refs, j):
        pages = [r[0].reshape(W_MIX, PAGE_SIZE).astype(BF16)
                 for r in page_refs[j * PAGES_PER_BLOCK:(j + 1) * PAGES_PER_BLOCK]]
        return jnp.concatenate(pages, axis=1)

    pm, pl_, pg = pm_ref[...], pl_ref[...], pg_ref[...]
    blocks = range(MOBA_SAMPLE_BLOCKS_PER_STEP)
    scores = [_dot(q, block_pages(k_refs, j)) for j in blocks]
    maxes = [jnp.max(z, axis=1, keepdims=True) for z in scores]
    weights = [jnp.exp2(z - m) for z, m in zip(scores, maxes)]
    for j in blocks:
        blk = step * MOBA_SAMPLE_BLOCKS_PER_STEP + j
        here = block_lane == blk
        pm = jnp.where(here, maxes[j], pm)
        pl_ = jnp.where(here, jnp.sum(weights[j], axis=1, keepdims=True), pl_)
        pg = jnp.where(here, jnp.sum(scores[j], axis=1, keepdims=True), pg)
        po_ref[blk] = _dot_nt(weights[j].astype(BF16), block_pages(v_refs, j))
    pm_ref[...], pl_ref[...], pg_ref[...] = pm, pl_, pg

    @pl.when(step == pl.num_programs(1) - 1)
    def _():
        chosen = _top_blocks(pg, block_lane < n_blocks, block_lane)
        slot = lax.broadcasted_iota(I32, (rows, PAGE_SIZE), 1)
        tok = lax.broadcasted_iota(I32, (rows, PAGE_SIZE), 0) % n_new
        zn = jnp.where(slot <= tok, _dot(q, kn_ref[0].astype(BF16)), NEG_INF)
        m_all = jnp.maximum(jnp.max(jnp.where(chosen, pm, NEG_INF), axis=1, keepdims=True),
                            jnp.max(zn, axis=1, keepdims=True))
        pn = jnp.exp2(zn - m_all)
        w = jnp.where(chosen, jnp.exp2(pm - m_all), 0.0)
        total = jnp.sum(w * pl_, axis=1, keepdims=True) + jnp.sum(pn, axis=1, keepdims=True)
        acc = _dot_nt(pn.astype(BF16), vn_ref[0].astype(BF16))
        for n in range(po_ref.shape[0]):
            acc = acc + w[:, n:n + 1] * po_ref[n]
        o_ref[0] = _own_head_block(acc, n_new) / total


def _moba_sample(page_table, q_bd, kn_t, vn_t, cache_kt, cache_vt, n_new):
    nb, n_pages = page_table.shape
    rows = N_HEADS * n_new
    pages_per_step = MOBA_SAMPLE_BLOCKS_PER_STEP * PAGES_PER_BLOCK
    n_steps = n_pages // pages_per_step
    per_b = lambda b, s, pt: (b, 0, 0)
    page_spec = lambda j: pl.BlockSpec((1, N_HEADS, HEAD_DIM, PAGE_SIZE),
                                       lambda b, s, pt: (pt[b, s * pages_per_step + j], 0, 0, 0))
    page_specs = [page_spec(j) for j in range(pages_per_step)]
    grid_spec = pltpu.PrefetchScalarGridSpec(
        num_scalar_prefetch=1, grid=(nb, n_steps),
        in_specs=[pl.BlockSpec((1, rows, W_MIX), per_b),
                  pl.BlockSpec((1, W_MIX, PAGE_SIZE), per_b),
                  pl.BlockSpec((1, W_MIX, PAGE_SIZE), per_b)] + page_specs + page_specs,
        out_specs=pl.BlockSpec((1, rows, HEAD_DIM), per_b),
        scratch_shapes=[pltpu.VMEM((rows, LANES), F32), pltpu.VMEM((rows, LANES), F32),
                        pltpu.VMEM((rows, LANES), F32),
                        pltpu.VMEM((n_pages // PAGES_PER_BLOCK, rows, W_MIX), F32)])
    return pl.pallas_call(
        functools.partial(_moba_sample_kernel, n_new=n_new),
        grid_spec=grid_spec,
        out_shape=jax.ShapeDtypeStruct((nb, rows, HEAD_DIM), F32),
        compiler_params=_cparams(("parallel", "arbitrary"), 32),
        name="moba_sample",
    )(page_table, q_bd, kn_t, vn_t, *([cache_kt] * pages_per_step), *([cache_vt] * pages_per_step))


def _rope_tables(pos):
    half = HEAD_DIM // 2
    inv_freq = ROPE_THETA ** (-jnp.arange(half, dtype=F32) / half)
    ang = pos.astype(F32)[:, None] * inv_freq[None, :]
    cos = jnp.cos(ang)
    sin = jnp.sin(ang)
    heads_per_tile = LANES // HEAD_DIM
    return (jnp.tile(jnp.concatenate([cos, cos], axis=1), (1, heads_per_tile)),
            jnp.tile(jnp.concatenate([-sin, sin], axis=1), (1, heads_per_tile)))


def _pad_lanes(a):
    return jnp.pad(a, ((0, 0), (0, LANES - a.shape[1])))


def _prepare_weights(norm_mix_g, w_in, b_gate, w_out_sb, w_out_mb, w_out, norm_mem_g, norm_memsrc_g, w_q_mem,
                     w_kv_mem, w_o_mem, norm_ffn_g, w_router_group, b_router_group, w_router_expert,
                     b_router_expert, w_gate_e, w_up_e, w_down_e, norm_final_g):
    row = lambda v: v.reshape(1, -1).astype(F32)
    wrg_hi, wrg_lo = _split_bf16(_pad_lanes(w_router_group))
    w_re = w_router_expert.transpose(1, 0, 2).reshape(D_MODEL, N_EXPERTS)
    wre_hi, wre_lo = _split_bf16(_pad_lanes(w_re))
    return dict(
        norm_mix_g=row(norm_mix_g), w_in=w_in.astype(BF16), b_gate=row(b_gate),
        w_out_sb=w_out_sb.astype(BF16), w_out_mb=w_out_mb.astype(BF16), w_out=w_out.astype(BF16),
        norm_mem_g=row(norm_mem_g), norm_memsrc_g=row(norm_memsrc_g), w_q_mem=w_q_mem.astype(BF16),
        w_kv_mem=w_kv_mem.astype(BF16), w_o_mem=w_o_mem.astype(BF16), norm_ffn_g=row(norm_ffn_g),
        wrg_hi=wrg_hi, wrg_lo=wrg_lo, brg=_pad_lanes(row(b_router_group)),
        wre_hi=wre_hi, wre_lo=wre_lo, bre=_pad_lanes(row(b_router_expert)),
        w_gate_e=w_gate_e.astype(BF16), w_up_e=w_up_e.astype(BF16), w_down_e=w_down_e.astype(BF16),
        norm_final_g=row(norm_final_g))


def _tail(x, osb, omb, gate, mem_k16, mem_v16, rows_per_batch, w, tm_post, tm_moe, **mem_mask):
    x2, xn16, comb, _ = _post(x, osb, omb, gate, mem_k16, mem_v16, rows_per_batch, tm_post, w, **mem_mask)
    return _moe(xn16, comb, x2, w["w_gate_e"], w["w_up_e"], w["w_down_e"], w["norm_final_g"], tm_moe)


def kernel(x_prompt, x_sample, mem_prompt, cache_sb_k, cache_sb_v, cache_mb_k, cache_mb_v, cache_mem_k, cache_mem_v, page_table, norm_mix_g, w_in, b_gate, w_out_sb, w_out_mb, w_out, norm_mem_g, norm_memsrc_g, w_q_mem, w_kv_mem, w_o_mem, norm_ffn_g, w_router_group, b_router_group, w_router_expert, b_router_expert, w_gate_e, w_up_e, w_down_e, norm_final_g):
    w = _prepare_weights(norm_mix_g, w_in, b_gate, w_out_sb, w_out_mb, w_out, norm_mem_g, norm_memsrc_g, w_q_mem,
                         w_kv_mem, w_o_mem, norm_ffn_g, w_router_group, b_router_group, w_router_expert,
                         b_router_expert, w_gate_e, w_up_e, w_down_e, norm_final_g)
    batch, seq, _ = x_prompt.shape
    dec_batch, n_new, _ = x_sample.shape
    n_mem = mem_prompt.shape[1]
    n_pages = page_table.shape[1]
    past_len = n_pages * PAGE_SIZE
    assert seq % (MOBA_BLOCK * MOBA_BLOCKS_PER_ITER) == 0 and seq % (SB_TILE * SB_TILES_PER_STEP) == 0
    assert n_new <= PAGE_SIZE and n_pages // PAGES_PER_BLOCK <= LANES
    assert n_pages % (MOBA_SAMPLE_BLOCKS_PER_STEP * PAGES_PER_BLOCK) == 0
    heads = lambda t, b, s: t.reshape(b, s, N_HEADS, HEAD_DIM)

    xp = x_prompt.reshape(batch * seq, D_MODEL)
    cos_p, sin_p = _rope_tables(jnp.arange(seq, dtype=I32))
    (qsb, qmb, gate, ksb_t, vsb_t, kmb_t, vmb_t, ksb16, vsb16_t, kmb16, vmb16_t, kmean) = _proj(
        xp, w["norm_mix_g"], w["w_in"], w["b_gate"], cos_p, sin_p, MOBA_BLOCK, rows_per_batch=seq)
    osb = _sb_prompt(qsb, ksb16, vsb16_t, batch, seq, SB_TILE)
    omb = _moba_prompt(qmb, kmb16, vmb16_t, kmean.reshape(batch * seq // MOBA_BLOCK, W_MIX), batch, seq)
    mem_k, mem_v = _mem_kv(mem_prompt.reshape(batch * n_mem, D_MODEL), w["norm_memsrc_g"], w["w_kv_mem"])
    x2, xn16, comb, counts = _post(xp, osb, omb, gate, mem_k.astype(BF16).reshape(batch, n_mem, W_MEM),
                                   mem_v.astype(BF16).reshape(batch, n_mem, W_MEM), seq, EXPERT_TILE, w)
    n_tiles = batch * seq // EXPERT_TILE
    dst, total_rows, tile_group, tile_live = _group_layout(
        counts[:, 0, :N_GROUPS].astype(I32), batch * seq, n_tiles)
    pos, rows_by_group, weights_by_group = _scatter(dst, xn16, comb, total_rows)
    expert_out = _group_experts(rows_by_group, weights_by_group, tile_group, tile_live,
                                w["w_gate_e"], w["w_up_e"], w["w_down_e"])
    y_prompt = _gather_norm(dst, pos, x2, w["norm_final_g"], expert_out)

    rows_s = dec_batch * n_new
    xs = x_sample.reshape(rows_s, D_MODEL)
    cos_s, sin_s = _rope_tables(past_len + (jnp.arange(rows_s, dtype=I32) % n_new))
    (qsb_s, qmb_s, gate_s, ksb_s, vsb_s, kmb_s, vmb_s) = _proj(
        xs, w["norm_mix_g"], w["w_in"], w["b_gate"], cos_s, sin_s, rows_s)

    def block_diagonal(q16):
        q_cols = heads(q16, dec_batch, n_new).transpose(0, 2, 1, 3).reshape(dec_batch, N_HEADS * n_new, HEAD_DIM)
        own = (jnp.arange(N_HEADS * n_new)[:, None] // n_new) == (jnp.arange(W_MIX)[None, :] // HEAD_DIM)
        return jnp.where(own[None], jnp.tile(q_cols, (1, 1, N_HEADS)), jnp.zeros((), q16.dtype))

    def new_page(t):
        t = t.reshape(dec_batch, n_new, W_MIX).transpose(0, 2, 1)
        return jnp.pad(t, ((0, 0), (0, 0), (0, PAGE_SIZE - n_new)))

    def token_rows(o):
        o = o.reshape(dec_batch, N_HEADS, n_new, HEAD_DIM).transpose(0, 2, 1, 3)
        return o.reshape(rows_s, W_MIX).astype(BF16)

    pages = lambda c: c.transpose(0, 2, 3, 1)
    osb_s = token_rows(_sb_sample(page_table, block_diagonal(qsb_s), new_page(ksb_s), new_page(vsb_s),
                                  pages(cache_sb_k), pages(cache_sb_v), n_new))
    omb_s = token_rows(_moba_sample(page_table, block_diagonal(qmb_s), new_page(kmb_s), new_page(vmb_s),
                                    pages(cache_mb_k), pages(cache_mb_v), n_new))
    y_sample = _tail(xs, osb_s, omb_s, gate_s,
                     cache_mem_k.astype(BF16).reshape(1, dec_batch * n_mem, W_MEM),
                     cache_mem_v.astype(BF16).reshape(1, dec_batch * n_mem, W_MEM),
                     rows_s, w, rows_s, rows_s, rows_per_mem=n_new, keys_per_mem=n_mem)

    mem_heads = lambda t: t.reshape(batch, n_mem, H_MEM, HD_MEM)
    from_head_major = lambda t: t.reshape(batch, N_HEADS, HEAD_DIM, seq).transpose(0, 3, 1, 2)
    return (y_prompt.reshape(batch, seq, D_MODEL), y_sample.reshape(dec_batch, n_new, D_MODEL),
            from_head_major(ksb_t), from_head_major(vsb_t), from_head_major(kmb_t), from_head_major(vmb_t),
            mem_heads(mem_k), mem_heads(mem_v),
            heads(ksb_s, dec_batch, n_new), heads(vsb_s, dec_batch, n_new),
            heads(kmb_s, dec_batch, n_new), heads(vmb_s, dec_batch, n_new))
```

```python
import functools

import jax
import jax.numpy as jnp
from jax import lax
from jax.experimental import pallas as pl
from jax.experimental.pallas import tpu as pltpu

F32 = jnp.float32
BF16 = jnp.bfloat16
I32 = jnp.int32

D_MODEL = 1024
N_HEADS = 8
HEAD_DIM = 64
W_MIX = N_HEADS * HEAD_DIM
PAGE_SIZE = 128
MOBA_BLOCK = 256
MOBA_TOPK = 3
H_MEM = 4
HD_MEM = 128
W_MEM = H_MEM * HD_MEM
N_GROUPS = 4
EXPERTS_PER_GROUP = 8
N_EXPERTS = N_GROUPS * EXPERTS_PER_GROUP
D_EXPERT = 256
ROPE_THETA = 10000.0
RMS_EPS = 1e-6
W_IN_COLS = 6 * W_MIX + 2 * D_MODEL

LOG2_E = 1.4426950408889634
LANES = 128
BF16_ROWS = 16
HEADS_PER_STEP = 4
QUAD = HEADS_PER_STEP * HEAD_DIM
SB_DEAD = -160.0
MOBA_TILES_PER_STEP = 4
MOBA_BLOCKS_PER_ITER = 2
SB_TILE = 128
SB_TILES_PER_STEP = 4
NEG_INF = float("-inf")
MIB = 1024 * 1024


def _cparams(semantics, vmem_mib):
    return pltpu.CompilerParams(dimension_semantics=semantics, vmem_limit_bytes=vmem_mib * MIB)


def _rms(x, g):
    ms = jnp.mean(x * x, axis=-1, keepdims=True)
    return (x * lax.rsqrt(ms + RMS_EPS)) * g


def _dot(a, b):
    return jnp.dot(a, b, preferred_element_type=F32)


def _dot_nt(a, b):
    return lax.dot_general(a, b, (((1,), (1,)), ((), ())), preferred_element_type=F32)


def _split_bf16(x):
    hi = x.astype(BF16)
    lo = (x - hi.astype(F32)).astype(BF16)
    return hi, lo


def _later_keys(n):
    return (lax.broadcasted_iota(I32, (n, n), 0) > lax.broadcasted_iota(I32, (n, n), 1)).astype(BF16)


def _stick_breaking_tile(z, later, carried, valid):
    l1p = jnp.log1p(jnp.exp(-jnp.abs(z)))
    log_keep = -(jnp.maximum(z, 0.0) + l1p)
    if valid is not None:
        log_keep = jnp.where(valid, log_keep, 0.0)
    hi, lo = _split_bf16(log_keep)
    inner = _dot(hi, later) + _dot(lo, later)
    a = jnp.exp(jnp.minimum(z, 0.0) - l1p + carried + inner)
    if valid is not None:
        a = jnp.where(valid, a, 0.0)
    return a, carried + inner[:, 0:1] + log_keep[:, 0:1]


def _proj_kernel(x_ref, g_ref, w_ref, bg_ref, cos_ref, sin_ref, qsb_ref, qmb_ref, gate_ref, *kv_refs, head_major):
    xb = _rms(x_ref[...], g_ref[...]).astype(BF16)

    def seg(lo, width):
        return _dot(xb, w_ref[:, lo:lo + width])

    lane = lax.broadcasted_iota(I32, (1, W_MIX), 1)
    first_half = (lane % HEAD_DIM) < (HEAD_DIM // 2)
    cos = jnp.concatenate([cos_ref[...]] * (W_MIX // LANES), axis=1)
    sin = jnp.concatenate([sin_ref[...]] * (W_MIX // LANES), axis=1)

    def rope(t):
        partner = jnp.where(first_half, pltpu.roll(t, W_MIX - HEAD_DIM // 2, 1),
                            pltpu.roll(t, HEAD_DIM // 2, 1))
        return t * cos + partner * sin

    scale = HEAD_DIM ** -0.5
    qsb_ref[...] = (seg(0, W_MIX) * scale).astype(BF16)
    qmb_ref[...] = (rope(seg(3 * W_MIX, W_MIX)) * (scale * LOG2_E)).astype(BF16)
    gl = seg(6 * W_MIX, 2 * D_MODEL) + bg_ref[...]
    gate_ref[...] = 1.0 / (1.0 + jnp.exp(-gl))
    ksb = seg(W_MIX, W_MIX)
    vsb = seg(2 * W_MIX, W_MIX)
    kmb = rope(seg(4 * W_MIX, W_MIX))
    vmb = seg(5 * W_MIX, W_MIX)
    if not head_major:
        for ref, t in zip(kv_refs, (ksb, vsb, kmb, vmb)):
            ref[...] = t
        return
    ksb_t_ref, vsb_t_ref, kmb_t_ref, vmb_t_ref, ksb16_ref, vsb16_t_ref, kmb16_ref, vmb16_t_ref, kmean_ref = kv_refs
    ksb_t_ref[0] = ksb.T
    vsb_t = vsb.T
    vsb_t_ref[0] = vsb_t
    kmb_t_ref[0] = kmb.T
    vmb_t = vmb.T
    vmb_t_ref[0] = vmb_t
    ksb16_ref[...] = ksb.astype(BF16)
    vsb16_t_ref[0] = vsb_t.astype(BF16)
    kmb16_ref[...] = kmb.astype(BF16)
    vmb16_t_ref[0] = vmb_t.astype(BF16)
    kmean_ref[0] = jnp.mean(kmb, axis=0, keepdims=True)


def _proj(x, g, w_in16, b_gate, cos, sin, tm, rows_per_batch=None):
    n = x.shape[0]
    nt = n // tm
    n_rope = cos.shape[0] // tm
    row = lambda i: (i, 0)
    const = lambda i: (0, 0)
    head_major = rows_per_batch is not None
    out_shape = [jax.ShapeDtypeStruct((n, W_MIX), BF16)] * 2 + [jax.ShapeDtypeStruct((n, 2 * D_MODEL), F32)]
    out_specs = [pl.BlockSpec((tm, W_MIX), row)] * 2 + [pl.BlockSpec((tm, 2 * D_MODEL), row)]
    if head_major:
        tiles = rows_per_batch // tm
        n_batch = n // rows_per_batch
        t_spec = pl.BlockSpec((1, W_MIX, tm), lambda i: (i // tiles, 0, i % tiles))
        rows16 = jax.ShapeDtypeStruct((n, W_MIX), BF16)
        t16 = jax.ShapeDtypeStruct((n_batch, W_MIX, rows_per_batch), BF16)
        out_shape += ([jax.ShapeDtypeStruct((n_batch, W_MIX, rows_per_batch), F32)] * 4
                      + [rows16, t16, rows16, t16, jax.ShapeDtypeStruct((nt, 1, W_MIX), F32)])
        r_spec = pl.BlockSpec((tm, W_MIX), row)
        out_specs += ([t_spec] * 4 + [r_spec, t_spec, r_spec, t_spec,
                                      pl.BlockSpec((1, 1, W_MIX), lambda i: (i, 0, 0))])
    else:
        out_shape += [jax.ShapeDtypeStruct((n, W_MIX), F32)] * 4
        out_specs += [pl.BlockSpec((tm, W_MIX), row)] * 4
    return pl.pallas_call(
        functools.partial(_proj_kernel, head_major=head_major),
        grid=(nt,),
        in_specs=[pl.BlockSpec((tm, D_MODEL), row),
                  pl.BlockSpec((1, D_MODEL), const),
                  pl.BlockSpec((D_MODEL, W_IN_COLS), const),
                  pl.BlockSpec((1, 2 * D_MODEL), const),
                  pl.BlockSpec((tm, LANES), lambda i: (i % n_rope, 0)),
                  pl.BlockSpec((tm, LANES), lambda i: (i % n_rope, 0))],
        out_specs=out_specs,
        out_shape=out_shape,
        compiler_params=_cparams(("parallel",), 48),
        name="proj",
    )(x, g, w_in16, b_gate, cos, sin)


def _head_masks(width):
    lane_head = lax.broadcasted_iota(I32, (1, width), 1) // HEAD_DIM
    return [lane_head == h for h in range(HEADS_PER_STEP)]


def _sb_prompt_kernel(q_ref, k_ref, vt_ref, o_ref, acc_ref, *, tq):
    i = pl.program_id(2)
    masks = _head_masks(QUAD)
    tiles = range(SB_TILES_PER_STEP)
    chains = [(t, h) for t in tiles for h in range(HEADS_PER_STEP)]
    q = [q_ref[t * tq:(t + 1) * tq, :] for t in tiles]
    qs = {(t, h): jnp.where(masks[h], q[t], jnp.zeros_like(q[t])) for t, h in chains}
    key = lax.broadcasted_iota(I32, (tq, tq), 0)
    qry = lax.broadcasted_iota(I32, (tq, tq), 1)
    after = (qry > key).astype(BF16)

    def block(step, dead, diagonal):
        kblk, vtblk = [], []
        for t in tiles:
            kb = i * SB_TILES_PER_STEP + t - step
            if not diagonal:
                dead = {c: (jnp.where(kb < 0, jnp.inf, d) if c[0] == t else d) for c, d in dead.items()}
                kb = jnp.maximum(kb, 0)
            start = pl.multiple_of(kb * tq, tq)
            kblk.append(k_ref[pl.ds(start, tq), :])
            vtblk.append(vt_ref[0, :, pl.ds(start, tq)])
        scores = {c: _dot_nt(kblk[c[0]], qs[c]) for c in chains}
        softplus, parts = {}, {}
        for c in chains:
            z = scores[c]
            sp = jnp.maximum(z, 0.0) + jnp.log(1.0 + jnp.exp(-jnp.abs(z)))
            if diagonal:
                sp = jnp.where(key < qry, sp, 0.0)
            softplus[c] = sp
            parts[c] = _split_bf16(sp)
        inner = {c: _dot(after, parts[c][0]) + _dot(after, parts[c][1]) for c in chains}
        new_dead = {}
        for c in chains:
            t, h = c
            a = jnp.exp(scores[c] - softplus[c] - (dead[c] + inner[c]))
            if diagonal:
                a = jnp.where(key < qry, a, 0.0)
            rows, cols = slice(h * HEAD_DIM, (h + 1) * HEAD_DIM), slice(t * tq, (t + 1) * tq)
            acc_ref[rows, cols] += _dot(vtblk[t][rows, :], a.astype(BF16))
            new_dead[c] = dead[c] + inner[c][0:1] + softplus[c][0:1]
        return new_dead

    def alive(dead):
        low = functools.reduce(jnp.minimum, [dead[c] for c in chains])
        return (jnp.min(low) < -SB_DEAD).astype(I32)

    acc_ref[...] = jnp.zeros_like(acc_ref)
    dead = block(0, {c: jnp.zeros((1, tq), F32) for c in chains}, True)
    newest = i * SB_TILES_PER_STEP + SB_TILES_PER_STEP - 1

    def cond(carry):
        return jnp.logical_and(carry[0] <= newest, carry[1] > 0)

    def body(carry):
        dead = block(carry[0], dict(zip(chains, carry[2:])), False)
        return (carry[0] + 1, alive(dead)) + tuple(dead[c] for c in chains)

    lax.while_loop(cond, body, (jnp.int32(1), alive(dead)) + tuple(dead[c] for c in chains))
    o_ref[...] = acc_ref[...].T.astype(o_ref.dtype)


def _sb_prompt(q16, k16, vt16, batch, seq, tq):
    rows = tq * SB_TILES_PER_STEP
    nq = seq // rows
    ng = W_MIX // QUAD
    return pl.pallas_call(
        functools.partial(_sb_prompt_kernel, tq=tq),
        grid=(batch, ng, nq),
        in_specs=[pl.BlockSpec((rows, QUAD), lambda b, g, i: (b * nq + i, g)),
                  pl.BlockSpec((seq, QUAD), lambda b, g, i: (b, g)),
                  pl.BlockSpec((1, QUAD, seq), lambda b, g, i: (b, g, 0))],
        out_specs=pl.BlockSpec((rows, QUAD), lambda b, g, i: (b * nq + i, g)),
        out_shape=jax.ShapeDtypeStruct((batch * seq, W_MIX), BF16),
        scratch_shapes=[pltpu.VMEM((QUAD, rows), F32)],
        compiler_params=_cparams(("parallel", "parallel", "arbitrary"), 48),
        name="sb_prompt",
    )(q16, k16, vt16)


def _first_argmax(x, lane, width):
    mx = jnp.max(x, axis=1, keepdims=True)
    idx = jnp.min(jnp.where(x == mx, lane, width), axis=1, keepdims=True)
    return mx, idx


def _top_blocks(gate, eligible, blk):
    g = jnp.where(eligible, gate, NEG_INF)
    sel = jnp.zeros(gate.shape, jnp.bool_)
    for _ in range(MOBA_TOPK):
        _, first = _first_argmax(g, blk, gate.shape[1])
        pick = blk == first
        sel = jnp.logical_or(sel, jnp.logical_and(pick, eligible))
        g = jnp.where(pick, NEG_INF, g)
    return sel


def _top_block_rows(gate, eligible, blk):
    n = gate.shape[0]
    g = jnp.where(eligible, gate, NEG_INF)
    sel = jnp.zeros(gate.shape, jnp.bool_)
    for _ in range(MOBA_TOPK):
        mx = jnp.max(g, axis=0, keepdims=True)
        first = jnp.min(jnp.where(g == mx, blk, n), axis=0, keepdims=True)
        pick = blk == first
        sel = jnp.logical_or(sel, jnp.logical_and(pick, eligible))
        g = jnp.where(pick, NEG_INF, g)
    return sel


def _moba_prompt_kernel(q_ref, k_ref, vt_ref, km_ref, o_ref, sel_ref, acc_ref, *, tq):
    first_tile = pl.program_id(2) * MOBA_TILES_PER_STEP
    tiles = range(MOBA_TILES_PER_STEP)
    chains = [(t, h) for t in tiles for h in range(HEADS_PER_STEP)]
    masks = _head_masks(QUAD)
    q = [q_ref[t * tq:(t + 1) * tq, :] for t in tiles]
    qs = {(t, h): jnp.where(masks[h], q[t], jnp.zeros_like(q[t])) for t, h in chains}
    nb = km_ref.shape[0]
    km_hi, km_lo = _split_bf16(km_ref[...])
    blk = lax.broadcasted_iota(I32, (nb, 1), 0)
    for n, (t, h) in enumerate(chains):
        gate = _dot_nt(km_hi, qs[t, h]) + _dot_nt(km_lo, qs[t, h])
        sel_ref[n] = _top_block_rows(gate, blk < first_tile + t, blk).astype(F32)
    key = lax.broadcasted_iota(I32, (tq, tq), 0)
    qry = lax.broadcasted_iota(I32, (tq, tq), 1)

    def load(kb):
        start = pl.multiple_of(kb * tq, tq)
        return k_ref[pl.ds(start, tq), :], vt_ref[0, :, pl.ds(start, tq)]

    def head_rows(h):
        return slice(h * HEAD_DIM, (h + 1) * HEAD_DIM)

    def tile_cols(t):
        return slice(t * tq, (t + 1) * tq)

    def weighted_values(vtblk, h, p16):
        ones = jnp.ones((BF16_ROWS, vtblk.shape[1]), BF16)
        out = _dot(jnp.concatenate([vtblk[head_rows(h), :], ones], axis=0), p16)
        return out[:HEAD_DIM], out[HEAD_DIM:HEAD_DIM + 1]

    own = [load(first_tile + t) for t in tiles]
    scores = {c: jnp.where(key <= qry, _dot_nt(own[c[0]][0], qs[c]), NEG_INF) for c in chains}
    m_run = {c: jnp.max(scores[c], axis=0, keepdims=True) for c in chains}
    probs = {c: jnp.exp2(scores[c] - m_run[c]).astype(BF16) for c in chains}
    l_run = {}
    for t, h in chains:
        pv, psum = weighted_values(own[t][1], h, probs[t, h])
        acc_ref[head_rows(h), tile_cols(t)] = pv
        l_run[t, h] = psum

    span = MOBA_BLOCKS_PER_ITER
    n_chains = len(chains)

    def body(it, carry):
        m_run = dict(zip(chains, carry[:n_chains]))
        l_run = dict(zip(chains, carry[n_chains:]))
        start = pl.multiple_of(it * (span * tq), span * tq)
        kblk = k_ref[pl.ds(start, span * tq), :]
        vtblk = vt_ref[0, :, pl.ds(start, span * tq)]
        new_m, new_l, scores, alphas, probs = {}, {}, {}, {}, {}

        def score(c):
            scores[c] = _dot_nt(kblk, qs[c])

        def softmax(c):
            n = chains.index(c)
            parts = [scores[c][j * tq:(j + 1) * tq] for j in range(span)]
            chosen = [sel_ref[n, pl.ds(span * it + j, 1), :] > 0.0 for j in range(span)]
            m = m_run[c]
            for s, ch in zip(parts, chosen):
                m = jnp.where(ch, jnp.maximum(m, jnp.max(s, axis=0, keepdims=True)), m)
            new_m[c] = m
            alphas[c] = jnp.exp2(m_run[c] - m)
            probs[c] = jnp.concatenate(
                [jnp.exp2(s - jnp.where(ch, m, jnp.inf)).astype(BF16) for s, ch in zip(parts, chosen)], axis=0)

        def values(c):
            t, h = c
            pv, psum = weighted_values(vtblk, h, probs[c])
            acc_ref[head_rows(h), tile_cols(t)] = alphas[c] * acc_ref[head_rows(h), tile_cols(t)] + pv
            new_l[c] = alphas[c] * l_run[c] + psum

        for stage in (score, softmax, values):
            for c in chains:
                stage(c)
        return tuple(new_m[c] for c in chains) + tuple(new_l[c] for c in chains)

    n_iter = (first_tile + MOBA_TILES_PER_STEP - 1 + span - 1) // span
    out = lax.fori_loop(0, n_iter, body, tuple(m_run[c] for c in chains) + tuple(l_run[c] for c in chains))
    for n, (t, h) in enumerate(chains):
        acc_ref[head_rows(h), tile_cols(t)] = acc_ref[head_rows(h), tile_cols(t)] * (1.0 / out[n_chains + n])
    o_ref[...] = acc_ref[...].T.astype(o_ref.dtype)


def _moba_prompt(q16, k16, vt16, kmean, batch, seq):
    tq = MOBA_BLOCK
    rows = tq * MOBA_TILES_PER_STEP
    nb = seq // tq
    nq = seq // rows
    ng = W_MIX // QUAD
    return pl.pallas_call(
        functools.partial(_moba_prompt_kernel, tq=tq),
        grid=(batch, ng, nq),
        in_specs=[pl.BlockSpec((rows, QUAD), lambda b, g, i: (b * nq + i, g)),
                  pl.BlockSpec((seq, QUAD), lambda b, g, i: (b, g)),
                  pl.BlockSpec((1, QUAD, seq), lambda b, g, i: (b, g, 0)),
                  pl.BlockSpec((nb, QUAD), lambda b, g, i: (b, g))],
        out_specs=pl.BlockSpec((rows, QUAD), lambda b, g, i: (b * nq + i, g)),
        out_shape=jax.ShapeDtypeStruct((batch * seq, W_MIX), BF16),
        scratch_shapes=[pltpu.VMEM((MOBA_TILES_PER_STEP * HEADS_PER_STEP, nb, tq), F32),
                        pltpu.VMEM((QUAD, rows), F32)],
        compiler_params=_cparams(("parallel", "parallel", "arbitrary"), 48),
        name="moba_prompt",
    )(q16, k16, vt16, kmean)


def _mem_kv_kernel(mem_ref, g_ref, w_ref, k_ref, v_ref):
    kv = _dot(_rms(mem_ref[...], g_ref[...]).astype(BF16), w_ref[...])
    k_ref[...] = kv[:, :W_MEM]
    v_ref[...] = kv[:, W_MEM:]


def _mem_kv(mem, g, w_kv16):
    n = mem.shape[0]
    return pl.pallas_call(
        _mem_kv_kernel,
        out_shape=[jax.ShapeDtypeStruct((n, W_MEM), F32)] * 2,
        name="mem_kv",
    )(mem, g, w_kv16)


def _dot3(x, w_hi, w_lo):
    x_hi, x_lo = _split_bf16(x)
    return _dot(x_hi, w_hi) + (_dot(x_hi, w_lo) + _dot(x_lo, w_hi))


def _post_kernel(x_ref, osb_ref, omb_ref, gate_ref, wsb_ref, wmb_ref, wo_ref,
                 gmem_ref, wq_ref, mk_ref, mv_ref, wom_ref, gffn_ref,
                 wrg_hi_ref, wrg_lo_ref, brg_ref, wre_hi_ref, wre_lo_ref, bre_ref,
                 x2_ref, xn_ref, comb_ref, count_ref, *, rows_per_mem, keys_per_mem):
    gate = gate_ref[...]
    h = gate[:, :D_MODEL] * _dot(osb_ref[...], wsb_ref[...]) + gate[:, D_MODEL:] * _dot(omb_ref[...], wmb_ref[...])
    x1 = x_ref[...] + _dot(h.astype(BF16), wo_ref[...])

    q = (_dot(_rms(x1, gmem_ref[...]).astype(BF16), wq_ref[...]) * (HD_MEM ** -0.5)).astype(BF16)
    heads = []
    if rows_per_mem is not None:
        shape = (x1.shape[0], mk_ref.shape[1])
        same_mem = (lax.broadcasted_iota(I32, shape, 0) // rows_per_mem
                    == lax.broadcasted_iota(I32, shape, 1) // keys_per_mem)
    for hh in range(H_MEM):
        sl = slice(hh * HD_MEM, (hh + 1) * HD_MEM)
        s = _dot_nt(q[:, sl], mk_ref[0, :, sl])
        if rows_per_mem is not None:
            s = jnp.where(same_mem, s, NEG_INF)
        p = jnp.exp(s - jnp.max(s, axis=1, keepdims=True))
        p = p / jnp.sum(p, axis=1, keepdims=True)
        heads.append(_dot(p.astype(BF16), mv_ref[0, :, sl]))
    o = jnp.concatenate(heads, axis=1)
    x2 = x1 + _dot(o.astype(BF16), wom_ref[...])
    x2_ref[...] = x2

    xn = _rms(x2, gffn_ref[...])
    xn16 = xn.astype(BF16)

    lane = lax.broadcasted_iota(I32, (1, LANES), 1)
    gl = jnp.where(lane < N_GROUPS, _dot3(xn, wrg_hi_ref[...], wrg_lo_ref[...]) + brg_ref[...], NEG_INF)
    g_max, g_idx = _first_argmax(gl, lane, LANES)
    g_w = 1.0 / jnp.sum(jnp.exp(gl - g_max), axis=1, keepdims=True)
    el = _dot3(xn, wre_hi_ref[...], wre_lo_ref[...]) + bre_ref[...]
    el = jnp.where((lane // EXPERTS_PER_GROUP) == g_idx, el, NEG_INF)
    e_max, i1 = _first_argmax(el, lane, LANES)
    e_sum = jnp.sum(jnp.exp(el - e_max), axis=1, keepdims=True)
    el2 = jnp.where(lane == i1, NEG_INF, el)
    e_max2, i2 = _first_argmax(el2, lane, LANES)
    w1 = 1.0 / e_sum
    w2 = jnp.exp(e_max2 - e_max) / e_sum
    norm = w1 + w2
    comb = jnp.where(lane == i1, g_w * (w1 / norm), 0.0) + jnp.where(lane == i2, g_w * (w2 / norm), 0.0)
    xn_ref[...] = xn16
    comb_ref[...] = jnp.where(lane == GROUP_ID_LANE, g_idx.astype(F32), comb)
    count_ref[0] = jnp.sum((lane == g_idx).astype(F32), axis=0, keepdims=True)


GROUP_ID_LANE = N_EXPERTS
GROUP_RUN_ALIGN = 16
SORT_ROWS_PAD = 128
EXPERT_TILE = 512
EXPERTS_PER_STEP = 4


def _run_starts(dst_ref, k):
    starts = [jnp.int32(0)]
    for g in range(N_GROUPS):
        starts.append(starts[-1] + (dst_ref[k + 1, g] - dst_ref[k, g]))
    return starts


def _scatter_kernel(dst_ref, xn_ref, comb_ref, xs_zero, cs_zero, pos_ref, xs_hbm, cs_hbm, local_x, local_c, sem):
    del xs_zero, cs_zero
    k = pl.program_id(0)
    last = pl.num_programs(0) - 1
    tm = xn_ref.shape[0]
    sorted_rows = tm + SORT_ROWS_PAD
    lane = lax.broadcasted_iota(I32, (1, LANES), 1)
    routed = comb_ref[...]
    g_idx = routed[:, GROUP_ID_LANE:GROUP_ID_LANE + 1].astype(I32)
    comb = jnp.where(lane < N_EXPERTS, routed, 0.0)

    def window_copies(g, src_row, dst_row):
        src = pl.ds(pl.multiple_of(src_row, GROUP_RUN_ALIGN), tm)
        dst = pl.ds(pl.multiple_of(dst_row, GROUP_RUN_ALIGN), tm)
        return (pltpu.make_async_copy(local_x.at[src], xs_hbm.at[dst], sem.at[0, g]),
                pltpu.make_async_copy(local_c.at[src], cs_hbm.at[dst], sem.at[1, g]))

    def wait_windows():
        for g in range(N_GROUPS):
            for c in window_copies(g, 0, 0):
                c.wait()

    @pl.when(k == 0)
    def _():
        local_x[...] = jnp.zeros_like(local_x)
        local_c[...] = jnp.zeros_like(local_c)

    onehot = (lane == g_idx).astype(BF16)
    earlier = (lax.broadcasted_iota(I32, (tm, tm), 0) > lax.broadcasted_iota(I32, (tm, tm), 1)).astype(BF16)
    before = _dot(earlier, onehot)
    rank = jnp.sum(jnp.where(lane == g_idx, before, 0.0), axis=1, keepdims=True).astype(I32)
    run_start = _run_starts(dst_ref, k)
    pos = rank
    for g in range(N_GROUPS):
        pos = pos + jnp.where(g_idx == g, run_start[g], 0)
    pos_ref[...] = jnp.broadcast_to(pos, pos_ref.shape)

    place_t = (lax.broadcasted_iota(I32, (tm, sorted_rows), 1) == pos).astype(F32)
    place = place_t.T.astype(BF16)
    sorted_x = _dot(place, xn_ref[...]).astype(BF16)
    own = jnp.zeros_like(comb)
    for g in range(N_GROUPS):
        shifted = comb if g == 0 else pltpu.roll(comb, LANES - g * EXPERTS_PER_GROUP, 1)
        own = jnp.where(g_idx == g, shifted, own)
    c_hi = own.astype(BF16)
    c_mid, c_lo = _split_bf16(own - c_hi.astype(F32))
    sorted_c = _dot(place, c_hi) + (_dot(place, c_mid) + _dot(place, c_lo))

    @pl.when(k > 0)
    def _():
        wait_windows()

    local_x[0:sorted_rows, :] = sorted_x
    local_c[0:sorted_rows, :] = sorted_c
    for g in range(N_GROUPS):
        for c in window_copies(g, run_start[g], dst_ref[k, g]):
            c.start()

    @pl.when(k == last)
    def _():
        wait_windows()


def _group_layout(counts, n, nt):
    rows = (counts + (GROUP_RUN_ALIGN - 1)) // GROUP_RUN_ALIGN * GROUP_RUN_ALIGN
    rel = jnp.concatenate([jnp.zeros((1, N_GROUPS), I32), jnp.cumsum(rows, axis=0)], axis=0)
    tiles = (rel[nt] + EXPERT_TILE - 1) // EXPERT_TILE + 1
    ends = jnp.cumsum(tiles)
    dst = (rel + ((ends - tiles) * EXPERT_TILE)[None, :]).astype(I32)
    n_steps = -(-(n + nt * N_GROUPS * (GROUP_RUN_ALIGN - 1)) // EXPERT_TILE) + 2 * N_GROUPS
    step = jnp.arange(n_steps, dtype=I32)
    group = jnp.sum(step[:, None] >= ends[None, :], axis=1).astype(I32)
    live = (group < N_GROUPS).astype(I32)
    return dst, n_steps * EXPERT_TILE, jnp.minimum(group, N_GROUPS - 1), live


def _scatter(dst, xn16, comb, total_rows):
    n = xn16.shape[0]
    tm = EXPERT_TILE
    row = lambda i, dst: (i, 0)
    any_spec = pl.BlockSpec(memory_space=pl.ANY)
    local_rows = tm + SORT_ROWS_PAD + tm
    grid_spec = pltpu.PrefetchScalarGridSpec(
        num_scalar_prefetch=1, grid=(n // tm,),
        in_specs=[pl.BlockSpec((tm, D_MODEL), row), pl.BlockSpec((tm, LANES), row), any_spec, any_spec],
        out_specs=[pl.BlockSpec((tm, LANES), row), any_spec, any_spec],
        scratch_shapes=[pltpu.VMEM((local_rows, D_MODEL), BF16), pltpu.VMEM((local_rows, LANES), F32),
                        pltpu.SemaphoreType.DMA((2, N_GROUPS))])
    return pl.pallas_call(
        _scatter_kernel,
        grid_spec=grid_spec,
        out_shape=[jax.ShapeDtypeStruct((n, LANES), I32), jax.ShapeDtypeStruct((total_rows, D_MODEL), BF16),
                   jax.ShapeDtypeStruct((total_rows, LANES), F32)],
        input_output_aliases={3: 1, 4: 2},
        compiler_params=_cparams(("arbitrary",), 32),
        name="scatter",
    )(dst, xn16, comb, jnp.zeros((total_rows, D_MODEL), BF16), jnp.zeros((total_rows, LANES), F32))


def _post(x, osb, omb, gate, mem_k16, mem_v16, rows_per_batch, tm, w, rows_per_mem=None, keys_per_mem=None):
    n = x.shape[0]
    nt = n // tm
    tiles_per_batch = rows_per_batch // tm
    row = lambda i: (i, 0)
    const = lambda i: (0, 0)
    mem = lambda i: (i // tiles_per_batch, 0, 0)
    n_mem = mem_k16.shape[1]
    full = lambda a: pl.BlockSpec(a.shape, const)
    out_specs = [pl.BlockSpec((tm, D_MODEL), row), pl.BlockSpec((tm, D_MODEL), row),
                 pl.BlockSpec((tm, LANES), row), pl.BlockSpec((1, 1, LANES), lambda i: (i, 0, 0))]
    out_shape = [jax.ShapeDtypeStruct((n, D_MODEL), F32), jax.ShapeDtypeStruct((n, D_MODEL), BF16),
                 jax.ShapeDtypeStruct((n, LANES), F32), jax.ShapeDtypeStruct((nt, 1, LANES), F32)]
    return pl.pallas_call(
        functools.partial(_post_kernel, rows_per_mem=rows_per_mem, keys_per_mem=keys_per_mem),
        grid=(nt,),
        in_specs=[pl.BlockSpec((tm, D_MODEL), row), pl.BlockSpec((tm, W_MIX), row), pl.BlockSpec((tm, W_MIX), row),
                  pl.BlockSpec((tm, 2 * D_MODEL), row),
                  full(w["w_out_sb"]), full(w["w_out_mb"]), full(w["w_out"]),
                  full(w["norm_mem_g"]), full(w["w_q_mem"]),
                  pl.BlockSpec((1, n_mem, W_MEM), mem), pl.BlockSpec((1, n_mem, W_MEM), mem),
                  full(w["w_o_mem"]), full(w["norm_ffn_g"]),
                  full(w["wrg_hi"]), full(w["wrg_lo"]), full(w["brg"]),
                  full(w["wre_hi"]), full(w["wre_lo"]), full(w["bre"])],
        out_specs=out_specs,
        out_shape=out_shape,
        compiler_params=_cparams(("parallel",), 48),
        name="post",
    )(x, osb, omb, gate, w["w_out_sb"], w["w_out_mb"], w["w_out"], w["norm_mem_g"], w["w_q_mem"],
      mem_k16, mem_v16, w["w_o_mem"], w["norm_ffn_g"],
      w["wrg_hi"], w["wrg_lo"], w["brg"], w["wre_hi"], w["wre_lo"], w["bre"])


def _moe_kernel(xn_ref, comb_ref, x2_ref, wg_ref, wu_ref, wd_ref, gfin_ref, y_ref, acc_ref):
    e = pl.program_id(1)

    @pl.when(e == 0)
    def _():
        acc_ref[...] = jnp.zeros_like(acc_ref)

    xn = xn_ref[...]
    hg = _dot(xn, wg_ref[0])
    hu = _dot(xn, wu_ref[0])
    lane = lax.broadcasted_iota(I32, (1, LANES), 1)
    weight = jnp.sum(jnp.where(lane == e, comb_ref[...], 0.0), axis=1, keepdims=True)
    hidden = (hg / (1.0 + jnp.exp(-hg))) * hu * weight
    acc_ref[...] += _dot(hidden.astype(BF16), wd_ref[0])

    @pl.when(e == pl.num_programs(1) - 1)
    def _():
        y_ref[...] = _rms(x2_ref[...] + acc_ref[...], gfin_ref[...])


def _moe(xn16, comb, x2, wg16, wu16, wd16, gfin, tm):
    n = xn16.shape[0]
    nt = n // tm
    row = lambda i, e: (i, 0)
    exp_w = lambda i, e: (e, 0, 0)
    return pl.pallas_call(
        _moe_kernel,
        grid=(nt, N_EXPERTS),
        in_specs=[pl.BlockSpec((tm, D_MODEL), row), pl.BlockSpec((tm, LANES), row), pl.BlockSpec((tm, D_MODEL), row),
                  pl.BlockSpec((1, D_MODEL, D_EXPERT), exp_w), pl.BlockSpec((1, D_MODEL, D_EXPERT), exp_w),
                  pl.BlockSpec((1, D_EXPERT, D_MODEL), exp_w),
                  pl.BlockSpec((1, D_MODEL), lambda i, e: (0, 0))],
        out_specs=pl.BlockSpec((tm, D_MODEL), row),
        out_shape=jax.ShapeDtypeStruct((n, D_MODEL), F32),
        scratch_shapes=[pltpu.VMEM((tm, D_MODEL), F32)],
        compiler_params=_cparams(("parallel", "arbitrary"), 48),
        name="moe",
    )(xn16, comb, x2, wg16, wu16, wd16, gfin)


def _group_experts_kernel(grp_ref, live_ref, xs_ref, cs_ref, wg_ref, wu_ref, wd_ref, ys_ref, acc_ref):
    j, e = pl.program_id(0), pl.program_id(1)

    @pl.when(live_ref[j] > 0)
    def _():
        @pl.when(e == 0)
        def _():
            acc_ref[...] = jnp.zeros_like(acc_ref)

        xs = xs_ref[...]
        cs = cs_ref[...]
        lane = lax.broadcasted_iota(I32, (1, LANES), 1)
        pairs = [(_dot(xs, wg_ref[0, j]), _dot(xs, wu_ref[0, j])) for j in range(EXPERTS_PER_STEP)]
        hidden = []
        for j, (hg, hu) in enumerate(pairs):
            weight = jnp.sum(jnp.where(lane == e * EXPERTS_PER_STEP + j, cs, 0.0), axis=1, keepdims=True)
            hidden.append(((hg / (1.0 + jnp.exp(-hg))) * hu * weight).astype(BF16))
        down = [_dot(hidden[j], wd_ref[0, j]) for j in range(EXPERTS_PER_STEP)]
        acc_ref[...] += functools.reduce(jnp.add, down)

        @pl.when(e == pl.num_programs(1) - 1)
        def _():
            ys_ref[...] = acc_ref[...]

    @pl.when(jnp.logical_and(live_ref[j] == 0, e == 0))
    def _():
        ys_ref[...] = jnp.zeros_like(ys_ref)


def _group_experts(xs, cs, tile_group, tile_live, wg16, wu16, wd16):
    n_steps = tile_group.shape[0]
    by_group = lambda a: a.reshape(N_GROUPS, EXPERTS_PER_GROUP, *a.shape[1:])
    rows = lambda j, e, grp, live: (j, 0)
    inner = EXPERTS_PER_GROUP // EXPERTS_PER_STEP
    expert = lambda j, e, grp, live: (grp[j], e * live[j] + (inner - 1) * (1 - live[j]), 0, 0)
    grid_spec = pltpu.PrefetchScalarGridSpec(
        num_scalar_prefetch=2, grid=(n_steps, inner),
        in_specs=[pl.BlockSpec((EXPERT_TILE, D_MODEL), rows), pl.BlockSpec((EXPERT_TILE, LANES), rows),
                  pl.BlockSpec((1, EXPERTS_PER_STEP, D_MODEL, D_EXPERT), expert),
                  pl.BlockSpec((1, EXPERTS_PER_STEP, D_MODEL, D_EXPERT), expert),
                  pl.BlockSpec((1, EXPERTS_PER_STEP, D_EXPERT, D_MODEL), expert)],
        out_specs=pl.BlockSpec((EXPERT_TILE, D_MODEL), rows),
        scratch_shapes=[pltpu.VMEM((EXPERT_TILE, D_MODEL), F32)])
    return pl.pallas_call(
        _group_experts_kernel,
        grid_spec=grid_spec,
        out_shape=jax.ShapeDtypeStruct((xs.shape[0], D_MODEL), F32),
        compiler_params=_cparams(("arbitrary", "arbitrary"), 48),
        name="group_experts",
    )(tile_group, tile_live, xs, cs, by_group(wg16), by_group(wu16), by_group(wd16))


def _gather_norm_kernel(dst_ref, pos_ref, x2_ref, gfin_ref, ys_hbm, y_ref, stage, local, sem):
    k = pl.program_id(0)
    tm = x2_ref.shape[0]
    sorted_rows = tm + SORT_ROWS_PAD
    slot = k % 2

    def window_copies(step, into):
        return [pltpu.make_async_copy(
            ys_hbm.at[pl.ds(pl.multiple_of(dst_ref[step, g], GROUP_RUN_ALIGN), tm)],
            stage.at[into, g], sem.at[into, g]) for g in range(N_GROUPS)]

    @pl.when(k == 0)
    def _():
        local[...] = jnp.zeros_like(local)
        for c in window_copies(0, 0):
            c.start()

    @pl.when(k + 1 < pl.num_programs(0))
    def _():
        for c in window_copies(k + 1, 1 - slot):
            c.start()

    for c in window_copies(k, slot):
        c.wait()
    run_start = _run_starts(dst_ref, k)
    for g in range(N_GROUPS):
        local[pl.ds(pl.multiple_of(run_start[g], GROUP_RUN_ALIGN), tm), :] = stage[slot, g]
    local[pl.ds(pl.multiple_of(run_start[N_GROUPS], GROUP_RUN_ALIGN), tm), :] = jnp.zeros((tm, local.shape[1]), F32)

    pick = (lax.broadcasted_iota(I32, (tm, sorted_rows), 1) == pos_ref[:, 0:1]).astype(BF16)
    hi, lo = _split_bf16(local[0:sorted_rows, :])
    y_ref[...] = _rms(x2_ref[...] + (_dot(pick, hi) + _dot(pick, lo)), gfin_ref[...])


def _gather_norm(dst, pos, x2, gfin, ys):
    n = x2.shape[0]
    tm = EXPERT_TILE
    nt = n // tm
    row = lambda i, offs: (i, 0)
    grid_spec = pltpu.PrefetchScalarGridSpec(
        num_scalar_prefetch=1, grid=(nt,),
        in_specs=[pl.BlockSpec((tm, LANES), row), pl.BlockSpec((tm, D_MODEL), row),
                  pl.BlockSpec((1, D_MODEL), lambda i, offs: (0, 0)), pl.BlockSpec(memory_space=pl.ANY)],
        out_specs=pl.BlockSpec((tm, D_MODEL), row),
        scratch_shapes=[pltpu.VMEM((2, N_GROUPS, tm, D_MODEL), F32),
                        pltpu.VMEM((tm + SORT_ROWS_PAD + tm, D_MODEL), F32),
                        pltpu.SemaphoreType.DMA((2, N_GROUPS))])
    return pl.pallas_call(
        _gather_norm_kernel,
        grid_spec=grid_spec,
        out_shape=jax.ShapeDtypeStruct((n, D_MODEL), F32),
        compiler_params=_cparams(("arbitrary",), 56),
        name="gather_norm",
    )(dst, pos, x2, gfin, ys)


NEW_ROWS = 8


def _new_token_page(ref):
    rows = ref[0]
    return jnp.concatenate([rows, jnp.zeros((PAGE_SIZE - rows.shape[0], rows.shape[1]), rows.dtype)],
                           axis=0).astype(BF16)


def _own_head_block(full, n_new):
    row_head = lax.broadcasted_iota(I32, (full.shape[0], 1), 0) // n_new
    out = jnp.zeros((full.shape[0], HEAD_DIM), F32)
    for h in range(N_HEADS):
        out = jnp.where(row_head == h, full[:, h * HEAD_DIM:(h + 1) * HEAD_DIM], out)
    return out


def _sb_sample_kernel(pt_ref, q_ref, kn_ref, vn_ref, kc_hbm, vc_hbm, o_ref, kbuf, vbuf, sem, *, n_new, n_pages):
    b = pl.program_id(0)
    q = q_ref[0]
    rows = q.shape[0]
    later = _later_keys(PAGE_SIZE)

    def page_copies(p, slot):
        page = pt_ref[b, p]
        return (pltpu.make_async_copy(kc_hbm.at[page], kbuf.at[slot], sem.at[0, slot]),
                pltpu.make_async_copy(vc_hbm.at[page], vbuf.at[slot], sem.at[1, slot]))

    def start(p, slot):
        for c in page_copies(p, slot):
            c.start()

    def wait(p, slot):
        for c in page_copies(p, slot):
            c.wait()

    slot_of = lambda p: (n_pages - 1 - p) % 2
    start(n_pages - 1, 0)

    key_slot = lax.broadcasted_iota(I32, (rows, PAGE_SIZE), 1)
    tok = lax.broadcasted_iota(I32, (rows, PAGE_SIZE), 0) % n_new
    a, run = _stick_breaking_tile(_dot_nt(q, _new_token_page(kn_ref)), later, jnp.zeros((rows, 1), F32),
                                  key_slot < tok)
    acc = _dot(a.astype(BF16), _new_token_page(vn_ref))

    def cond(carry):
        p, alive = carry[0], carry[1]
        return jnp.logical_and(p >= 0, alive > 0)

    def body(carry):
        p, _, run, acc = carry
        slot = slot_of(p)

        @pl.when(p > 0)
        def _():
            start(p - 1, 1 - slot)

        wait(p, slot)
        z = _dot(q, kbuf[slot].reshape(W_MIX, PAGE_SIZE).astype(BF16))
        a, run = _stick_breaking_tile(z, later, run, None)
        acc = acc + _dot_nt(a.astype(BF16), vbuf[slot].reshape(W_MIX, PAGE_SIZE).astype(BF16))
        alive = (jnp.max(run) > SB_DEAD).astype(I32)
        return (p - 1, alive, run, acc)

    p_next, _, _, acc = lax.while_loop(cond, body, (jnp.int32(n_pages - 1), jnp.int32(1), run, acc))

    @pl.when(p_next >= 0)
    def _():
        wait(p_next, slot_of(p_next))

    o_ref[0] = _own_head_block(acc, n_new)


def _sb_sample(page_table, q_bd, kn_t, vn_t, cache_kt, cache_vt, n_new):
    nb, n_pages = page_table.shape
    rows = N_HEADS * n_new
    per_b = lambda b, pt: (b, 0, 0)
    page_buffers = pltpu.VMEM((2, N_HEADS, HEAD_DIM, PAGE_SIZE), F32)
    grid_spec = pltpu.PrefetchScalarGridSpec(
        num_scalar_prefetch=1, grid=(nb,),
        in_specs=[pl.BlockSpec((1, rows, W_MIX), per_b),
                  pl.BlockSpec((1, NEW_ROWS, W_MIX), per_b),
                  pl.BlockSpec((1, NEW_ROWS, W_MIX), per_b),
                  pl.BlockSpec(memory_space=pl.ANY),
                  pl.BlockSpec(memory_space=pl.ANY)],
        out_specs=pl.BlockSpec((1, rows, HEAD_DIM), per_b),
        scratch_shapes=[page_buffers, page_buffers, pltpu.SemaphoreType.DMA((2, 2))])
    return pl.pallas_call(
        functools.partial(_sb_sample_kernel, n_new=n_new, n_pages=n_pages),
        grid_spec=grid_spec,
        out_shape=jax.ShapeDtypeStruct((nb, rows, HEAD_DIM), F32),
        compiler_params=_cparams(("arbitrary",), 32),
        name="sb_sample",
    )(page_table, q_bd, kn_t, vn_t, cache_kt, cache_vt)


PAGES_PER_BLOCK = MOBA_BLOCK // PAGE_SIZE
MOBA_SAMPLE_BLOCKS_PER_STEP = 8


def _moba_sample_kernel(pt_ref, q_ref, kn_ref, vn_ref, *refs, n_new):
    pages_per_step = MOBA_SAMPLE_BLOCKS_PER_STEP * PAGES_PER_BLOCK
    k_refs, v_refs = refs[:pages_per_step], refs[pages_per_step:2 * pages_per_step]
    o_ref, pm_ref, pl_ref, pg_ref, po_ref = refs[2 * pages_per_step:]
    step = pl.program_id(1)
    n_blocks = pl.num_programs(1) * MOBA_SAMPLE_BLOCKS_PER_STEP
    q = q_ref[0]
    rows = q.shape[0]
    block_lane = lax.broadcasted_iota(I32, (1, LANES), 1)

    @pl.when(step == 0)
    def _():
        pm_ref[...] = jnp.full(pm_ref.shape, NEG_INF, F32)
        pl_ref[...] = jnp.zeros_like(pl_ref)
        pg_ref[...] = jnp.zeros_like(pg_ref)

    def block_pages(page_refs, j):
        pages = [r[0].reshape(W_MIX, PAGE_SIZE).astype(BF16)
                 for r in page_refs[j * PAGES_PER_BLOCK:(j + 1) * PAGES_PER_BLOCK]]
        return jnp.concatenate(pages, axis=1)

    pm, pl_, pg = pm_ref[...], pl_ref[...], pg_ref[...]
    blocks = range(MOBA_SAMPLE_BLOCKS_PER_STEP)
    scores = [_dot(q, block_pages(k_refs, j)) for j in blocks]
    maxes = [jnp.max(z, axis=1, keepdims=True) for z in scores]
    weights = [jnp.exp2(z - m) for z, m in zip(scores, maxes)]
    for j in blocks:
        blk = step * MOBA_SAMPLE_BLOCKS_PER_STEP + j
        here = block_lane == blk
        pm = jnp.where(here, maxes[j], pm)
        pl_ = jnp.where(here, jnp.sum(weights[j], axis=1, keepdims=True), pl_)
        pg = jnp.where(here, jnp.sum(scores[j], axis=1, keepdims=True), pg)
        po_ref[blk] = _dot_nt(weights[j].astype(BF16), block_pages(v_refs, j))
    pm_ref[...], pl_ref[...], pg_ref[...] = pm, pl_, pg

    @pl.when(step == pl.num_programs(1) - 1)
    def _():
        chosen = _top_blocks(pg, block_lane < n_blocks, block_lane)
        slot = lax.broadcasted_iota(I32, (rows, PAGE_SIZE), 1)
        tok = lax.broadcasted_iota(I32, (rows, PAGE_SIZE), 0) % n_new
        zn = jnp.where(slot <= tok, _dot_nt(q, _new_token_page(kn_ref)), NEG_INF)
        m_all = jnp.maximum(jnp.max(jnp.where(chosen, pm, NEG_INF), axis=1, keepdims=True),
                            jnp.max(zn, axis=1, keepdims=True))
        pn = jnp.exp2(zn - m_all)
        w = jnp.where(chosen, jnp.exp2(pm - m_all), 0.0)
        total = jnp.sum(w * pl_, axis=1, keepdims=True) + jnp.sum(pn, axis=1, keepdims=True)
        acc = _dot(pn.astype(BF16), _new_token_page(vn_ref))
        for n in range(po_ref.shape[0]):
            acc = acc + w[:, n:n + 1] * po_ref[n]
        o_ref[0] = _own_head_block(acc, n_new) / total


def _moba_sample(page_table, q_bd, kn_t, vn_t, cache_kt, cache_vt, n_new):
    nb, n_pages = page_table.shape
    rows = N_HEADS * n_new
    pages_per_step = MOBA_SAMPLE_BLOCKS_PER_STEP * PAGES_PER_BLOCK
    n_steps = n_pages // pages_per_step
    per_b = lambda b, s, pt: (b, 0, 0)
    page_spec = lambda j: pl.BlockSpec((1, N_HEADS, HEAD_DIM, PAGE_SIZE),
                                       lambda b, s, pt: (pt[b, s * pages_per_step + j], 0, 0, 0))
    page_specs = [page_spec(j) for j in range(pages_per_step)]
    grid_spec = pltpu.PrefetchScalarGridSpec(
        num_scalar_prefetch=1, grid=(nb, n_steps),
        in_specs=[pl.BlockSpec((1, rows, W_MIX), per_b),
                  pl.BlockSpec((1, NEW_ROWS, W_MIX), per_b),
                  pl.BlockSpec((1, NEW_ROWS, W_MIX), per_b)] + page_specs + page_specs,
        out_specs=pl.BlockSpec((1, rows, HEAD_DIM), per_b),
        scratch_shapes=[pltpu.VMEM((rows, LANES), F32), pltpu.VMEM((rows, LANES), F32),
                        pltpu.VMEM((rows, LANES), F32),
                        pltpu.VMEM((n_pages // PAGES_PER_BLOCK, rows, W_MIX), F32)])
    return pl.pallas_call(
        functools.partial(_moba_sample_kernel, n_new=n_new),
        grid_spec=grid_spec,
        out_shape=jax.ShapeDtypeStruct((nb, rows, HEAD_DIM), F32),
        compiler_params=_cparams(("parallel", "arbitrary"), 32),
        name="moba_sample",
    )(page_table, q_bd, kn_t, vn_t, *([cache_kt] * pages_per_step), *([cache_vt] * pages_per_step))


def _rope_tables(pos):
    half = HEAD_DIM // 2
    inv_freq = ROPE_THETA ** (-jnp.arange(half, dtype=F32) / half)
    ang = pos.astype(F32)[:, None] * inv_freq[None, :]
    cos = jnp.cos(ang)
    sin = jnp.sin(ang)
    heads_per_tile = LANES // HEAD_DIM
    return (jnp.tile(jnp.concatenate([cos, cos], axis=1), (1, heads_per_tile)),
            jnp.tile(jnp.concatenate([-sin, sin], axis=1), (1, heads_per_tile)))


def _pad_lanes(a):
    return jnp.pad(a, ((0, 0), (0, LANES - a.shape[1])))


def _prepare_weights(norm_mix_g, w_in, b_gate, w_out_sb, w_out_mb, w_out, norm_mem_g, norm_memsrc_g, w_q_mem,
                     w_kv_mem, w_o_mem, norm_ffn_g, w_router_group, b_router_group, w_router_expert,
                     b_router_expert, w_gate_e, w_up_e, w_down_e, norm_final_g):
    row = lambda v: v.reshape(1, -1).astype(F32)
    wrg_hi, wrg_lo = _split_bf16(_pad_lanes(w_router_group))
    w_re = w_router_expert.transpose(1, 0, 2).reshape(D_MODEL, N_EXPERTS)
    wre_hi, wre_lo = _split_bf16(_pad_lanes(w_re))
    return dict(
        norm_mix_g=row(norm_mix_g), w_in=w_in.astype(BF16), b_gate=row(b_gate),
        w_out_sb=w_out_sb.astype(BF16), w_out_mb=w_out_mb.astype(BF16), w_out=w_out.astype(BF16),
        norm_mem_g=row(norm_mem_g), norm_memsrc_g=row(norm_memsrc_g), w_q_mem=w_q_mem.astype(BF16),
        w_kv_mem=w_kv_mem.astype(BF16), w_o_mem=w_o_mem.astype(BF16), norm_ffn_g=row(norm_ffn_g),
        wrg_hi=wrg_hi, wrg_lo=wrg_lo, brg=_pad_lanes(row(b_router_group)),
        wre_hi=wre_hi, wre_lo=wre_lo, bre=_pad_lanes(row(b_router_expert)),
        w_gate_e=w_gate_e.astype(BF16), w_up_e=w_up_e.astype(BF16), w_down_e=w_down_e.astype(BF16),
        norm_final_g=row(norm_final_g))


def _tail(x, osb, omb, gate, mem_k16, mem_v16, rows_per_batch, w, tm_post, tm_moe, **mem_mask):
    x2, xn16, comb, _ = _post(x, osb, omb, gate, mem_k16, mem_v16, rows_per_batch, tm_post, w, **mem_mask)
    return _moe(xn16, comb, x2, w["w_gate_e"], w["w_up_e"], w["w_down_e"], w["norm_final_g"], tm_moe)


def kernel(x_prompt, x_sample, mem_prompt, cache_sb_k, cache_sb_v, cache_mb_k, cache_mb_v, cache_mem_k, cache_mem_v, page_table, norm_mix_g, w_in, b_gate, w_out_sb, w_out_mb, w_out, norm_mem_g, norm_memsrc_g, w_q_mem, w_kv_mem, w_o_mem, norm_ffn_g, w_router_group, b_router_group, w_router_expert, b_router_expert, w_gate_e, w_up_e, w_down_e, norm_final_g):
    w = _prepare_weights(norm_mix_g, w_in, b_gate, w_out_sb, w_out_mb, w_out, norm_mem_g, norm_memsrc_g, w_q_mem,
                         w_kv_mem, w_o_mem, norm_ffn_g, w_router_group, b_router_group, w_router_expert,
                         b_router_expert, w_gate_e, w_up_e, w_down_e, norm_final_g)
    batch, seq, _ = x_prompt.shape
    dec_batch, n_new, _ = x_sample.shape
    n_mem = mem_prompt.shape[1]
    n_pages = page_table.shape[1]
    past_len = n_pages * PAGE_SIZE
    assert seq % (MOBA_BLOCK * MOBA_BLOCKS_PER_ITER) == 0 and seq % (MOBA_BLOCK * MOBA_TILES_PER_STEP) == 0
    assert seq % (SB_TILE * SB_TILES_PER_STEP) == 0 and (batch * seq) % EXPERT_TILE == 0
    assert n_new <= PAGE_SIZE and n_pages // PAGES_PER_BLOCK <= LANES
    assert n_pages % (MOBA_SAMPLE_BLOCKS_PER_STEP * PAGES_PER_BLOCK) == 0
    heads = lambda t, b, s: t.reshape(b, s, N_HEADS, HEAD_DIM)

    xp = x_prompt.reshape(batch * seq, D_MODEL)
    cos_p, sin_p = _rope_tables(jnp.arange(seq, dtype=I32))
    (qsb, qmb, gate, ksb_t, vsb_t, kmb_t, vmb_t, ksb16, vsb16_t, kmb16, vmb16_t, kmean) = _proj(
        xp, w["norm_mix_g"], w["w_in"], w["b_gate"], cos_p, sin_p, MOBA_BLOCK, rows_per_batch=seq)
    osb = _sb_prompt(qsb, ksb16, vsb16_t, batch, seq, SB_TILE)
    omb = _moba_prompt(qmb, kmb16, vmb16_t, kmean.reshape(batch * seq // MOBA_BLOCK, W_MIX), batch, seq)
    mem_k, mem_v = _mem_kv(mem_prompt.reshape(batch * n_mem, D_MODEL), w["norm_memsrc_g"], w["w_kv_mem"])
    x2, xn16, comb, counts = _post(xp, osb, omb, gate, mem_k.astype(BF16).reshape(batch, n_mem, W_MEM),
                                   mem_v.astype(BF16).reshape(batch, n_mem, W_MEM), seq, EXPERT_TILE, w)
    n_tiles = batch * seq // EXPERT_TILE
    dst, total_rows, tile_group, tile_live = _group_layout(
        counts[:, 0, :N_GROUPS].astype(I32), batch * seq, n_tiles)
    pos, rows_by_group, weights_by_group = _scatter(dst, xn16, comb, total_rows)
    expert_out = _group_experts(rows_by_group, weights_by_group, tile_group, tile_live,
                                w["w_gate_e"], w["w_up_e"], w["w_down_e"])
    y_prompt = _gather_norm(dst, pos, x2, w["norm_final_g"], expert_out)

    rows_s = dec_batch * n_new
    xs = x_sample.reshape(rows_s, D_MODEL)
    cos_s, sin_s = _rope_tables(past_len + (jnp.arange(rows_s, dtype=I32) % n_new))
    (qsb_s, qmb_s, gate_s, ksb_s, vsb_s, kmb_s, vmb_s) = _proj(
        xs, w["norm_mix_g"], w["w_in"], w["b_gate"], cos_s, sin_s, rows_s)

    def block_diagonal(q16):
        q_cols = heads(q16, dec_batch, n_new).transpose(0, 2, 1, 3).reshape(dec_batch, N_HEADS * n_new, HEAD_DIM)
        own = (jnp.arange(N_HEADS * n_new)[:, None] // n_new) == (jnp.arange(W_MIX)[None, :] // HEAD_DIM)
        return jnp.where(own[None], jnp.tile(q_cols, (1, 1, N_HEADS)), jnp.zeros((), q16.dtype))

    def new_page(t):
        return jnp.pad(t.reshape(dec_batch, n_new, W_MIX), ((0, 0), (0, NEW_ROWS - n_new), (0, 0)))

    def token_rows(o):
        o = o.reshape(dec_batch, N_HEADS, n_new, HEAD_DIM).transpose(0, 2, 1, 3)
        return o.reshape(rows_s, W_MIX).astype(BF16)

    pages = lambda c: c.transpose(0, 2, 3, 1)
    osb_s = token_rows(_sb_sample(page_table, block_diagonal(qsb_s), new_page(ksb_s), new_page(vsb_s),
                                  pages(cache_sb_k), pages(cache_sb_v), n_new))
    omb_s = token_rows(_moba_sample(page_table, block_diagonal(qmb_s), new_page(kmb_s), new_page(vmb_s),
                                    pages(cache_mb_k), pages(cache_mb_v), n_new))
    y_sample = _tail(xs, osb_s, omb_s, gate_s,
                     cache_mem_k.astype(BF16).reshape(1, dec_batch * n_mem, W_MEM),
                     cache_mem_v.astype(BF16).reshape(1, dec_batch * n_mem, W_MEM),
                     rows_s, w, rows_s, rows_s, rows_per_mem=n_new, keys_per_mem=n_mem)

    mem_heads = lambda t: t.reshape(batch, n_mem, H_MEM, HD_MEM)
    from_head_major = lambda t: t.reshape(batch, N_HEADS, HEAD_DIM, seq).transpose(0, 3, 1, 2)
    return (y_prompt.reshape(batch, seq, D_MODEL), y_sample.reshape(dec_batch, n_new, D_MODEL),
            from_head_major(ksb_t), from_head_major(vsb_t), from_head_major(kmb_t), from_head_major(vmb_t),
            mem_heads(mem_k), mem_heads(mem_v),
            heads(ksb_s, dec_batch, n_new), heads(vsb_s, dec_batch, n_new),
            heads(kmb_s, dec_batch, n_new), heads(vmb_s, dec_batch, n_new))
```

```python
import functools

import jax
import jax.numpy as jnp
from jax import lax
from jax.experimental import pallas as pl
from jax.experimental.pallas import tpu as pltpu

F32 = jnp.float32
BF16 = jnp.bfloat16
I32 = jnp.int32

D_MODEL = 1024
N_HEADS = 8
HEAD_DIM = 64
W_MIX = N_HEADS * HEAD_DIM
PAGE_SIZE = 128
MOBA_BLOCK = 256
MOBA_TOPK = 3
H_MEM = 4
HD_MEM = 128
W_MEM = H_MEM * HD_MEM
N_GROUPS = 4
EXPERTS_PER_GROUP = 8
N_EXPERTS = N_GROUPS * EXPERTS_PER_GROUP
D_EXPERT = 256
ROPE_THETA = 10000.0
RMS_EPS = 1e-6
W_IN_COLS = 6 * W_MIX + 2 * D_MODEL

LOG2_E = 1.4426950408889634
LANES = 128
BF16_ROWS = 16
HEADS_PER_STEP = 4
QUAD = HEADS_PER_STEP * HEAD_DIM
SB_DEAD = -160.0
PROJ_TILE = 512
MOBA_TILES_PER_STEP = 4
MOBA_BLOCKS_PER_ITER = 2
SB_TILE = 128
SB_TILES_PER_STEP = 4
NEG_INF = float("-inf")
MIB = 1024 * 1024


def _cparams(semantics, vmem_mib):
    return pltpu.CompilerParams(dimension_semantics=semantics, vmem_limit_bytes=vmem_mib * MIB)


def _rms(x, g):
    ms = jnp.mean(x * x, axis=-1, keepdims=True)
    return (x * lax.rsqrt(ms + RMS_EPS)) * g


def _dot(a, b):
    return jnp.dot(a, b, preferred_element_type=F32)


def _dot_nt(a, b):
    return lax.dot_general(a, b, (((1,), (1,)), ((), ())), preferred_element_type=F32)


def _split_bf16(x):
    hi = x.astype(BF16)
    lo = (x - hi.astype(F32)).astype(BF16)
    return hi, lo


def _later_keys(n):
    return (lax.broadcasted_iota(I32, (n, n), 0) > lax.broadcasted_iota(I32, (n, n), 1)).astype(BF16)


def _stick_breaking_tile(z, later, carried, valid):
    l1p = jnp.log1p(jnp.exp(-jnp.abs(z)))
    log_keep = -(jnp.maximum(z, 0.0) + l1p)
    if valid is not None:
        log_keep = jnp.where(valid, log_keep, 0.0)
    hi, lo = _split_bf16(log_keep)
    inner = _dot(hi, later) + _dot(lo, later)
    a = jnp.exp(jnp.minimum(z, 0.0) - l1p + carried + inner)
    if valid is not None:
        a = jnp.where(valid, a, 0.0)
    return a, carried + inner[:, 0:1] + log_keep[:, 0:1]


def _proj_kernel(x_ref, g_ref, w_ref, bg_ref, cos_ref, sin_ref, qsb_ref, qmb_ref, gate_ref, *kv_refs, head_major):
    xb = _rms(x_ref[...], g_ref[...]).astype(BF16)

    def seg(lo, width):
        return _dot(xb, w_ref[:, lo:lo + width])

    lane = lax.broadcasted_iota(I32, (1, W_MIX), 1)
    first_half = (lane % HEAD_DIM) < (HEAD_DIM // 2)
    cos = jnp.concatenate([cos_ref[...]] * (W_MIX // LANES), axis=1)
    sin = jnp.concatenate([sin_ref[...]] * (W_MIX // LANES), axis=1)

    def rope(t):
        partner = jnp.where(first_half, pltpu.roll(t, W_MIX - HEAD_DIM // 2, 1),
                            pltpu.roll(t, HEAD_DIM // 2, 1))
        return t * cos + partner * sin

    scale = HEAD_DIM ** -0.5
    qsb_ref[...] = (seg(0, W_MIX) * scale).astype(BF16)
    qmb_ref[...] = (rope(seg(3 * W_MIX, W_MIX)) * (scale * LOG2_E)).astype(BF16)
    gl = seg(6 * W_MIX, 2 * D_MODEL) + bg_ref[...]
    gate_ref[...] = 1.0 / (1.0 + jnp.exp(-gl))
    ksb = seg(W_MIX, W_MIX)
    vsb = seg(2 * W_MIX, W_MIX)
    kmb = rope(seg(4 * W_MIX, W_MIX))
    vmb = seg(5 * W_MIX, W_MIX)
    if not head_major:
        for ref, t in zip(kv_refs, (ksb, vsb, kmb, vmb)):
            ref[...] = t
        return
    ksb_t_ref, vsb_t_ref, kmb_t_ref, vmb_t_ref, ksb16_ref, vsb16_t_ref, kmb16_ref, vmb16_t_ref, kmean_ref = kv_refs
    ksb_t_ref[0] = ksb.T
    vsb_t = vsb.T
    vsb_t_ref[0] = vsb_t
    kmb_t_ref[0] = kmb.T
    vmb_t = vmb.T
    vmb_t_ref[0] = vmb_t
    ksb16_ref[...] = ksb.astype(BF16)
    vsb16_t_ref[0] = vsb_t.astype(BF16)
    kmb16_ref[...] = kmb.astype(BF16)
    vmb16_t_ref[0] = vmb_t.astype(BF16)
    for j in range(kmean_ref.shape[0]):
        kmean_ref[j] = jnp.mean(kmb[j * MOBA_BLOCK:(j + 1) * MOBA_BLOCK], axis=0, keepdims=True)


def _proj(x, g, w_in16, b_gate, cos, sin, tm, rows_per_batch=None):
    n = x.shape[0]
    nt = n // tm
    n_rope = cos.shape[0] // tm
    row = lambda i: (i, 0)
    const = lambda i: (0, 0)
    head_major = rows_per_batch is not None
    out_shape = [jax.ShapeDtypeStruct((n, W_MIX), BF16)] * 2 + [jax.ShapeDtypeStruct((n, 2 * D_MODEL), F32)]
    out_specs = [pl.BlockSpec((tm, W_MIX), row)] * 2 + [pl.BlockSpec((tm, 2 * D_MODEL), row)]
    if head_major:
        tiles = rows_per_batch // tm
        n_batch = n // rows_per_batch
        t_spec = pl.BlockSpec((1, W_MIX, tm), lambda i: (i // tiles, 0, i % tiles))
        rows16 = jax.ShapeDtypeStruct((n, W_MIX), BF16)
        t16 = jax.ShapeDtypeStruct((n_batch, W_MIX, rows_per_batch), BF16)
        out_shape += ([jax.ShapeDtypeStruct((n_batch, W_MIX, rows_per_batch), F32)] * 4
                      + [rows16, t16, rows16, t16, jax.ShapeDtypeStruct((n // MOBA_BLOCK, 1, W_MIX), F32)])
        r_spec = pl.BlockSpec((tm, W_MIX), row)
        out_specs += ([t_spec] * 4 + [r_spec, t_spec, r_spec, t_spec,
                                      pl.BlockSpec((tm // MOBA_BLOCK, 1, W_MIX), lambda i: (i, 0, 0))])
    else:
        out_shape += [jax.ShapeDtypeStruct((n, W_MIX), F32)] * 4
        out_specs += [pl.BlockSpec((tm, W_MIX), row)] * 4
    return pl.pallas_call(
        functools.partial(_proj_kernel, head_major=head_major),
        grid=(nt,),
        in_specs=[pl.BlockSpec((tm, D_MODEL), row),
                  pl.BlockSpec((1, D_MODEL), const),
                  pl.BlockSpec((D_MODEL, W_IN_COLS), const),
                  pl.BlockSpec((1, 2 * D_MODEL), const),
                  pl.BlockSpec((tm, LANES), lambda i: (i % n_rope, 0)),
                  pl.BlockSpec((tm, LANES), lambda i: (i % n_rope, 0))],
        out_specs=out_specs,
        out_shape=out_shape,
        compiler_params=_cparams(("parallel",), 48),
        name="proj",
    )(x, g, w_in16, b_gate, cos, sin)


def _head_masks(width):
    lane_head = lax.broadcasted_iota(I32, (1, width), 1) // HEAD_DIM
    return [lane_head == h for h in range(HEADS_PER_STEP)]


def _sb_prompt_kernel(q_ref, k_ref, vt_ref, o_ref, acc_ref, *, tq):
    i = pl.program_id(2)
    masks = _head_masks(QUAD)
    tiles = range(SB_TILES_PER_STEP)
    chains = [(t, h) for t in tiles for h in range(HEADS_PER_STEP)]
    q = [q_ref[t * tq:(t + 1) * tq, :] for t in tiles]
    qs = {(t, h): jnp.where(masks[h], q[t], jnp.zeros_like(q[t])) for t, h in chains}
    key = lax.broadcasted_iota(I32, (tq, tq), 0)
    qry = lax.broadcasted_iota(I32, (tq, tq), 1)
    after = (qry > key).astype(BF16)

    def block(step, dead, diagonal):
        kblk, vtblk = [], []
        for t in tiles:
            kb = i * SB_TILES_PER_STEP + t - step
            if not diagonal:
                dead = {c: (jnp.where(kb < 0, jnp.inf, d) if c[0] == t else d) for c, d in dead.items()}
                kb = jnp.maximum(kb, 0)
            start = pl.multiple_of(kb * tq, tq)
            kblk.append(k_ref[pl.ds(start, tq), :])
            vtblk.append(vt_ref[0, :, pl.ds(start, tq)])
        scores = {c: _dot_nt(kblk[c[0]], qs[c]) for c in chains}
        softplus, parts = {}, {}
        for c in chains:
            z = scores[c]
            sp = jnp.maximum(z, 0.0) + jnp.log(1.0 + jnp.exp(-jnp.abs(z)))
            if diagonal:
                sp = jnp.where(key < qry, sp, 0.0)
            softplus[c] = sp
            parts[c] = _split_bf16(sp)
        inner = {c: _dot(after, parts[c][0]) + _dot(after, parts[c][1]) for c in chains}
        new_dead = {}
        for c in chains:
            t, h = c
            a = jnp.exp(scores[c] - softplus[c] - (dead[c] + inner[c]))
            if diagonal:
                a = jnp.where(key < qry, a, 0.0)
            rows, cols = slice(h * HEAD_DIM, (h + 1) * HEAD_DIM), slice(t * tq, (t + 1) * tq)
            acc_ref[rows, cols] += _dot(vtblk[t][rows, :], a.astype(BF16))
            new_dead[c] = dead[c] + inner[c][0:1] + softplus[c][0:1]
        return new_dead

    def alive(dead):
        low = functools.reduce(jnp.minimum, [dead[c] for c in chains])
        return (jnp.min(low) < -SB_DEAD).astype(I32)

    acc_ref[...] = jnp.zeros_like(acc_ref)
    dead = block(0, {c: jnp.zeros((1, tq), F32) for c in chains}, True)
    newest = i * SB_TILES_PER_STEP + SB_TILES_PER_STEP - 1

    def cond(carry):
        return jnp.logical_and(carry[0] <= newest, carry[1] > 0)

    def body(carry):
        dead = block(carry[0], dict(zip(chains, carry[2:])), False)
        return (carry[0] + 1, alive(dead)) + tuple(dead[c] for c in chains)

    lax.while_loop(cond, body, (jnp.int32(1), alive(dead)) + tuple(dead[c] for c in chains))
    o_ref[...] = acc_ref[...].T.astype(o_ref.dtype)


def _sb_prompt(q16, k16, vt16, batch, seq, tq):
    rows = tq * SB_TILES_PER_STEP
    nq = seq // rows
    ng = W_MIX // QUAD
    return pl.pallas_call(
        functools.partial(_sb_prompt_kernel, tq=tq),
        grid=(batch, ng, nq),
        in_specs=[pl.BlockSpec((rows, QUAD), lambda b, g, i: (b * nq + i, g)),
                  pl.BlockSpec((seq, QUAD), lambda b, g, i: (b, g)),
                  pl.BlockSpec((1, QUAD, seq), lambda b, g, i: (b, g, 0))],
        out_specs=pl.BlockSpec((rows, QUAD), lambda b, g, i: (b * nq + i, g)),
        out_shape=jax.ShapeDtypeStruct((batch * seq, W_MIX), BF16),
        scratch_shapes=[pltpu.VMEM((QUAD, rows), F32)],
        compiler_params=_cparams(("parallel", "parallel", "arbitrary"), 48),
        name="sb_prompt",
    )(q16, k16, vt16)


def _first_argmax(x, lane, width):
    mx = jnp.max(x, axis=1, keepdims=True)
    idx = jnp.min(jnp.where(x == mx, lane, width), axis=1, keepdims=True)
    return mx, idx


def _top_blocks(gate, eligible, blk):
    g = jnp.where(eligible, gate, NEG_INF)
    sel = jnp.zeros(gate.shape, jnp.bool_)
    for _ in range(MOBA_TOPK):
        _, first = _first_argmax(g, blk, gate.shape[1])
        pick = blk == first
        sel = jnp.logical_or(sel, jnp.logical_and(pick, eligible))
        g = jnp.where(pick, NEG_INF, g)
    return sel


def _top_block_rows(gate, eligible, blk):
    n = gate.shape[0]
    g = jnp.where(eligible, gate, NEG_INF)
    sel = jnp.zeros(gate.shape, jnp.bool_)
    for _ in range(MOBA_TOPK):
        mx = jnp.max(g, axis=0, keepdims=True)
        first = jnp.min(jnp.where(g == mx, blk, n), axis=0, keepdims=True)
        pick = blk == first
        sel = jnp.logical_or(sel, jnp.logical_and(pick, eligible))
        g = jnp.where(pick, NEG_INF, g)
    return sel


def _moba_prompt_kernel(q_ref, k_ref, vt_ref, km_ref, o_ref, sel_ref, acc_ref, *, tq):
    first_tile = pl.program_id(2) * MOBA_TILES_PER_STEP
    tiles = range(MOBA_TILES_PER_STEP)
    chains = [(t, h) for t in tiles for h in range(HEADS_PER_STEP)]
    masks = _head_masks(QUAD)
    q = [q_ref[t * tq:(t + 1) * tq, :] for t in tiles]
    qs = {(t, h): jnp.where(masks[h], q[t], jnp.zeros_like(q[t])) for t, h in chains}
    nb = km_ref.shape[0]
    km_hi, km_lo = _split_bf16(km_ref[...])
    blk = lax.broadcasted_iota(I32, (nb, 1), 0)
    for n, (t, h) in enumerate(chains):
        gate = _dot_nt(km_hi, qs[t, h]) + _dot_nt(km_lo, qs[t, h])
        sel_ref[n] = _top_block_rows(gate, blk < first_tile + t, blk).astype(F32)
    key = lax.broadcasted_iota(I32, (tq, tq), 0)
    qry = lax.broadcasted_iota(I32, (tq, tq), 1)

    def load(kb):
        start = pl.multiple_of(kb * tq, tq)
        return k_ref[pl.ds(start, tq), :], vt_ref[0, :, pl.ds(start, tq)]

    def head_rows(h):
        return slice(h * HEAD_DIM, (h + 1) * HEAD_DIM)

    def tile_cols(t):
        return slice(t * tq, (t + 1) * tq)

    def weighted_values(vtblk, h, p16):
        ones = jnp.ones((BF16_ROWS, vtblk.shape[1]), BF16)
        out = _dot(jnp.concatenate([vtblk[head_rows(h), :], ones], axis=0), p16)
        return out[:HEAD_DIM], out[HEAD_DIM:HEAD_DIM + 1]

    own = [load(first_tile + t) for t in tiles]
    scores = {c: jnp.where(key <= qry, _dot_nt(own[c[0]][0], qs[c]), NEG_INF) for c in chains}
    m_run = {c: jnp.max(scores[c], axis=0, keepdims=True) for c in chains}
    probs = {c: jnp.exp2(scores[c] - m_run[c]).astype(BF16) for c in chains}
    l_run = {}
    for t, h in chains:
        pv, psum = weighted_values(own[t][1], h, probs[t, h])
        acc_ref[head_rows(h), tile_cols(t)] = pv
        l_run[t, h] = psum

    span = MOBA_BLOCKS_PER_ITER
    n_chains = len(chains)

    def body(it, carry):
        m_run = dict(zip(chains, carry[:n_chains]))
        l_run = dict(zip(chains, carry[n_chains:]))
        start = pl.multiple_of(it * (span * tq), span * tq)
        kblk = k_ref[pl.ds(start, span * tq), :]
        vtblk = vt_ref[0, :, pl.ds(start, span * tq)]
        new_m, new_l, scores, alphas, probs = {}, {}, {}, {}, {}

        def score(c):
            scores[c] = _dot_nt(kblk, qs[c])

        def softmax(c):
            n = chains.index(c)
            parts = [scores[c][j * tq:(j + 1) * tq] for j in range(span)]
            chosen = [sel_ref[n, pl.ds(span * it + j, 1), :] > 0.0 for j in range(span)]
            m = m_run[c]
            for s, ch in zip(parts, chosen):
                m = jnp.where(ch, jnp.maximum(m, jnp.max(s, axis=0, keepdims=True)), m)
            new_m[c] = m
            alphas[c] = jnp.exp2(m_run[c] - m)
            probs[c] = jnp.concatenate(
                [jnp.exp2(s - jnp.where(ch, m, jnp.inf)).astype(BF16) for s, ch in zip(parts, chosen)], axis=0)

        def values(c):
            t, h = c
            pv, psum = weighted_values(vtblk, h, probs[c])
            acc_ref[head_rows(h), tile_cols(t)] = alphas[c] * acc_ref[head_rows(h), tile_cols(t)] + pv
            new_l[c] = alphas[c] * l_run[c] + psum

        for stage in (score, softmax, values):
            for c in chains:
                stage(c)
        return tuple(new_m[c] for c in chains) + tuple(new_l[c] for c in chains)

    n_iter = (first_tile + MOBA_TILES_PER_STEP - 1 + span - 1) // span
    out = lax.fori_loop(0, n_iter, body, tuple(m_run[c] for c in chains) + tuple(l_run[c] for c in chains))
    for n, (t, h) in enumerate(chains):
        acc_ref[head_rows(h), tile_cols(t)] = acc_ref[head_rows(h), tile_cols(t)] * (1.0 / out[n_chains + n])
    o_ref[...] = acc_ref[...].T.astype(o_ref.dtype)


def _moba_prompt(q16, k16, vt16, kmean, batch, seq):
    tq = MOBA_BLOCK
    rows = tq * MOBA_TILES_PER_STEP
    nb = seq // tq
    nq = seq // rows
    ng = W_MIX // QUAD
    return pl.pallas_call(
        functools.partial(_moba_prompt_kernel, tq=tq),
        grid=(batch, ng, nq),
        in_specs=[pl.BlockSpec((rows, QUAD), lambda b, g, i: (b * nq + i, g)),
                  pl.BlockSpec((seq, QUAD), lambda b, g, i: (b, g)),
                  pl.BlockSpec((1, QUAD, seq), lambda b, g, i: (b, g, 0)),
                  pl.BlockSpec((nb, QUAD), lambda b, g, i: (b, g))],
        out_specs=pl.BlockSpec((rows, QUAD), lambda b, g, i: (b * nq + i, g)),
        out_shape=jax.ShapeDtypeStruct((batch * seq, W_MIX), BF16),
        scratch_shapes=[pltpu.VMEM((MOBA_TILES_PER_STEP * HEADS_PER_STEP, nb, tq), F32),
                        pltpu.VMEM((QUAD, rows), F32)],
        compiler_params=_cparams(("parallel", "parallel", "arbitrary"), 48),
        name="moba_prompt",
    )(q16, k16, vt16, kmean)


def _mem_kv_kernel(mem_ref, g_ref, w_ref, k_ref, v_ref):
    kv = _dot(_rms(mem_ref[...], g_ref[...]).astype(BF16), w_ref[...])
    k_ref[...] = kv[:, :W_MEM]
    v_ref[...] = kv[:, W_MEM:]


def _mem_kv(mem, g, w_kv16):
    n = mem.shape[0]
    return pl.pallas_call(
        _mem_kv_kernel,
        out_shape=[jax.ShapeDtypeStruct((n, W_MEM), F32)] * 2,
        name="mem_kv",
    )(mem, g, w_kv16)


def _post_kernel(x_ref, osb_ref, omb_ref, gate_ref, wsb_ref, wmb_ref, wo_ref,
                 gmem_ref, wq_ref, mk_ref, mv_ref, wom_ref, gffn_ref,
                 wr_hilo_ref, wr_hi_ref, br_ref,
                 x2_ref, xn_ref, comb_ref, count_ref, *, rows_per_mem, keys_per_mem):
    gate = gate_ref[...]
    h = gate[:, :D_MODEL] * _dot(osb_ref[...], wsb_ref[...]) + gate[:, D_MODEL:] * _dot(omb_ref[...], wmb_ref[...])
    x1 = x_ref[...] + _dot(h.astype(BF16), wo_ref[...])

    q = (_dot(_rms(x1, gmem_ref[...]).astype(BF16), wq_ref[...]) * (HD_MEM ** -0.5)).astype(BF16)
    heads = []
    if rows_per_mem is not None:
        shape = (x1.shape[0], mk_ref.shape[1])
        same_mem = (lax.broadcasted_iota(I32, shape, 0) // rows_per_mem
                    == lax.broadcasted_iota(I32, shape, 1) // keys_per_mem)
    for hh in range(H_MEM):
        sl = slice(hh * HD_MEM, (hh + 1) * HD_MEM)
        s = _dot_nt(q[:, sl], mk_ref[0, :, sl])
        if rows_per_mem is not None:
            s = jnp.where(same_mem, s, NEG_INF)
        p = jnp.exp(s - jnp.max(s, axis=1, keepdims=True))
        p = p / jnp.sum(p, axis=1, keepdims=True)
        heads.append(_dot(p.astype(BF16), mv_ref[0, :, sl]))
    o = jnp.concatenate(heads, axis=1)
    x2 = x1 + _dot(o.astype(BF16), wom_ref[...])
    x2_ref[...] = x2

    xn = _rms(x2, gffn_ref[...])
    xn16 = xn.astype(BF16)

    lane = lax.broadcasted_iota(I32, (1, LANES), 1)
    x_hi, x_lo = _split_bf16(xn)
    both = _dot(x_hi, wr_hilo_ref[...])
    logits = both[:, :LANES] + both[:, LANES:] + _dot(x_lo, wr_hi_ref[...]) + br_ref[...]
    is_group = jnp.logical_and(lane >= N_EXPERTS, lane < N_EXPERTS + N_GROUPS)
    gl = jnp.where(is_group, logits, NEG_INF)
    g_max, g_lane = _first_argmax(gl, lane, LANES)
    g_idx = g_lane - N_EXPERTS
    g_w = 1.0 / jnp.sum(jnp.exp(gl - g_max), axis=1, keepdims=True)
    el = jnp.where((lane // EXPERTS_PER_GROUP) == g_idx, logits, NEG_INF)
    e_max, i1 = _first_argmax(el, lane, LANES)
    e_sum = jnp.sum(jnp.exp(el - e_max), axis=1, keepdims=True)
    el2 = jnp.where(lane == i1, NEG_INF, el)
    e_max2, i2 = _first_argmax(el2, lane, LANES)
    w1 = 1.0 / e_sum
    w2 = jnp.exp(e_max2 - e_max) / e_sum
    norm = w1 + w2
    comb = jnp.where(lane == i1, g_w * (w1 / norm), 0.0) + jnp.where(lane == i2, g_w * (w2 / norm), 0.0)
    xn_ref[...] = xn16
    comb_ref[...] = jnp.where(lane == GROUP_ID_LANE, g_idx.astype(F32), comb)
    count_ref[0] = jnp.sum((lane == g_idx).astype(F32), axis=0, keepdims=True)


GROUP_ID_LANE = N_EXPERTS
GROUP_RUN_ALIGN = 16
SORT_ROWS_PAD = 128
EXPERT_TILE = 512
EXPERTS_PER_STEP = 8


def _run_starts(dst_ref, k):
    starts = [jnp.int32(0)]
    for g in range(N_GROUPS):
        starts.append(starts[-1] + (dst_ref[k + 1, g] - dst_ref[k, g]))
    return starts


def _scatter_kernel(dst_ref, xn_ref, comb_ref, xs_zero, cs_zero, pos_ref, xs_hbm, cs_hbm, local_x, local_c, sem):
    del xs_zero, cs_zero
    k = pl.program_id(0)
    last = pl.num_programs(0) - 1
    tm = xn_ref.shape[0]
    sorted_rows = tm + SORT_ROWS_PAD
    lane = lax.broadcasted_iota(I32, (1, LANES), 1)
    routed = comb_ref[...]
    g_idx = routed[:, GROUP_ID_LANE:GROUP_ID_LANE + 1].astype(I32)
    comb = jnp.where(lane < N_EXPERTS, routed, 0.0)

    def window_copies(g, src_row, dst_row):
        src = pl.ds(pl.multiple_of(src_row, GROUP_RUN_ALIGN), tm)
        dst = pl.ds(pl.multiple_of(dst_row, GROUP_RUN_ALIGN), tm)
        return (pltpu.make_async_copy(local_x.at[src], xs_hbm.at[dst], sem.at[0, g]),
                pltpu.make_async_copy(local_c.at[src], cs_hbm.at[dst], sem.at[1, g]))

    def wait_windows():
        for g in range(N_GROUPS):
            for c in window_copies(g, 0, 0):
                c.wait()

    @pl.when(k == 0)
    def _():
        local_x[...] = jnp.zeros_like(local_x)
        local_c[...] = jnp.zeros_like(local_c)

    onehot = (lane == g_idx).astype(BF16)
    earlier = (lax.broadcasted_iota(I32, (tm, tm), 0) > lax.broadcasted_iota(I32, (tm, tm), 1)).astype(BF16)
    before = _dot(earlier, onehot)
    rank = jnp.sum(jnp.where(lane == g_idx, before, 0.0), axis=1, keepdims=True).astype(I32)
    run_start = _run_starts(dst_ref, k)
    pos = rank
    for g in range(N_GROUPS):
        pos = pos + jnp.where(g_idx == g, run_start[g], 0)
    pos_ref[...] = jnp.broadcast_to(pos, pos_ref.shape)

    place_t = (lax.broadcasted_iota(I32, (tm, sorted_rows), 1) == pos).astype(F32)
    place = place_t.T.astype(BF16)
    sorted_x = _dot(place, xn_ref[...]).astype(BF16)
    own = jnp.zeros_like(comb)
    for g in range(N_GROUPS):
        shifted = comb if g == 0 else pltpu.roll(comb, LANES - g * EXPERTS_PER_GROUP, 1)
        own = jnp.where(g_idx == g, shifted, own)
    c_hi = own.astype(BF16)
    c_mid, c_lo = _split_bf16(own - c_hi.astype(F32))
    sorted_c = _dot(place, c_hi) + (_dot(place, c_mid) + _dot(place, c_lo))

    @pl.when(k > 0)
    def _():
        wait_windows()

    local_x[0:sorted_rows, :] = sorted_x
    local_c[0:sorted_rows, :] = sorted_c
    for g in range(N_GROUPS):
        for c in window_copies(g, run_start[g], dst_ref[k, g]):
            c.start()

    @pl.when(k == last)
    def _():
        wait_windows()


def _group_layout(counts, n, nt):
    rows = (counts + (GROUP_RUN_ALIGN - 1)) // GROUP_RUN_ALIGN * GROUP_RUN_ALIGN
    rel = jnp.concatenate([jnp.zeros((1, N_GROUPS), I32), jnp.cumsum(rows, axis=0)], axis=0)
    tiles = (rel[nt] + EXPERT_TILE - 1) // EXPERT_TILE + 1
    ends = jnp.cumsum(tiles)
    dst = (rel + ((ends - tiles) * EXPERT_TILE)[None, :]).astype(I32)
    n_steps = -(-(n + nt * N_GROUPS * (GROUP_RUN_ALIGN - 1)) // EXPERT_TILE) + 2 * N_GROUPS
    step = jnp.arange(n_steps, dtype=I32)
    group = jnp.sum(step[:, None] >= ends[None, :], axis=1).astype(I32)
    live = (group < N_GROUPS).astype(I32)
    return dst, n_steps * EXPERT_TILE, jnp.minimum(group, N_GROUPS - 1), live


def _scatter(dst, xn16, comb, total_rows):
    n = xn16.shape[0]
    tm = EXPERT_TILE
    row = lambda i, dst: (i, 0)
    any_spec = pl.BlockSpec(memory_space=pl.ANY)
    local_rows = tm + SORT_ROWS_PAD + tm
    grid_spec = pltpu.PrefetchScalarGridSpec(
        num_scalar_prefetch=1, grid=(n // tm,),
        in_specs=[pl.BlockSpec((tm, D_MODEL), row), pl.BlockSpec((tm, LANES), row), any_spec, any_spec],
        out_specs=[pl.BlockSpec((tm, LANES), row), any_spec, any_spec],
        scratch_shapes=[pltpu.VMEM((local_rows, D_MODEL), BF16), pltpu.VMEM((local_rows, LANES), F32),
                        pltpu.SemaphoreType.DMA((2, N_GROUPS))])
    return pl.pallas_call(
        _scatter_kernel,
        grid_spec=grid_spec,
        out_shape=[jax.ShapeDtypeStruct((n, LANES), I32), jax.ShapeDtypeStruct((total_rows, D_MODEL), BF16),
                   jax.ShapeDtypeStruct((total_rows, LANES), F32)],
        input_output_aliases={3: 1, 4: 2},
        compiler_params=_cparams(("arbitrary",), 32),
        name="scatter",
    )(dst, xn16, comb, jnp.zeros((total_rows, D_MODEL), BF16), jnp.zeros((total_rows, LANES), F32))


def _post(x, osb, omb, gate, mem_k16, mem_v16, rows_per_batch, tm, w, rows_per_mem=None, keys_per_mem=None):
    n = x.shape[0]
    nt = n // tm
    tiles_per_batch = rows_per_batch // tm
    row = lambda i: (i, 0)
    const = lambda i: (0, 0)
    mem = lambda i: (i // tiles_per_batch, 0, 0)
    n_mem = mem_k16.shape[1]
    full = lambda a: pl.BlockSpec(a.shape, const)
    out_specs = [pl.BlockSpec((tm, D_MODEL), row), pl.BlockSpec((tm, D_MODEL), row),
                 pl.BlockSpec((tm, LANES), row), pl.BlockSpec((1, 1, LANES), lambda i: (i, 0, 0))]
    out_shape = [jax.ShapeDtypeStruct((n, D_MODEL), F32), jax.ShapeDtypeStruct((n, D_MODEL), BF16),
                 jax.ShapeDtypeStruct((n, LANES), F32), jax.ShapeDtypeStruct((nt, 1, LANES), F32)]
    return pl.pallas_call(
        functools.partial(_post_kernel, rows_per_mem=rows_per_mem, keys_per_mem=keys_per_mem),
        grid=(nt,),
        in_specs=[pl.BlockSpec((tm, D_MODEL), row), pl.BlockSpec((tm, W_MIX), row), pl.BlockSpec((tm, W_MIX), row),
                  pl.BlockSpec((tm, 2 * D_MODEL), row),
                  full(w["w_out_sb"]), full(w["w_out_mb"]), full(w["w_out"]),
                  full(w["norm_mem_g"]), full(w["w_q_mem"]),
                  pl.BlockSpec((1, n_mem, W_MEM), mem), pl.BlockSpec((1, n_mem, W_MEM), mem),
                  full(w["w_o_mem"]), full(w["norm_ffn_g"]),
                  full(w["w_router_hilo"]), full(w["w_router_hi"]), full(w["b_router"])],
        out_specs=out_specs,
        out_shape=out_shape,
        compiler_params=_cparams(("parallel",), 48),
        name="post",
    )(x, osb, omb, gate, w["w_out_sb"], w["w_out_mb"], w["w_out"], w["norm_mem_g"], w["w_q_mem"],
      mem_k16, mem_v16, w["w_o_mem"], w["norm_ffn_g"],
      w["w_router_hilo"], w["w_router_hi"], w["b_router"])


def _moe_kernel(xn_ref, comb_ref, x2_ref, wg_ref, wu_ref, wd_ref, gfin_ref, y_ref, acc_ref):
    e = pl.program_id(1)

    @pl.when(e == 0)
    def _():
        acc_ref[...] = jnp.zeros_like(acc_ref)

    xn = xn_ref[...]
    hg = _dot(xn, wg_ref[0])
    hu = _dot(xn, wu_ref[0])
    lane = lax.broadcasted_iota(I32, (1, LANES), 1)
    weight = jnp.sum(jnp.where(lane == e, comb_ref[...], 0.0), axis=1, keepdims=True)
    hidden = (hg / (1.0 + jnp.exp(-hg))) * hu * weight
    acc_ref[...] += _dot(hidden.astype(BF16), wd_ref[0])

    @pl.when(e == pl.num_programs(1) - 1)
    def _():
        y_ref[...] = _rms(x2_ref[...] + acc_ref[...], gfin_ref[...])


def _moe(xn16, comb, x2, wg16, wu16, wd16, gfin, tm):
    n = xn16.shape[0]
    nt = n // tm
    row = lambda i, e: (i, 0)
    exp_w = lambda i, e: (e, 0, 0)
    return pl.pallas_call(
        _moe_kernel,
        grid=(nt, N_EXPERTS),
        in_specs=[pl.BlockSpec((tm, D_MODEL), row), pl.BlockSpec((tm, LANES), row), pl.BlockSpec((tm, D_MODEL), row),
                  pl.BlockSpec((1, D_MODEL, D_EXPERT), exp_w), pl.BlockSpec((1, D_MODEL, D_EXPERT), exp_w),
                  pl.BlockSpec((1, D_EXPERT, D_MODEL), exp_w),
                  pl.BlockSpec((1, D_MODEL), lambda i, e: (0, 0))],
        out_specs=pl.BlockSpec((tm, D_MODEL), row),
        out_shape=jax.ShapeDtypeStruct((n, D_MODEL), F32),
        scratch_shapes=[pltpu.VMEM((tm, D_MODEL), F32)],
        compiler_params=_cparams(("parallel", "arbitrary"), 48),
        name="moe",
    )(xn16, comb, x2, wg16, wu16, wd16, gfin)


def _group_experts_kernel(grp_ref, live_ref, xs_ref, cs_ref, wg_ref, wu_ref, wd_ref, ys_ref, acc_ref):
    j, e = pl.program_id(0), pl.program_id(1)

    @pl.when(live_ref[j] > 0)
    def _():
        @pl.when(e == 0)
        def _():
            acc_ref[...] = jnp.zeros_like(acc_ref)

        xs = xs_ref[...]
        cs = cs_ref[...]
        lane = lax.broadcasted_iota(I32, (1, LANES), 1)
        pairs = [(_dot(xs, wg_ref[0, j]), _dot(xs, wu_ref[0, j])) for j in range(EXPERTS_PER_STEP)]
        hidden = []
        for j, (hg, hu) in enumerate(pairs):
            weight = jnp.sum(jnp.where(lane == e * EXPERTS_PER_STEP + j, cs, 0.0), axis=1, keepdims=True)
            hidden.append(((hg / (1.0 + jnp.exp(-hg))) * hu * weight).astype(BF16))
        down = [_dot(hidden[j], wd_ref[0, j]) for j in range(EXPERTS_PER_STEP)]
        acc_ref[...] += functools.reduce(jnp.add, down)

        @pl.when(e == pl.num_programs(1) - 1)
        def _():
            ys_ref[...] = acc_ref[...]

    @pl.when(jnp.logical_and(live_ref[j] == 0, e == 0))
    def _():
        ys_ref[...] = jnp.zeros_like(ys_ref)


def _group_experts(xs, cs, tile_group, tile_live, wg16, wu16, wd16):
    n_steps = tile_group.shape[0]
    by_group = lambda a: a.reshape(N_GROUPS, EXPERTS_PER_GROUP, *a.shape[1:])
    rows = lambda j, e, grp, live: (j, 0)
    inner = EXPERTS_PER_GROUP // EXPERTS_PER_STEP
    expert = lambda j, e, grp, live: (grp[j], e * live[j] + (inner - 1) * (1 - live[j]), 0, 0)
    grid_spec = pltpu.PrefetchScalarGridSpec(
        num_scalar_prefetch=2, grid=(n_steps, inner),
        in_specs=[pl.BlockSpec((EXPERT_TILE, D_MODEL), rows), pl.BlockSpec((EXPERT_TILE, LANES), rows),
                  pl.BlockSpec((1, EXPERTS_PER_STEP, D_MODEL, D_EXPERT), expert),
                  pl.BlockSpec((1, EXPERTS_PER_STEP, D_MODEL, D_EXPERT), expert),
                  pl.BlockSpec((1, EXPERTS_PER_STEP, D_EXPERT, D_MODEL), expert)],
        out_specs=pl.BlockSpec((EXPERT_TILE, D_MODEL), rows),
        scratch_shapes=[pltpu.VMEM((EXPERT_TILE, D_MODEL), F32)])
    return pl.pallas_call(
        _group_experts_kernel,
        grid_spec=grid_spec,
        out_shape=jax.ShapeDtypeStruct((xs.shape[0], D_MODEL), F32),
        compiler_params=_cparams(("arbitrary", "arbitrary"), 48),
        name="group_experts",
    )(tile_group, tile_live, xs, cs, by_group(wg16), by_group(wu16), by_group(wd16))


def _gather_norm_kernel(dst_ref, pos_ref, x2_ref, gfin_ref, ys_hbm, y_ref, stage, local, sem):
    k = pl.program_id(0)
    tm = x2_ref.shape[0]
    sorted_rows = tm + SORT_ROWS_PAD
    slot = k % 2

    def window_copies(step, into):
        return [pltpu.make_async_copy(
            ys_hbm.at[pl.ds(pl.multiple_of(dst_ref[step, g], GROUP_RUN_ALIGN), tm)],
            stage.at[into, g], sem.at[into, g]) for g in range(N_GROUPS)]

    @pl.when(k == 0)
    def _():
        local[...] = jnp.zeros_like(local)
        for c in window_copies(0, 0):
            c.start()

    @pl.when(k + 1 < pl.num_programs(0))
    def _():
        for c in window_copies(k + 1, 1 - slot):
            c.start()

    for c in window_copies(k, slot):
        c.wait()
    run_start = _run_starts(dst_ref, k)
    for g in range(N_GROUPS):
        local[pl.ds(pl.multiple_of(run_start[g], GROUP_RUN_ALIGN), tm), :] = stage[slot, g]
    local[pl.ds(pl.multiple_of(run_start[N_GROUPS], GROUP_RUN_ALIGN), tm), :] = jnp.zeros((tm, local.shape[1]), F32)

    pick = (lax.broadcasted_iota(I32, (tm, sorted_rows), 1) == pos_ref[:, 0:1]).astype(BF16)
    hi, lo = _split_bf16(local[0:sorted_rows, :])
    y_ref[...] = _rms(x2_ref[...] + (_dot(pick, hi) + _dot(pick, lo)), gfin_ref[...])


def _gather_norm(dst, pos, x2, gfin, ys):
    n = x2.shape[0]
    tm = EXPERT_TILE
    nt = n // tm
    row = lambda i, offs: (i, 0)
    grid_spec = pltpu.PrefetchScalarGridSpec(
        num_scalar_prefetch=1, grid=(nt,),
        in_specs=[pl.BlockSpec((tm, LANES), row), pl.BlockSpec((tm, D_MODEL), row),
                  pl.BlockSpec((1, D_MODEL), lambda i, offs: (0, 0)), pl.BlockSpec(memory_space=pl.ANY)],
        out_specs=pl.BlockSpec((tm, D_MODEL), row),
        scratch_shapes=[pltpu.VMEM((2, N_GROUPS, tm, D_MODEL), F32),
                        pltpu.VMEM((tm + SORT_ROWS_PAD + tm, D_MODEL), F32),
                        pltpu.SemaphoreType.DMA((2, N_GROUPS))])
    return pl.pallas_call(
        _gather_norm_kernel,
        grid_spec=grid_spec,
        out_shape=jax.ShapeDtypeStruct((n, D_MODEL), F32),
        compiler_params=_cparams(("arbitrary",), 56),
        name="gather_norm",
    )(dst, pos, x2, gfin, ys)


NEW_ROWS = 8


def _new_token_page(ref):
    rows = ref[0]
    return jnp.concatenate([rows, jnp.zeros((PAGE_SIZE - rows.shape[0], rows.shape[1]), rows.dtype)],
                           axis=0).astype(BF16)


def _own_head_block(full, n_new):
    row_head = lax.broadcasted_iota(I32, (full.shape[0], 1), 0) // n_new
    out = jnp.zeros((full.shape[0], HEAD_DIM), F32)
    for h in range(N_HEADS):
        out = jnp.where(row_head == h, full[:, h * HEAD_DIM:(h + 1) * HEAD_DIM], out)
    return out


def _sb_sample_kernel(pt_ref, q_ref, kn_ref, vn_ref, kc_hbm, vc_hbm, o_ref, kbuf, vbuf, sem, *, n_new, n_pages):
    b = pl.program_id(0)
    q = q_ref[0]
    rows = q.shape[0]
    later = _later_keys(PAGE_SIZE)

    def page_copies(p, slot, row=b):
        page = pt_ref[row, p]
        return (pltpu.make_async_copy(kc_hbm.at[page], kbuf.at[slot], sem.at[0, slot]),
                pltpu.make_async_copy(vc_hbm.at[page], vbuf.at[slot], sem.at[1, slot]))

    def start(p, slot, row=b):
        for c in page_copies(p, slot, row):
            c.start()

    def wait(p, slot):
        for c in page_copies(p, slot):
            c.wait()

    slot_of = lambda p: (n_pages - 1 - p) % 2

    @pl.when(b == 0)
    def _():
        start(n_pages - 1, 0)

    key_slot = lax.broadcasted_iota(I32, (rows, PAGE_SIZE), 1)
    tok = lax.broadcasted_iota(I32, (rows, PAGE_SIZE), 0) % n_new
    a, run = _stick_breaking_tile(_dot_nt(q, _new_token_page(kn_ref)), later, jnp.zeros((rows, 1), F32),
                                  key_slot < tok)
    acc = _dot(a.astype(BF16), _new_token_page(vn_ref))

    def cond(carry):
        p, alive = carry[0], carry[1]
        return jnp.logical_and(p >= 0, alive > 0)

    def body(carry):
        p, _, run, acc = carry
        slot = slot_of(p)

        @pl.when(p > 0)
        def _():
            start(p - 1, 1 - slot)

        wait(p, slot)
        z = _dot(q, kbuf[slot].reshape(W_MIX, PAGE_SIZE).astype(BF16))
        a, run = _stick_breaking_tile(z, later, run, None)
        acc = acc + _dot_nt(a.astype(BF16), vbuf[slot].reshape(W_MIX, PAGE_SIZE).astype(BF16))
        alive = (jnp.max(run) > SB_DEAD).astype(I32)
        return (p - 1, alive, run, acc)

    p_next, _, _, acc = lax.while_loop(cond, body, (jnp.int32(n_pages - 1), jnp.int32(1), run, acc))

    @pl.when(p_next >= 0)
    def _():
        wait(p_next, slot_of(p_next))

    @pl.when(b + 1 < pl.num_programs(0))
    def _():
        start(n_pages - 1, 0, b + 1)

    o_ref[0] = _own_head_block(acc, n_new)


def _sb_sample(page_table, q_bd, kn_t, vn_t, cache_kt, cache_vt, n_new):
    nb, n_pages = page_table.shape
    rows = N_HEADS * n_new
    per_b = lambda b, pt: (b, 0, 0)
    page_buffers = pltpu.VMEM((2, N_HEADS, HEAD_DIM, PAGE_SIZE), F32)
    grid_spec = pltpu.PrefetchScalarGridSpec(
        num_scalar_prefetch=1, grid=(nb,),
        in_specs=[pl.BlockSpec((1, rows, W_MIX), per_b),
                  pl.BlockSpec((1, NEW_ROWS, W_MIX), per_b),
                  pl.BlockSpec((1, NEW_ROWS, W_MIX), per_b),
                  pl.BlockSpec(memory_space=pl.ANY),
                  pl.BlockSpec(memory_space=pl.ANY)],
        out_specs=pl.BlockSpec((1, rows, HEAD_DIM), per_b),
        scratch_shapes=[page_buffers, page_buffers, pltpu.SemaphoreType.DMA((2, 2))])
    return pl.pallas_call(
        functools.partial(_sb_sample_kernel, n_new=n_new, n_pages=n_pages),
        grid_spec=grid_spec,
        out_shape=jax.ShapeDtypeStruct((nb, rows, HEAD_DIM), F32),
        compiler_params=_cparams(("arbitrary",), 32),
        name="sb_sample",
    )(page_table, q_bd, kn_t, vn_t, cache_kt, cache_vt)


PAGES_PER_BLOCK = MOBA_BLOCK // PAGE_SIZE
MOBA_SAMPLE_BLOCKS_PER_STEP = 8


def _moba_sample_kernel(pt_ref, q_ref, kn_ref, vn_ref, *refs, n_new):
    pages_per_step = MOBA_SAMPLE_BLOCKS_PER_STEP * PAGES_PER_BLOCK
    k_refs, v_refs = refs[:pages_per_step], refs[pages_per_step:2 * pages_per_step]
    o_ref, pm_ref, pl_ref, pg_ref, po_ref = refs[2 * pages_per_step:]
    step = pl.program_id(1)
    n_blocks = pl.num_programs(1) * MOBA_SAMPLE_BLOCKS_PER_STEP
    q = q_ref[0]
    rows = q.shape[0]
    block_lane = lax.broadcasted_iota(I32, (1, LANES), 1)

    @pl.when(step == 0)
    def _():
        pm_ref[...] = jnp.full(pm_ref.shape, NEG_INF, F32)
        pl_ref[...] = jnp.zeros_like(pl_ref)
        pg_ref[...] = jnp.zeros_like(pg_ref)

    def block_pages(page_refs, j):
        pages = [r[0].reshape(W_MIX, PAGE_SIZE).astype(BF16)
                 for r in page_refs[j * PAGES_PER_BLOCK:(j + 1) * PAGES_PER_BLOCK]]
        return jnp.concatenate(pages, axis=1)

    pm, pl_, pg = pm_ref[...], pl_ref[...], pg_ref[...]
    blocks = range(MOBA_SAMPLE_BLOCKS_PER_STEP)
    scores = [_dot(q, block_pages(k_refs, j)) for j in blocks]
    maxes = [jnp.max(z, axis=1, keepdims=True) for z in scores]
    weights = [jnp.exp2(z - m) for z, m in zip(scores, maxes)]
    for j in blocks:
        blk = step * MOBA_SAMPLE_BLOCKS_PER_STEP + j
        here = block_lane == blk
        pm = jnp.where(here, maxes[j], pm)
        pl_ = jnp.where(here, jnp.sum(weights[j], axis=1, keepdims=True), pl_)
        pg = jnp.where(here, jnp.sum(scores[j], axis=1, keepdims=True), pg)
        po_ref[blk] = _dot_nt(weights[j].astype(BF16), block_pages(v_refs, j))
    pm_ref[...], pl_ref[...], pg_ref[...] = pm, pl_, pg

    @pl.when(step == pl.num_programs(1) - 1)
    def _():
        chosen = _top_blocks(pg, block_lane < n_blocks, block_lane)
        slot = lax.broadcasted_iota(I32, (rows, PAGE_SIZE), 1)
        tok = lax.broadcasted_iota(I32, (rows, PAGE_SIZE), 0) % n_new
        zn = jnp.where(slot <= tok, _dot_nt(q, _new_token_page(kn_ref)), NEG_INF)
        m_all = jnp.maximum(jnp.max(jnp.where(chosen, pm, NEG_INF), axis=1, keepdims=True),
                            jnp.max(zn, axis=1, keepdims=True))
        pn = jnp.exp2(zn - m_all)
        w = jnp.where(chosen, jnp.exp2(pm - m_all), 0.0)
        total = jnp.sum(w * pl_, axis=1, keepdims=True) + jnp.sum(pn, axis=1, keepdims=True)
        acc = _dot(pn.astype(BF16), _new_token_page(vn_ref))
        for n in range(po_ref.shape[0]):
            acc = acc + w[:, n:n + 1] * po_ref[n]
        o_ref[0] = _own_head_block(acc, n_new) / total


def _moba_sample(page_table, q_bd, kn_t, vn_t, cache_kt, cache_vt, n_new):
    nb, n_pages = page_table.shape
    rows = N_HEADS * n_new
    pages_per_step = MOBA_SAMPLE_BLOCKS_PER_STEP * PAGES_PER_BLOCK
    n_steps = n_pages // pages_per_step
    per_b = lambda b, s, pt: (b, 0, 0)
    page_spec = lambda j: pl.BlockSpec((1, N_HEADS, HEAD_DIM, PAGE_SIZE),
                                       lambda b, s, pt: (pt[b, s * pages_per_step + j], 0, 0, 0))
    page_specs = [page_spec(j) for j in range(pages_per_step)]
    grid_spec = pltpu.PrefetchScalarGridSpec(
        num_scalar_prefetch=1, grid=(nb, n_steps),
        in_specs=[pl.BlockSpec((1, rows, W_MIX), per_b),
                  pl.BlockSpec((1, NEW_ROWS, W_MIX), per_b),
                  pl.BlockSpec((1, NEW_ROWS, W_MIX), per_b)] + page_specs + page_specs,
        out_specs=pl.BlockSpec((1, rows, HEAD_DIM), per_b),
        scratch_shapes=[pltpu.VMEM((rows, LANES), F32), pltpu.VMEM((rows, LANES), F32),
                        pltpu.VMEM((rows, LANES), F32),
                        pltpu.VMEM((n_pages // PAGES_PER_BLOCK, rows, W_MIX), F32)])
    return pl.pallas_call(
        functools.partial(_moba_sample_kernel, n_new=n_new),
        grid_spec=grid_spec,
        out_shape=jax.ShapeDtypeStruct((nb, rows, HEAD_DIM), F32),
        compiler_params=_cparams(("parallel", "arbitrary"), 32),
        name="moba_sample",
    )(page_table, q_bd, kn_t, vn_t, *([cache_kt] * pages_per_step), *([cache_vt] * pages_per_step))


def _rope_tables(pos):
    half = HEAD_DIM // 2
    inv_freq = ROPE_THETA ** (-jnp.arange(half, dtype=F32) / half)
    ang = pos.astype(F32)[:, None] * inv_freq[None, :]
    cos = jnp.cos(ang)
    sin = jnp.sin(ang)
    heads_per_tile = LANES // HEAD_DIM
    return (jnp.tile(jnp.concatenate([cos, cos], axis=1), (1, heads_per_tile)),
            jnp.tile(jnp.concatenate([-sin, sin], axis=1), (1, heads_per_tile)))


def _pad_lanes(a):
    return jnp.pad(a, ((0, 0), (0, LANES - a.shape[1])))


def _prepare_weights(norm_mix_g, w_in, b_gate, w_out_sb, w_out_mb, w_out, norm_mem_g, norm_memsrc_g, w_q_mem,
                     w_kv_mem, w_o_mem, norm_ffn_g, w_router_group, b_router_group, w_router_expert,
                     b_router_expert, w_gate_e, w_up_e, w_down_e, norm_final_g):
    row = lambda v: v.reshape(1, -1).astype(F32)
    w_re = w_router_expert.transpose(1, 0, 2).reshape(D_MODEL, N_EXPERTS)
    wr_hi, wr_lo = _split_bf16(_pad_lanes(jnp.concatenate([w_re, w_router_group], axis=1)))
    b_router = _pad_lanes(jnp.concatenate([row(b_router_expert), row(b_router_group)], axis=1))
    return dict(
        norm_mix_g=row(norm_mix_g), w_in=w_in.astype(BF16), b_gate=row(b_gate),
        w_out_sb=w_out_sb.astype(BF16), w_out_mb=w_out_mb.astype(BF16), w_out=w_out.astype(BF16),
        norm_mem_g=row(norm_mem_g), norm_memsrc_g=row(norm_memsrc_g), w_q_mem=w_q_mem.astype(BF16),
        w_kv_mem=w_kv_mem.astype(BF16), w_o_mem=w_o_mem.astype(BF16), norm_ffn_g=row(norm_ffn_g),
        w_router_hilo=jnp.concatenate([wr_hi, wr_lo], axis=1), w_router_hi=wr_hi, b_router=b_router,
        w_gate_e=w_gate_e.astype(BF16), w_up_e=w_up_e.astype(BF16), w_down_e=w_down_e.astype(BF16),
        norm_final_g=row(norm_final_g))


def _tail(x, osb, omb, gate, mem_k16, mem_v16, rows_per_batch, w, tm_post, tm_moe, **mem_mask):
    x2, xn16, comb, _ = _post(x, osb, omb, gate, mem_k16, mem_v16, rows_per_batch, tm_post, w, **mem_mask)
    return _moe(xn16, comb, x2, w["w_gate_e"], w["w_up_e"], w["w_down_e"], w["norm_final_g"], tm_moe)


def kernel(x_prompt, x_sample, mem_prompt, cache_sb_k, cache_sb_v, cache_mb_k, cache_mb_v, cache_mem_k, cache_mem_v, page_table, norm_mix_g, w_in, b_gate, w_out_sb, w_out_mb, w_out, norm_mem_g, norm_memsrc_g, w_q_mem, w_kv_mem, w_o_mem, norm_ffn_g, w_router_group, b_router_group, w_router_expert, b_router_expert, w_gate_e, w_up_e, w_down_e, norm_final_g):
    w = _prepare_weights(norm_mix_g, w_in, b_gate, w_out_sb, w_out_mb, w_out, norm_mem_g, norm_memsrc_g, w_q_mem,
                         w_kv_mem, w_o_mem, norm_ffn_g, w_router_group, b_router_group, w_router_expert,
                         b_router_expert, w_gate_e, w_up_e, w_down_e, norm_final_g)
    batch, seq, _ = x_prompt.shape
    dec_batch, n_new, _ = x_sample.shape
    n_mem = mem_prompt.shape[1]
    n_pages = page_table.shape[1]
    past_len = n_pages * PAGE_SIZE
    assert seq % (MOBA_BLOCK * MOBA_BLOCKS_PER_ITER) == 0 and seq % (MOBA_BLOCK * MOBA_TILES_PER_STEP) == 0
    assert seq % (SB_TILE * SB_TILES_PER_STEP) == 0 and (batch * seq) % EXPERT_TILE == 0
    assert n_new <= PAGE_SIZE and n_pages // PAGES_PER_BLOCK <= LANES
    assert n_pages % (MOBA_SAMPLE_BLOCKS_PER_STEP * PAGES_PER_BLOCK) == 0
    heads = lambda t, b, s: t.reshape(b, s, N_HEADS, HEAD_DIM)

    xp = x_prompt.reshape(batch * seq, D_MODEL)
    cos_p, sin_p = _rope_tables(jnp.arange(seq, dtype=I32))
    (qsb, qmb, gate, ksb_t, vsb_t, kmb_t, vmb_t, ksb16, vsb16_t, kmb16, vmb16_t, kmean) = _proj(
        xp, w["norm_mix_g"], w["w_in"], w["b_gate"], cos_p, sin_p, PROJ_TILE, rows_per_batch=seq)
    osb = _sb_prompt(qsb, ksb16, vsb16_t, batch, seq, SB_TILE)
    omb = _moba_prompt(qmb, kmb16, vmb16_t, kmean.reshape(batch * seq // MOBA_BLOCK, W_MIX), batch, seq)
    mem_k, mem_v = _mem_kv(mem_prompt.reshape(batch * n_mem, D_MODEL), w["norm_memsrc_g"], w["w_kv_mem"])
    x2, xn16, comb, counts = _post(xp, osb, omb, gate, mem_k.astype(BF16).reshape(batch, n_mem, W_MEM),
                                   mem_v.astype(BF16).reshape(batch, n_mem, W_MEM), seq, EXPERT_TILE, w)
    n_tiles = batch * seq // EXPERT_TILE
    dst, total_rows, tile_group, tile_live = _group_layout(
        counts[:, 0, :N_GROUPS].astype(I32), batch * seq, n_tiles)
    pos, rows_by_group, weights_by_group = _scatter(dst, xn16, comb, total_rows)
    expert_out = _group_experts(rows_by_group, weights_by_group, tile_group, tile_live,
                                w["w_gate_e"], w["w_up_e"], w["w_down_e"])
    y_prompt = _gather_norm(dst, pos, x2, w["norm_final_g"], expert_out)

    rows_s = dec_batch * n_new
    xs = x_sample.reshape(rows_s, D_MODEL)
    cos_s, sin_s = _rope_tables(past_len + (jnp.arange(rows_s, dtype=I32) % n_new))
    (qsb_s, qmb_s, gate_s, ksb_s, vsb_s, kmb_s, vmb_s) = _proj(
        xs, w["norm_mix_g"], w["w_in"], w["b_gate"], cos_s, sin_s, rows_s)

    def block_diagonal(q16):
        q_cols = heads(q16, dec_batch, n_new).transpose(0, 2, 1, 3).reshape(dec_batch, N_HEADS * n_new, HEAD_DIM)
        own = (jnp.arange(N_HEADS * n_new)[:, None] // n_new) == (jnp.arange(W_MIX)[None, :] // HEAD_DIM)
        return jnp.where(own[None], jnp.tile(q_cols, (1, 1, N_HEADS)), jnp.zeros((), q16.dtype))

    def new_page(t):
        return jnp.pad(t.reshape(dec_batch, n_new, W_MIX), ((0, 0), (0, NEW_ROWS - n_new), (0, 0)))

    def token_rows(o):
        o = o.reshape(dec_batch, N_HEADS, n_new, HEAD_DIM).transpose(0, 2, 1, 3)
        return o.reshape(rows_s, W_MIX).astype(BF16)

    pages = lambda c: c.transpose(0, 2, 3, 1)
    osb_s = token_rows(_sb_sample(page_table, block_diagonal(qsb_s), new_page(ksb_s), new_page(vsb_s),
                                  pages(cache_sb_k), pages(cache_sb_v), n_new))
    omb_s = token_rows(_moba_sample(page_table, block_diagonal(qmb_s), new_page(kmb_s), new_page(vmb_s),
                                    pages(cache_mb_k), pages(cache_mb_v), n_new))
    y_sample = _tail(xs, osb_s, omb_s, gate_s,
                     cache_mem_k.astype(BF16).reshape(1, dec_batch * n_mem, W_MEM),
                     cache_mem_v.astype(BF16).reshape(1, dec_batch * n_mem, W_MEM),
                     rows_s, w, rows_s, rows_s, rows_per_mem=n_new, keys_per_mem=n_mem)

    mem_heads = lambda t: t.reshape(batch, n_mem, H_MEM, HD_MEM)
    from_head_major = lambda t: t.reshape(batch, N_HEADS, HEAD_DIM, seq).transpose(0, 3, 1, 2)
    return (y_prompt.reshape(batch, seq, D_MODEL), y_sample.reshape(dec_batch, n_new, D_MODEL),
            from_head_major(ksb_t), from_head_major(vsb_t), from_head_major(kmb_t), from_head_major(vmb_t),
            mem_heads(mem_k), mem_heads(mem_v),
            heads(ksb_s, dec_batch, n_new), heads(vsb_s, dec_batch, n_new),
            heads(kmb_s, dec_batch, n_new), heads(vmb_s, dec_batch, n_new))
```

```python
import functools

import jax
import jax.numpy as jnp
from jax import lax
from jax.experimental import pallas as pl
from jax.experimental.pallas import tpu as pltpu

F32 = jnp.float32
BF16 = jnp.bfloat16
I32 = jnp.int32

D_MODEL = 1024
N_HEADS = 8
HEAD_DIM = 64
W_MIX = N_HEADS * HEAD_DIM
PAGE_SIZE = 128
MOBA_BLOCK = 256
MOBA_TOPK = 3
H_MEM = 4
HD_MEM = 128
W_MEM = H_MEM * HD_MEM
N_GROUPS = 4
EXPERTS_PER_GROUP = 8
N_EXPERTS = N_GROUPS * EXPERTS_PER_GROUP
D_EXPERT = 256
ROPE_THETA = 10000.0
RMS_EPS = 1e-6
W_IN_COLS = 6 * W_MIX + 2 * D_MODEL

LOG2_E = 1.4426950408889634
LANES = 128
BF16_ROWS = 16
HEADS_PER_STEP = 4
QUAD = HEADS_PER_STEP * HEAD_DIM
SB_DEAD = -160.0 * LOG2_E
PROJ_TILE = 512
MOBA_TILES_PER_STEP = 4
MOBA_BLOCKS_PER_ITER = 2
SB_TILE = 128
SB_TILES_PER_STEP = 4
NEG_INF = float("-inf")
MIB = 1024 * 1024


def _cparams(semantics, vmem_mib):
    return pltpu.CompilerParams(dimension_semantics=semantics, vmem_limit_bytes=vmem_mib * MIB)


def _rms(x, g):
    ms = jnp.mean(x * x, axis=-1, keepdims=True)
    return (x * lax.rsqrt(ms + RMS_EPS)) * g


def _dot(a, b):
    return jnp.dot(a, b, preferred_element_type=F32)


def _dot_nt(a, b):
    return lax.dot_general(a, b, (((1,), (1,)), ((), ())), preferred_element_type=F32)


def _split_bf16(x):
    hi = x.astype(BF16)
    lo = (x - hi.astype(F32)).astype(BF16)
    return hi, lo


def _later_keys(n):
    return (lax.broadcasted_iota(I32, (n, n), 0) > lax.broadcasted_iota(I32, (n, n), 1)).astype(BF16)


def _stick_breaking_tile(z, later, carried, valid):
    l1p = jnp.log2(1.0 + jnp.exp2(-jnp.abs(z)))
    log_keep = -(jnp.maximum(z, 0.0) + l1p)
    if valid is not None:
        log_keep = jnp.where(valid, log_keep, 0.0)
    hi, lo = _split_bf16(log_keep)
    both = _dot(jnp.concatenate([hi, lo], axis=0), later)
    inner = both[:z.shape[0]] + both[z.shape[0]:]
    a = jnp.exp2(jnp.minimum(z, 0.0) - l1p + carried + inner)
    if valid is not None:
        a = jnp.where(valid, a, 0.0)
    return a, carried + inner[:, 0:1] + log_keep[:, 0:1]


def _proj_kernel(x_ref, g_ref, w_ref, bg_ref, cos_ref, sin_ref, qsb_ref, qmb_ref, gate_ref, *kv_refs, head_major):
    xb = _rms(x_ref[...], g_ref[...]).astype(BF16)

    def seg(lo, width):
        return _dot(xb, w_ref[:, lo:lo + width])

    lane = lax.broadcasted_iota(I32, (1, W_MIX), 1)
    first_half = (lane % HEAD_DIM) < (HEAD_DIM // 2)
    cos = jnp.concatenate([cos_ref[...]] * (W_MIX // LANES), axis=1)
    sin = jnp.concatenate([sin_ref[...]] * (W_MIX // LANES), axis=1)

    def rope(t):
        partner = jnp.where(first_half, pltpu.roll(t, W_MIX - HEAD_DIM // 2, 1),
                            pltpu.roll(t, HEAD_DIM // 2, 1))
        return t * cos + partner * sin

    scale = HEAD_DIM ** -0.5 * LOG2_E
    qsb_ref[...] = (seg(0, W_MIX) * scale).astype(BF16)
    qmb_ref[...] = (rope(seg(3 * W_MIX, W_MIX)) * scale).astype(BF16)
    gl = seg(6 * W_MIX, 2 * D_MODEL) + bg_ref[...]
    gate_ref[...] = 1.0 / (1.0 + jnp.exp(-gl))
    ksb = seg(W_MIX, W_MIX)
    vsb = seg(2 * W_MIX, W_MIX)
    kmb = rope(seg(4 * W_MIX, W_MIX))
    vmb = seg(5 * W_MIX, W_MIX)
    if not head_major:
        for ref, t in zip(kv_refs, (ksb, vsb, kmb, vmb)):
            ref[...] = t
        return
    ksb_t_ref, vsb_t_ref, kmb_t_ref, vmb_t_ref, ksb16_ref, vsb16_t_ref, kmb16_ref, vmb16_t_ref, kmean_ref = kv_refs
    ksb_t_ref[0] = ksb.T
    vsb_t = vsb.T
    vsb_t_ref[0] = vsb_t
    kmb_t_ref[0] = kmb.T
    vmb_t = vmb.T
    vmb_t_ref[0] = vmb_t
    ksb16_ref[...] = ksb.astype(BF16)
    vsb16_t_ref[0] = vsb_t.astype(BF16)
    kmb16_ref[...] = kmb.astype(BF16)
    vmb16_t_ref[0] = vmb_t.astype(BF16)
    for j in range(kmean_ref.shape[0]):
        kmean_ref[j] = jnp.mean(kmb[j * MOBA_BLOCK:(j + 1) * MOBA_BLOCK], axis=0, keepdims=True)


def _proj(x, g, w_in16, b_gate, cos, sin, tm, rows_per_batch=None):
    n = x.shape[0]
    nt = n // tm
    n_rope = cos.shape[0] // tm
    row = lambda i: (i, 0)
    const = lambda i: (0, 0)
    head_major = rows_per_batch is not None
    out_shape = [jax.ShapeDtypeStruct((n, W_MIX), BF16)] * 2 + [jax.ShapeDtypeStruct((n, 2 * D_MODEL), F32)]
    out_specs = [pl.BlockSpec((tm, W_MIX), row)] * 2 + [pl.BlockSpec((tm, 2 * D_MODEL), row)]
    if head_major:
        tiles = rows_per_batch // tm
        n_batch = n // rows_per_batch
        t_spec = pl.BlockSpec((1, W_MIX, tm), lambda i: (i // tiles, 0, i % tiles))
        rows16 = jax.ShapeDtypeStruct((n, W_MIX), BF16)
        t16 = jax.ShapeDtypeStruct((n_batch, W_MIX, rows_per_batch), BF16)
        out_shape += ([jax.ShapeDtypeStruct((n_batch, W_MIX, rows_per_batch), F32)] * 4
                      + [rows16, t16, rows16, t16, jax.ShapeDtypeStruct((n // MOBA_BLOCK, 1, W_MIX), F32)])
        r_spec = pl.BlockSpec((tm, W_MIX), row)
        out_specs += ([t_spec] * 4 + [r_spec, t_spec, r_spec, t_spec,
                                      pl.BlockSpec((tm // MOBA_BLOCK, 1, W_MIX), lambda i: (i, 0, 0))])
    else:
        out_shape += [jax.ShapeDtypeStruct((n, W_MIX), F32)] * 4
        out_specs += [pl.BlockSpec((tm, W_MIX), row)] * 4
    return pl.pallas_call(
        functools.partial(_proj_kernel, head_major=head_major),
        grid=(nt,),
        in_specs=[pl.BlockSpec((tm, D_MODEL), row),
                  pl.BlockSpec((1, D_MODEL), const),
                  pl.BlockSpec((D_MODEL, W_IN_COLS), const),
                  pl.BlockSpec((1, 2 * D_MODEL), const),
                  pl.BlockSpec((tm, LANES), lambda i: (i % n_rope, 0)),
                  pl.BlockSpec((tm, LANES), lambda i: (i % n_rope, 0))],
        out_specs=out_specs,
        out_shape=out_shape,
        compiler_params=_cparams(("parallel",), 48),
        name="proj",
    )(x, g, w_in16, b_gate, cos, sin)


def _head_masks(width):
    lane_head = lax.broadcasted_iota(I32, (1, width), 1) // HEAD_DIM
    return [lane_head == h for h in range(HEADS_PER_STEP)]


def _sb_prompt_kernel(q_ref, k_ref, vt_ref, o_ref, acc_ref, *, tq):
    i = pl.program_id(2)
    masks = _head_masks(QUAD)
    tiles = range(SB_TILES_PER_STEP)
    chains = [(t, h) for t in tiles for h in range(HEADS_PER_STEP)]
    q = [q_ref[t * tq:(t + 1) * tq, :] for t in tiles]
    qs = {(t, h): jnp.where(masks[h], q[t], jnp.zeros_like(q[t])) for t, h in chains}
    key = lax.broadcasted_iota(I32, (tq, tq), 0)
    qry = lax.broadcasted_iota(I32, (tq, tq), 1)
    after = (qry > key).astype(BF16)

    def block(step, dead, diagonal):
        kblk, vtblk = [], []
        for t in tiles:
            kb = i * SB_TILES_PER_STEP + t - step
            if not diagonal:
                dead = {c: (jnp.where(kb < 0, jnp.inf, d) if c[0] == t else d) for c, d in dead.items()}
                kb = jnp.maximum(kb, 0)
            start = pl.multiple_of(kb * tq, tq)
            kblk.append(k_ref[pl.ds(start, tq), :])
            vtblk.append(vt_ref[0, :, pl.ds(start, tq)])
        scores = {c: _dot_nt(kblk[c[0]], qs[c]) for c in chains}
        softplus, parts = {}, {}
        for c in chains:
            z = scores[c]
            sp = jnp.maximum(z, 0.0) + jnp.log2(1.0 + jnp.exp2(-jnp.abs(z)))
            if diagonal:
                sp = jnp.where(key < qry, sp, 0.0)
            softplus[c] = sp
            parts[c] = _split_bf16(sp)
        both = {c: _dot(after, jnp.concatenate(parts[c], axis=1)) for c in chains}
        inner = {c: both[c][:, :tq] + both[c][:, tq:] for c in chains}
        new_dead = {}
        for c in chains:
            t, h = c
            a = jnp.exp2(scores[c] - softplus[c] - (dead[c] + inner[c]))
            if diagonal:
                a = jnp.where(key < qry, a, 0.0)
            rows, cols = slice(h * HEAD_DIM, (h + 1) * HEAD_DIM), slice(t * tq, (t + 1) * tq)
            acc_ref[rows, cols] += _dot(vtblk[t][rows, :], a.astype(BF16))
            new_dead[c] = dead[c] + inner[c][0:1] + softplus[c][0:1]
        return new_dead

    def alive(dead):
        low = functools.reduce(jnp.minimum, [dead[c] for c in chains])
        return (jnp.min(low) < -SB_DEAD).astype(I32)

    acc_ref[...] = jnp.zeros_like(acc_ref)
    dead = block(0, {c: jnp.zeros((1, tq), F32) for c in chains}, True)
    newest = i * SB_TILES_PER_STEP + SB_TILES_PER_STEP - 1

    def cond(carry):
        return jnp.logical_and(carry[0] <= newest, carry[1] > 0)

    def body(carry):
        dead = block(carry[0], dict(zip(chains, carry[2:])), False)
        return (carry[0] + 1, alive(dead)) + tuple(dead[c] for c in chains)

    lax.while_loop(cond, body, (jnp.int32(1), alive(dead)) + tuple(dead[c] for c in chains))
    o_ref[...] = acc_ref[...].T.astype(o_ref.dtype)


def _sb_prompt(q16, k16, vt16, batch, seq, tq):
    rows = tq * SB_TILES_PER_STEP
    nq = seq // rows
    ng = W_MIX // QUAD
    return pl.pallas_call(
        functools.partial(_sb_prompt_kernel, tq=tq),
        grid=(batch, ng, nq),
        in_specs=[pl.BlockSpec((rows, QUAD), lambda b, g, i: (b * nq + i, g)),
                  pl.BlockSpec((seq, QUAD), lambda b, g, i: (b, g)),
                  pl.BlockSpec((1, QUAD, seq), lambda b, g, i: (b, g, 0))],
        out_specs=pl.BlockSpec((rows, QUAD), lambda b, g, i: (b * nq + i, g)),
        out_shape=jax.ShapeDtypeStruct((batch * seq, W_MIX), BF16),
        scratch_shapes=[pltpu.VMEM((QUAD, rows), F32)],
        compiler_params=_cparams(("parallel", "parallel", "arbitrary"), 48),
        name="sb_prompt",
    )(q16, k16, vt16)


def _first_argmax(x, lane, width):
    mx = jnp.max(x, axis=1, keepdims=True)
    idx = jnp.min(jnp.where(x == mx, lane, width), axis=1, keepdims=True)
    return mx, idx


def _top_blocks(gate, eligible, blk):
    g = jnp.where(eligible, gate, NEG_INF)
    sel = jnp.zeros(gate.shape, jnp.bool_)
    for _ in range(MOBA_TOPK):
        _, first = _first_argmax(g, blk, gate.shape[1])
        pick = blk == first
        sel = jnp.logical_or(sel, jnp.logical_and(pick, eligible))
        g = jnp.where(pick, NEG_INF, g)
    return sel


def _top_block_rows(gate, eligible, blk):
    n = gate.shape[0]
    g = jnp.where(eligible, gate, NEG_INF)
    sel = jnp.zeros(gate.shape, jnp.bool_)
    for _ in range(MOBA_TOPK):
        mx = jnp.max(g, axis=0, keepdims=True)
        first = jnp.min(jnp.where(g == mx, blk, n), axis=0, keepdims=True)
        pick = blk == first
        sel = jnp.logical_or(sel, jnp.logical_and(pick, eligible))
        g = jnp.where(pick, NEG_INF, g)
    return sel


def _moba_prompt_kernel(q_ref, k_ref, vt_ref, km_ref, o_ref, sel_ref, acc_ref, *, tq):
    first_tile = pl.program_id(2) * MOBA_TILES_PER_STEP
    tiles = range(MOBA_TILES_PER_STEP)
    chains = [(t, h) for t in tiles for h in range(HEADS_PER_STEP)]
    masks = _head_masks(QUAD)
    q = [q_ref[t * tq:(t + 1) * tq, :] for t in tiles]
    qs = {(t, h): jnp.where(masks[h], q[t], jnp.zeros_like(q[t])) for t, h in chains}
    nb = km_ref.shape[0]
    km_hi, km_lo = _split_bf16(km_ref[...])
    blk = lax.broadcasted_iota(I32, (nb, 1), 0)
    for n, (t, h) in enumerate(chains):
        gate = _dot_nt(km_hi, qs[t, h]) + _dot_nt(km_lo, qs[t, h])
        sel_ref[n] = _top_block_rows(gate, blk < first_tile + t, blk).astype(F32)
    key = lax.broadcasted_iota(I32, (tq, tq), 0)
    qry = lax.broadcasted_iota(I32, (tq, tq), 1)

    def load(kb):
        start = pl.multiple_of(kb * tq, tq)
        return k_ref[pl.ds(start, tq), :], vt_ref[0, :, pl.ds(start, tq)]

    def head_rows(h):
        return slice(h * HEAD_DIM, (h + 1) * HEAD_DIM)

    def tile_cols(t):
        return slice(t * tq, (t + 1) * tq)

    def weighted_values(vtblk, h, p16):
        ones = jnp.ones((BF16_ROWS, vtblk.shape[1]), BF16)
        out = _dot(jnp.concatenate([vtblk[head_rows(h), :], ones], axis=0), p16)
        return out[:HEAD_DIM], out[HEAD_DIM:HEAD_DIM + 1]

    own = [load(first_tile + t) for t in tiles]
    scores = {c: jnp.where(key <= qry, _dot_nt(own[c[0]][0], qs[c]), NEG_INF) for c in chains}
    m_run = {c: jnp.max(scores[c], axis=0, keepdims=True) for c in chains}
    probs = {c: jnp.exp2(scores[c] - m_run[c]).astype(BF16) for c in chains}
    l_run = {}
    for t, h in chains:
        pv, psum = weighted_values(own[t][1], h, probs[t, h])
        acc_ref[head_rows(h), tile_cols(t)] = pv
        l_run[t, h] = psum

    span = MOBA_BLOCKS_PER_ITER
    n_chains = len(chains)

    def body(it, carry):
        m_run = dict(zip(chains, carry[:n_chains]))
        l_run = dict(zip(chains, carry[n_chains:]))
        start = pl.multiple_of(it * (span * tq), span * tq)
        kblk = k_ref[pl.ds(start, span * tq), :]
        vtblk = vt_ref[0, :, pl.ds(start, span * tq)]
        new_m, new_l, scores, alphas, probs = {}, {}, {}, {}, {}

        def score(c):
            scores[c] = _dot_nt(kblk, qs[c])

        def softmax(c):
            n = chains.index(c)
            parts = [scores[c][j * tq:(j + 1) * tq] for j in range(span)]
            chosen = [sel_ref[n, pl.ds(span * it + j, 1), :] > 0.0 for j in range(span)]
            m = m_run[c]
            for s, ch in zip(parts, chosen):
                m = jnp.where(ch, jnp.maximum(m, jnp.max(s, axis=0, keepdims=True)), m)
            new_m[c] = m
            alphas[c] = jnp.exp2(m_run[c] - m)
            probs[c] = jnp.concatenate(
                [jnp.exp2(s - jnp.where(ch, m, jnp.inf)).astype(BF16) for s, ch in zip(parts, chosen)], axis=0)

        def values(c):
            t, h = c
            pv, psum = weighted_values(vtblk, h, probs[c])
            acc_ref[head_rows(h), tile_cols(t)] = alphas[c] * acc_ref[head_rows(h), tile_cols(t)] + pv
            new_l[c] = alphas[c] * l_run[c] + psum

        for stage in (score, softmax, values):
            for c in chains:
                stage(c)
        return tuple(new_m[c] for c in chains) + tuple(new_l[c] for c in chains)

    n_iter = (first_tile + MOBA_TILES_PER_STEP - 1 + span - 1) // span
    out = lax.fori_loop(0, n_iter, body, tuple(m_run[c] for c in chains) + tuple(l_run[c] for c in chains))
    for n, (t, h) in enumerate(chains):
        acc_ref[head_rows(h), tile_cols(t)] = acc_ref[head_rows(h), tile_cols(t)] * (1.0 / out[n_chains + n])
    o_ref[...] = acc_ref[...].T.astype(o_ref.dtype)


def _moba_prompt(q16, k16, vt16, kmean, batch, seq):
    tq = MOBA_BLOCK
    rows = tq * MOBA_TILES_PER_STEP
    nb = seq // tq
    nq = seq // rows
    ng = W_MIX // QUAD
    return pl.pallas_call(
        functools.partial(_moba_prompt_kernel, tq=tq),
        grid=(batch, ng, nq),
        in_specs=[pl.BlockSpec((rows, QUAD), lambda b, g, i: (b * nq + i, g)),
                  pl.BlockSpec((seq, QUAD), lambda b, g, i: (b, g)),
                  pl.BlockSpec((1, QUAD, seq), lambda b, g, i: (b, g, 0)),
                  pl.BlockSpec((nb, QUAD), lambda b, g, i: (b, g))],
        out_specs=pl.BlockSpec((rows, QUAD), lambda b, g, i: (b * nq + i, g)),
        out_shape=jax.ShapeDtypeStruct((batch * seq, W_MIX), BF16),
        scratch_shapes=[pltpu.VMEM((MOBA_TILES_PER_STEP * HEADS_PER_STEP, nb, tq), F32),
                        pltpu.VMEM((QUAD, rows), F32)],
        compiler_params=_cparams(("parallel", "parallel", "arbitrary"), 48),
        name="moba_prompt",
    )(q16, k16, vt16, kmean)


def _mem_kv_kernel(mem_ref, g_ref, w_ref, k_ref, v_ref):
    kv = _dot(_rms(mem_ref[...], g_ref[...]).astype(BF16), w_ref[...])
    k_ref[...] = kv[:, :W_MEM]
    v_ref[...] = kv[:, W_MEM:]


def _mem_kv(mem, g, w_kv16):
    n = mem.shape[0]
    return pl.pallas_call(
        _mem_kv_kernel,
        out_shape=[jax.ShapeDtypeStruct((n, W_MEM), F32)] * 2,
        name="mem_kv",
    )(mem, g, w_kv16)


def _post_kernel(x_ref, osb_ref, omb_ref, gate_ref, wsb_ref, wmb_ref, wo_ref,
                 gmem_ref, wq_ref, mk_ref, mv_ref, wom_ref, gffn_ref,
                 wr_hilo_ref, wr_hi_ref, br_ref,
                 x2_ref, xn_ref, comb_ref, count_ref, *, rows_per_mem, keys_per_mem):
    gate = gate_ref[...]
    h = gate[:, :D_MODEL] * _dot(osb_ref[...], wsb_ref[...]) + gate[:, D_MODEL:] * _dot(omb_ref[...], wmb_ref[...])
    x1 = x_ref[...] + _dot(h.astype(BF16), wo_ref[...])

    q = (_dot(_rms(x1, gmem_ref[...]).astype(BF16), wq_ref[...]) * (HD_MEM ** -0.5)).astype(BF16)
    heads = []
    if rows_per_mem is not None:
        shape = (x1.shape[0], mk_ref.shape[1])
        same_mem = (lax.broadcasted_iota(I32, shape, 0) // rows_per_mem
                    == lax.broadcasted_iota(I32, shape, 1) // keys_per_mem)
    for hh in range(H_MEM):
        sl = slice(hh * HD_MEM, (hh + 1) * HD_MEM)
        s = _dot_nt(q[:, sl], mk_ref[0, :, sl])
        if rows_per_mem is not None:
            s = jnp.where(same_mem, s, NEG_INF)
        p = jnp.exp(s - jnp.max(s, axis=1, keepdims=True))
        p = p / jnp.sum(p, axis=1, keepdims=True)
        heads.append(_dot(p.astype(BF16), mv_ref[0, :, sl]))
    o = jnp.concatenate(heads, axis=1)
    x2 = x1 + _dot(o.astype(BF16), wom_ref[...])
    x2_ref[...] = x2

    xn = _rms(x2, gffn_ref[...])
    xn16 = xn.astype(BF16)

    lane = lax.broadcasted_iota(I32, (1, LANES), 1)
    x_hi, x_lo = _split_bf16(xn)
    both = _dot(x_hi, wr_hilo_ref[...])
    logits = both[:, :LANES] + both[:, LANES:] + _dot(x_lo, wr_hi_ref[...]) + br_ref[...]
    is_group = jnp.logical_and(lane >= N_EXPERTS, lane < N_EXPERTS + N_GROUPS)
    gl = jnp.where(is_group, logits, NEG_INF)
    g_max, g_lane = _first_argmax(gl, lane, LANES)
    g_idx = g_lane - N_EXPERTS
    g_w = 1.0 / jnp.sum(jnp.exp(gl - g_max), axis=1, keepdims=True)
    el = jnp.where((lane // EXPERTS_PER_GROUP) == g_idx, logits, NEG_INF)
    e_max, i1 = _first_argmax(el, lane, LANES)
    e_sum = jnp.sum(jnp.exp(el - e_max), axis=1, keepdims=True)
    el2 = jnp.where(lane == i1, NEG_INF, el)
    e_max2, i2 = _first_argmax(el2, lane, LANES)
    w1 = 1.0 / e_sum
    w2 = jnp.exp(e_max2 - e_max) / e_sum
    norm = w1 + w2
    comb = jnp.where(lane == i1, g_w * (w1 / norm), 0.0) + jnp.where(lane == i2, g_w * (w2 / norm), 0.0)
    xn_ref[...] = xn16
    comb_ref[...] = jnp.where(lane == GROUP_ID_LANE, g_idx.astype(F32), comb)
    count_ref[0] = jnp.sum((lane == g_idx).astype(F32), axis=0, keepdims=True)


GROUP_ID_LANE = N_EXPERTS
GROUP_RUN_ALIGN = 16
SORT_ROWS_PAD = 128
EXPERT_TILE = 512
WINDOW_ROWS = 256
EXPERTS_PER_STEP = 8


def _run_starts(dst_ref, k):
    starts = [jnp.int32(0)]
    for g in range(N_GROUPS):
        starts.append(starts[-1] + (dst_ref[k + 1, g] - dst_ref[k, g]))
    return starts


def _scatter_kernel(dst_ref, xn_ref, comb_ref, xs_zero, cs_zero, pos_ref, xs_hbm, cs_hbm, local_x, local_c, sem):
    del xs_zero, cs_zero
    k = pl.program_id(0)
    last = pl.num_programs(0) - 1
    tm = xn_ref.shape[0]
    sorted_rows = tm + SORT_ROWS_PAD
    lane = lax.broadcasted_iota(I32, (1, LANES), 1)
    routed = comb_ref[...]
    g_idx = routed[:, GROUP_ID_LANE:GROUP_ID_LANE + 1].astype(I32)
    comb = jnp.where(lane < N_EXPERTS, routed, 0.0)

    def windows(tile, act):
        starts = _run_starts(dst_ref, tile)
        for g in range(N_GROUPS):
            for half in range(EXPERT_TILE // WINDOW_ROWS):
                def go(g=g, half=half):
                    src = pl.ds(pl.multiple_of(starts[g] + half * WINDOW_ROWS, GROUP_RUN_ALIGN), WINDOW_ROWS)
                    dst = pl.ds(pl.multiple_of(dst_ref[tile, g] + half * WINDOW_ROWS, GROUP_RUN_ALIGN),
                                WINDOW_ROWS)
                    for c in (pltpu.make_async_copy(local_x.at[src], xs_hbm.at[dst], sem.at[0, g, half]),
                              pltpu.make_async_copy(local_c.at[src], cs_hbm.at[dst], sem.at[1, g, half])):
                        getattr(c, act)()
                if half == 0:
                    go()
                else:
                    pl.when(starts[g + 1] - starts[g] > half * WINDOW_ROWS)(go)

    @pl.when(k == 0)
    def _():
        local_x[...] = jnp.zeros_like(local_x)
        local_c[...] = jnp.zeros_like(local_c)

    onehot = (lane == g_idx).astype(BF16)
    earlier = (lax.broadcasted_iota(I32, (tm, tm), 0) > lax.broadcasted_iota(I32, (tm, tm), 1)).astype(BF16)
    before = _dot(earlier, onehot)
    rank = jnp.sum(jnp.where(lane == g_idx, before, 0.0), axis=1, keepdims=True).astype(I32)
    run_start = _run_starts(dst_ref, k)
    pos = rank
    for g in range(N_GROUPS):
        pos = pos + jnp.where(g_idx == g, run_start[g], 0)
    pos_ref[...] = jnp.broadcast_to(pos, pos_ref.shape)

    place_t = (lax.broadcasted_iota(I32, (tm, sorted_rows), 1) == pos).astype(F32)
    place = place_t.T.astype(BF16)
    sorted_x = _dot(place, xn_ref[...]).astype(BF16)
    own = jnp.zeros_like(comb)
    for g in range(N_GROUPS):
        shifted = comb if g == 0 else pltpu.roll(comb, LANES - g * EXPERTS_PER_GROUP, 1)
        own = jnp.where(g_idx == g, shifted, own)
    c_hi = own.astype(BF16)
    c_mid, c_lo = _split_bf16(own - c_hi.astype(F32))
    sorted_c = _dot(place, c_hi) + (_dot(place, c_mid) + _dot(place, c_lo))

    @pl.when(k > 0)
    def _():
        windows(k - 1, "wait")

    local_x[0:sorted_rows, :] = sorted_x
    local_c[0:sorted_rows, :] = sorted_c
    windows(k, "start")

    @pl.when(k == last)
    def _():
        windows(k, "wait")


def _group_layout(counts, n, nt):
    rows = (counts + (GROUP_RUN_ALIGN - 1)) // GROUP_RUN_ALIGN * GROUP_RUN_ALIGN
    rel = jnp.concatenate([jnp.zeros((1, N_GROUPS), I32), jnp.cumsum(rows, axis=0)], axis=0)
    tiles = (rel[nt] + WINDOW_ROWS + EXPERT_TILE - 1) // EXPERT_TILE
    ends = jnp.cumsum(tiles)
    dst = (rel + ((ends - tiles) * EXPERT_TILE)[None, :]).astype(I32)
    worst_rows = n + nt * N_GROUPS * (GROUP_RUN_ALIGN - 1) + N_GROUPS * (WINDOW_ROWS + EXPERT_TILE - 1)
    n_steps = -(-worst_rows // EXPERT_TILE)
    step = jnp.arange(n_steps, dtype=I32)
    group = jnp.sum(step[:, None] >= ends[None, :], axis=1).astype(I32)
    live = (group < N_GROUPS).astype(I32)
    return dst, n_steps * EXPERT_TILE, jnp.minimum(group, N_GROUPS - 1), live


def _scatter(dst, xn16, comb, total_rows):
    n = xn16.shape[0]
    tm = EXPERT_TILE
    row = lambda i, dst: (i, 0)
    any_spec = pl.BlockSpec(memory_space=pl.ANY)
    local_rows = tm + SORT_ROWS_PAD + tm
    grid_spec = pltpu.PrefetchScalarGridSpec(
        num_scalar_prefetch=1, grid=(n // tm,),
        in_specs=[pl.BlockSpec((tm, D_MODEL), row), pl.BlockSpec((tm, LANES), row), any_spec, any_spec],
        out_specs=[pl.BlockSpec((tm, LANES), row), any_spec, any_spec],
        scratch_shapes=[pltpu.VMEM((local_rows, D_MODEL), BF16), pltpu.VMEM((local_rows, LANES), F32),
                        pltpu.SemaphoreType.DMA((2, N_GROUPS, EXPERT_TILE // WINDOW_ROWS))])
    return pl.pallas_call(
        _scatter_kernel,
        grid_spec=grid_spec,
        out_shape=[jax.ShapeDtypeStruct((n, LANES), I32), jax.ShapeDtypeStruct((total_rows, D_MODEL), BF16),
                   jax.ShapeDtypeStruct((total_rows, LANES), F32)],
        input_output_aliases={3: 1, 4: 2},
        compiler_params=_cparams(("arbitrary",), 32),
        name="scatter",
    )(dst, xn16, comb, jnp.zeros((total_rows, D_MODEL), BF16), jnp.zeros((total_rows, LANES), F32))


def _post(x, osb, omb, gate, mem_k16, mem_v16, rows_per_batch, tm, w, rows_per_mem=None, keys_per_mem=None):
    n = x.shape[0]
    nt = n // tm
    tiles_per_batch = rows_per_batch // tm
    row = lambda i: (i, 0)
    const = lambda i: (0, 0)
    mem = lambda i: (i // tiles_per_batch, 0, 0)
    n_mem = mem_k16.shape[1]
    full = lambda a: pl.BlockSpec(a.shape, const)
    out_specs = [pl.BlockSpec((tm, D_MODEL), row), pl.BlockSpec((tm, D_MODEL), row),
                 pl.BlockSpec((tm, LANES), row), pl.BlockSpec((1, 1, LANES), lambda i: (i, 0, 0))]
    out_shape = [jax.ShapeDtypeStruct((n, D_MODEL), F32), jax.ShapeDtypeStruct((n, D_MODEL), BF16),
                 jax.ShapeDtypeStruct((n, LANES), F32), jax.ShapeDtypeStruct((nt, 1, LANES), F32)]
    return pl.pallas_call(
        functools.partial(_post_kernel, rows_per_mem=rows_per_mem, keys_per_mem=keys_per_mem),
        grid=(nt,),
        in_specs=[pl.BlockSpec((tm, D_MODEL), row), pl.BlockSpec((tm, W_MIX), row), pl.BlockSpec((tm, W_MIX), row),
                  pl.BlockSpec((tm, 2 * D_MODEL), row),
                  full(w["w_out_sb"]), full(w["w_out_mb"]), full(w["w_out"]),
                  full(w["norm_mem_g"]), full(w["w_q_mem"]),
                  pl.BlockSpec((1, n_mem, W_MEM), mem), pl.BlockSpec((1, n_mem, W_MEM), mem),
                  full(w["w_o_mem"]), full(w["norm_ffn_g"]),
                  full(w["w_router_hilo"]), full(w["w_router_hi"]), full(w["b_router"])],
        out_specs=out_specs,
        out_shape=out_shape,
        compiler_params=_cparams(("parallel",), 48),
        name="post",
    )(x, osb, omb, gate, w["w_out_sb"], w["w_out_mb"], w["w_out"], w["norm_mem_g"], w["w_q_mem"],
      mem_k16, mem_v16, w["w_o_mem"], w["norm_ffn_g"],
      w["w_router_hilo"], w["w_router_hi"], w["b_router"])


def _moe_kernel(xn_ref, comb_ref, x2_ref, wg_ref, wu_ref, wd_ref, gfin_ref, y_ref, acc_ref):
    e = pl.program_id(1)

    @pl.when(e == 0)
    def _():
        acc_ref[...] = jnp.zeros_like(acc_ref)

    xn = xn_ref[...]
    hg = _dot(xn, wg_ref[0])
    hu = _dot(xn, wu_ref[0])
    lane = lax.broadcasted_iota(I32, (1, LANES), 1)
    weight = jnp.sum(jnp.where(lane == e, comb_ref[...], 0.0), axis=1, keepdims=True)
    hidden = (hg / (1.0 + jnp.exp(-hg))) * hu * weight
    acc_ref[...] += _dot(hidden.astype(BF16), wd_ref[0])

    @pl.when(e == pl.num_programs(1) - 1)
    def _():
        y_ref[...] = _rms(x2_ref[...] + acc_ref[...], gfin_ref[...])


def _moe(xn16, comb, x2, wg16, wu16, wd16, gfin, tm):
    n = xn16.shape[0]
    nt = n // tm
    row = lambda i, e: (i, 0)
    exp_w = lambda i, e: (e, 0, 0)
    return pl.pallas_call(
        _moe_kernel,
        grid=(nt, N_EXPERTS),
        in_specs=[pl.BlockSpec((tm, D_MODEL), row), pl.BlockSpec((tm, LANES), row), pl.BlockSpec((tm, D_MODEL), row),
                  pl.BlockSpec((1, D_MODEL, D_EXPERT), exp_w), pl.BlockSpec((1, D_MODEL, D_EXPERT), exp_w),
                  pl.BlockSpec((1, D_EXPERT, D_MODEL), exp_w),
                  pl.BlockSpec((1, D_MODEL), lambda i, e: (0, 0))],
        out_specs=pl.BlockSpec((tm, D_MODEL), row),
        out_shape=jax.ShapeDtypeStruct((n, D_MODEL), F32),
        scratch_shapes=[pltpu.VMEM((tm, D_MODEL), F32)],
        compiler_params=_cparams(("parallel", "arbitrary"), 48),
        name="moe",
    )(xn16, comb, x2, wg16, wu16, wd16, gfin)


def _group_experts_kernel(grp_ref, live_ref, xs_ref, cs_ref, wg_ref, wu_ref, wd_ref, ys_ref, acc_ref):
    j, e = pl.program_id(0), pl.program_id(1)

    @pl.when(live_ref[j] > 0)
    def _():
        @pl.when(e == 0)
        def _():
            acc_ref[...] = jnp.zeros_like(acc_ref)

        xs = xs_ref[...]
        cs = cs_ref[...]
        lane = lax.broadcasted_iota(I32, (1, LANES), 1)
        pairs = [(_dot(xs, wg_ref[0, j]), _dot(xs, wu_ref[0, j])) for j in range(EXPERTS_PER_STEP)]
        hidden = []
        for j, (hg, hu) in enumerate(pairs):
            weight = jnp.sum(jnp.where(lane == e * EXPERTS_PER_STEP + j, cs, 0.0), axis=1, keepdims=True)
            hidden.append(((hg / (1.0 + jnp.exp(-hg))) * hu * weight).astype(BF16))
        down = [_dot(hidden[j], wd_ref[0, j]) for j in range(EXPERTS_PER_STEP)]
        acc_ref[...] += functools.reduce(jnp.add, down)

        @pl.when(e == pl.num_programs(1) - 1)
        def _():
            ys_ref[...] = acc_ref[...]

    @pl.when(jnp.logical_and(live_ref[j] == 0, e == 0))
    def _():
        ys_ref[...] = jnp.zeros_like(ys_ref)


def _group_experts(xs, cs, tile_group, tile_live, wg16, wu16, wd16):
    n_steps = tile_group.shape[0]
    by_group = lambda a: a.reshape(N_GROUPS, EXPERTS_PER_GROUP, *a.shape[1:])
    rows = lambda j, e, grp, live: (j, 0)
    inner = EXPERTS_PER_GROUP // EXPERTS_PER_STEP
    expert = lambda j, e, grp, live: (grp[j], e * live[j] + (inner - 1) * (1 - live[j]), 0, 0)
    grid_spec = pltpu.PrefetchScalarGridSpec(
        num_scalar_prefetch=2, grid=(n_steps, inner),
        in_specs=[pl.BlockSpec((EXPERT_TILE, D_MODEL), rows), pl.BlockSpec((EXPERT_TILE, LANES), rows),
                  pl.BlockSpec((1, EXPERTS_PER_STEP, D_MODEL, D_EXPERT), expert),
                  pl.BlockSpec((1, EXPERTS_PER_STEP, D_MODEL, D_EXPERT), expert),
                  pl.BlockSpec((1, EXPERTS_PER_STEP, D_EXPERT, D_MODEL), expert)],
        out_specs=pl.BlockSpec((EXPERT_TILE, D_MODEL), rows),
        scratch_shapes=[pltpu.VMEM((EXPERT_TILE, D_MODEL), F32)])
    return pl.pallas_call(
        _group_experts_kernel,
        grid_spec=grid_spec,
        out_shape=jax.ShapeDtypeStruct((xs.shape[0], D_MODEL), F32),
        compiler_params=_cparams(("arbitrary", "arbitrary"), 48),
        name="group_experts",
    )(tile_group, tile_live, xs, cs, by_group(wg16), by_group(wu16), by_group(wd16))


def _gather_norm_kernel(dst_ref, pos_ref, x2_ref, gfin_ref, ys_hbm, y_ref, stage, local, sem):
    k = pl.program_id(0)
    tm = x2_ref.shape[0]
    sorted_rows = tm + SORT_ROWS_PAD
    slot = k % 2
    halves = range(EXPERT_TILE // WINDOW_ROWS)

    def per_window(tile, fn):
        starts = _run_starts(dst_ref, tile)
        for g in range(N_GROUPS):
            for half in halves:
                go = functools.partial(fn, g, half, starts[g])
                if half == 0:
                    go()
                else:
                    pl.when(starts[g + 1] - starts[g] > half * WINDOW_ROWS)(go)
        return starts

    def dma(tile, into, act):
        def fn(g, half, _):
            src = pl.ds(pl.multiple_of(dst_ref[tile, g] + half * WINDOW_ROWS, GROUP_RUN_ALIGN), WINDOW_ROWS)
            copy = pltpu.make_async_copy(ys_hbm.at[src], stage.at[into, g, pl.ds(half * WINDOW_ROWS, WINDOW_ROWS)],
                                         sem.at[into, g, half])
            getattr(copy, act)()
        per_window(tile, fn)

    @pl.when(k == 0)
    def _():
        local[...] = jnp.zeros_like(local)
        dma(0, 0, "start")

    @pl.when(k + 1 < pl.num_programs(0))
    def _():
        dma(k + 1, 1 - slot, "start")

    dma(k, slot, "wait")

    def place(g, half, run_start):
        rows = pl.ds(pl.multiple_of(run_start + half * WINDOW_ROWS, GROUP_RUN_ALIGN), WINDOW_ROWS)
        local[rows, :] = stage[slot, g, pl.ds(half * WINDOW_ROWS, WINDOW_ROWS), :]

    run_start = per_window(k, place)
    local[pl.ds(pl.multiple_of(run_start[N_GROUPS], GROUP_RUN_ALIGN), WINDOW_ROWS), :] = jnp.zeros(
        (WINDOW_ROWS, local.shape[1]), F32)

    pick = (lax.broadcasted_iota(I32, (tm, sorted_rows), 1) == pos_ref[:, 0:1]).astype(BF16)
    hi, lo = _split_bf16(local[0:sorted_rows, :])
    y_ref[...] = _rms(x2_ref[...] + (_dot(pick, hi) + _dot(pick, lo)), gfin_ref[...])


def _gather_norm(dst, pos, x2, gfin, ys):
    n = x2.shape[0]
    tm = EXPERT_TILE
    nt = n // tm
    row = lambda i, offs: (i, 0)
    grid_spec = pltpu.PrefetchScalarGridSpec(
        num_scalar_prefetch=1, grid=(nt,),
        in_specs=[pl.BlockSpec((tm, LANES), row), pl.BlockSpec((tm, D_MODEL), row),
                  pl.BlockSpec((1, D_MODEL), lambda i, offs: (0, 0)), pl.BlockSpec(memory_space=pl.ANY)],
        out_specs=pl.BlockSpec((tm, D_MODEL), row),
        scratch_shapes=[pltpu.VMEM((2, N_GROUPS, tm, D_MODEL), F32),
                        pltpu.VMEM((tm + SORT_ROWS_PAD + tm, D_MODEL), F32),
                        pltpu.SemaphoreType.DMA((2, N_GROUPS, EXPERT_TILE // WINDOW_ROWS))])
    return pl.pallas_call(
        _gather_norm_kernel,
        grid_spec=grid_spec,
        out_shape=jax.ShapeDtypeStruct((n, D_MODEL), F32),
        compiler_params=_cparams(("arbitrary",), 56),
        name="gather_norm",
    )(dst, pos, x2, gfin, ys)


NEW_ROWS = 8


def _new_token_page(ref):
    rows = ref[0]
    return jnp.concatenate([rows, jnp.zeros((PAGE_SIZE - rows.shape[0], rows.shape[1]), rows.dtype)],
                           axis=0).astype(BF16)


def _own_head_block(full, n_new):
    row_head = lax.broadcasted_iota(I32, (full.shape[0], 1), 0) // n_new
    out = jnp.zeros((full.shape[0], HEAD_DIM), F32)
    for h in range(N_HEADS):
        out = jnp.where(row_head == h, full[:, h * HEAD_DIM:(h + 1) * HEAD_DIM], out)
    return out


def _sb_sample_kernel(pt_ref, q_ref, kn_ref, vn_ref, kc_hbm, vc_hbm, o_ref, kbuf, vbuf, sem, *, n_new, n_pages):
    b = pl.program_id(0)
    q = q_ref[0]
    rows = q.shape[0]
    later = _later_keys(PAGE_SIZE)

    def page_copies(p, slot, row=b):
        page = pt_ref[row, p]
        return (pltpu.make_async_copy(kc_hbm.at[page], kbuf.at[slot], sem.at[0, slot]),
                pltpu.make_async_copy(vc_hbm.at[page], vbuf.at[slot], sem.at[1, slot]))

    def start(p, slot, row=b):
        for c in page_copies(p, slot, row):
            c.start()

    def wait(p, slot):
        for c in page_copies(p, slot):
            c.wait()

    slot_of = lambda p: (n_pages - 1 - p) % 2

    @pl.when(b == 0)
    def _():
        start(n_pages - 1, 0)

    key_slot = lax.broadcasted_iota(I32, (rows, PAGE_SIZE), 1)
    tok = lax.broadcasted_iota(I32, (rows, PAGE_SIZE), 0) % n_new
    a, run = _stick_breaking_tile(_dot_nt(q, _new_token_page(kn_ref)), later, jnp.zeros((rows, 1), F32),
                                  key_slot < tok)
    acc = _dot(a.astype(BF16), _new_token_page(vn_ref))

    def cond(carry):
        p, alive = carry[0], carry[1]
        return jnp.logical_and(p >= 0, alive > 0)

    def body(carry):
        p, _, run, acc = carry
        slot = slot_of(p)

        @pl.when(p > 0)
        def _():
            start(p - 1, 1 - slot)

        wait(p, slot)
        z = _dot(q, kbuf[slot].reshape(W_MIX, PAGE_SIZE).astype(BF16))
        a, run = _stick_breaking_tile(z, later, run, None)
        acc = acc + _dot_nt(a.astype(BF16), vbuf[slot].reshape(W_MIX, PAGE_SIZE).astype(BF16))
        alive = (jnp.max(run) > SB_DEAD).astype(I32)
        return (p - 1, alive, run, acc)

    p_next, _, _, acc = lax.while_loop(cond, body, (jnp.int32(n_pages - 1), jnp.int32(1), run, acc))

    @pl.when(p_next >= 0)
    def _():
        wait(p_next, slot_of(p_next))

    @pl.when(b + 1 < pl.num_programs(0))
    def _():
        start(n_pages - 1, 0, b + 1)

    o_ref[0] = _own_head_block(acc, n_new)


def _sb_sample(page_table, q_bd, kn_t, vn_t, cache_kt, cache_vt, n_new):
    nb, n_pages = page_table.shape
    rows = N_HEADS * n_new
    per_b = lambda b, pt: (b, 0, 0)
    page_buffers = pltpu.VMEM((2, N_HEADS, HEAD_DIM, PAGE_SIZE), F32)
    grid_spec = pltpu.PrefetchScalarGridSpec(
        num_scalar_prefetch=1, grid=(nb,),
        in_specs=[pl.BlockSpec((1, rows, W_MIX), per_b),
                  pl.BlockSpec((1, NEW_ROWS, W_MIX), per_b),
                  pl.BlockSpec((1, NEW_ROWS, W_MIX), per_b),
                  pl.BlockSpec(memory_space=pl.ANY),
                  pl.BlockSpec(memory_space=pl.ANY)],
        out_specs=pl.BlockSpec((1, rows, HEAD_DIM), per_b),
        scratch_shapes=[page_buffers, page_buffers, pltpu.SemaphoreType.DMA((2, 2))])
    return pl.pallas_call(
        functools.partial(_sb_sample_kernel, n_new=n_new, n_pages=n_pages),
        grid_spec=grid_spec,
        out_shape=jax.ShapeDtypeStruct((nb, rows, HEAD_DIM), F32),
        compiler_params=_cparams(("arbitrary",), 32),
        name="sb_sample",
    )(page_table, q_bd, kn_t, vn_t, cache_kt, cache_vt)


PAGES_PER_BLOCK = MOBA_BLOCK // PAGE_SIZE
MOBA_SAMPLE_BLOCKS_PER_STEP = 8


def _moba_sample_kernel(pt_ref, q_ref, kn_ref, vn_ref, *refs, n_new):
    pages_per_step = MOBA_SAMPLE_BLOCKS_PER_STEP * PAGES_PER_BLOCK
    k_refs, v_refs = refs[:pages_per_step], refs[pages_per_step:2 * pages_per_step]
    o_ref, pm_ref, pl_ref, pg_ref, po_ref = refs[2 * pages_per_step:]
    step = pl.program_id(1)
    n_blocks = pl.num_programs(1) * MOBA_SAMPLE_BLOCKS_PER_STEP
    q = q_ref[0]
    rows = q.shape[0]
    block_lane = lax.broadcasted_iota(I32, (1, LANES), 1)

    @pl.when(step == 0)
    def _():
        pm_ref[...] = jnp.full(pm_ref.shape, NEG_INF, F32)
        pl_ref[...] = jnp.zeros_like(pl_ref)
        pg_ref[...] = jnp.zeros_like(pg_ref)

    def block_pages(page_refs, j):
        pages = [r[0].reshape(W_MIX, PAGE_SIZE).astype(BF16)
                 for r in page_refs[j * PAGES_PER_BLOCK:(j + 1) * PAGES_PER_BLOCK]]
        return jnp.concatenate(pages, axis=1)

    pm, pl_, pg = pm_ref[...], pl_ref[...], pg_ref[...]
    blocks = range(MOBA_SAMPLE_BLOCKS_PER_STEP)
    scores = [_dot(q, block_pages(k_refs, j)) for j in blocks]
    maxes = [jnp.max(z, axis=1, keepdims=True) for z in scores]
    weights = [jnp.exp2(z - m) for z, m in zip(scores, maxes)]
    for j in blocks:
        blk = step * MOBA_SAMPLE_BLOCKS_PER_STEP + j
        here = block_lane == blk
        pm = jnp.where(here, maxes[j], pm)
        pl_ = jnp.where(here, jnp.sum(weights[j], axis=1, keepdims=True), pl_)
        pg = jnp.where(here, jnp.sum(scores[j], axis=1, keepdims=True), pg)
        po_ref[blk] = _dot_nt(weights[j].astype(BF16), block_pages(v_refs, j))
    pm_ref[...], pl_ref[...], pg_ref[...] = pm, pl_, pg

    @pl.when(step == pl.num_programs(1) - 1)
    def _():
        chosen = _top_blocks(pg, block_lane < n_blocks, block_lane)
        slot = lax.broadcasted_iota(I32, (rows, PAGE_SIZE), 1)
        tok = lax.broadcasted_iota(I32, (rows, PAGE_SIZE), 0) % n_new
        zn = jnp.where(slot <= tok, _dot_nt(q, _new_token_page(kn_ref)), NEG_INF)
        m_all = jnp.maximum(jnp.max(jnp.where(chosen, pm, NEG_INF), axis=1, keepdims=True),
                            jnp.max(zn, axis=1, keepdims=True))
        pn = jnp.exp2(zn - m_all)
        w = jnp.where(chosen, jnp.exp2(pm - m_all), 0.0)
        total = jnp.sum(w * pl_, axis=1, keepdims=True) + jnp.sum(pn, axis=1, keepdims=True)
        acc = _dot(pn.astype(BF16), _new_token_page(vn_ref))
        for n in range(po_ref.shape[0]):
            acc = acc + w[:, n:n + 1] * po_ref[n]
        o_ref[0] = _own_head_block(acc, n_new) / total


def _moba_sample(page_table, q_bd, kn_t, vn_t, cache_kt, cache_vt, n_new):
    nb, n_pages = page_table.shape
    rows = N_HEADS * n_new
    pages_per_step = MOBA_SAMPLE_BLOCKS_PER_STEP * PAGES_PER_BLOCK
    n_steps = n_pages // pages_per_step
    per_b = lambda b, s, pt: (b, 0, 0)
    page_spec = lambda j: pl.BlockSpec((1, N_HEADS, HEAD_DIM, PAGE_SIZE),
                                       lambda b, s, pt: (pt[b, s * pages_per_step + j], 0, 0, 0))
    page_specs = [page_spec(j) for j in range(pages_per_step)]
    grid_spec = pltpu.PrefetchScalarGridSpec(
        num_scalar_prefetch=1, grid=(nb, n_steps),
        in_specs=[pl.BlockSpec((1, rows, W_MIX), per_b),
                  pl.BlockSpec((1, NEW_ROWS, W_MIX), per_b),
                  pl.BlockSpec((1, NEW_ROWS, W_MIX), per_b)] + page_specs + page_specs,
        out_specs=pl.BlockSpec((1, rows, HEAD_DIM), per_b),
        scratch_shapes=[pltpu.VMEM((rows, LANES), F32), pltpu.VMEM((rows, LANES), F32),
                        pltpu.VMEM((rows, LANES), F32),
                        pltpu.VMEM((n_pages // PAGES_PER_BLOCK, rows, W_MIX), F32)])
    return pl.pallas_call(
        functools.partial(_moba_sample_kernel, n_new=n_new),
        grid_spec=grid_spec,
        out_shape=jax.ShapeDtypeStruct((nb, rows, HEAD_DIM), F32),
        compiler_params=_cparams(("parallel", "arbitrary"), 32),
        name="moba_sample",
    )(page_table, q_bd, kn_t, vn_t, *([cache_kt] * pages_per_step), *([cache_vt] * pages_per_step))


def _rope_tables(pos):
    half = HEAD_DIM // 2
    inv_freq = ROPE_THETA ** (-jnp.arange(half, dtype=F32) / half)
    ang = pos.astype(F32)[:, None] * inv_freq[None, :]
    cos = jnp.cos(ang)
    sin = jnp.sin(ang)
    heads_per_tile = LANES // HEAD_DIM
    return (jnp.tile(jnp.concatenate([cos, cos], axis=1), (1, heads_per_tile)),
            jnp.tile(jnp.concatenate([-sin, sin], axis=1), (1, heads_per_tile)))


def _pad_lanes(a):
    return jnp.pad(a, ((0, 0), (0, LANES - a.shape[1])))


def _prepare_weights(norm_mix_g, w_in, b_gate, w_out_sb, w_out_mb, w_out, norm_mem_g, norm_memsrc_g, w_q_mem,
                     w_kv_mem, w_o_mem, norm_ffn_g, w_router_group, b_router_group, w_router_expert,
                     b_router_expert, w_gate_e, w_up_e, w_down_e, norm_final_g):
    row = lambda v: v.reshape(1, -1).astype(F32)
    w_re = w_router_expert.transpose(1, 0, 2).reshape(D_MODEL, N_EXPERTS)
    wr_hi, wr_lo = _split_bf16(_pad_lanes(jnp.concatenate([w_re, w_router_group], axis=1)))
    b_router = _pad_lanes(jnp.concatenate([row(b_router_expert), row(b_router_group)], axis=1))
    return dict(
        norm_mix_g=row(norm_mix_g), w_in=w_in.astype(BF16), b_gate=row(b_gate),
        w_out_sb=w_out_sb.astype(BF16), w_out_mb=w_out_mb.astype(BF16), w_out=w_out.astype(BF16),
        norm_mem_g=row(norm_mem_g), norm_memsrc_g=row(norm_memsrc_g), w_q_mem=w_q_mem.astype(BF16),
        w_kv_mem=w_kv_mem.astype(BF16), w_o_mem=w_o_mem.astype(BF16), norm_ffn_g=row(norm_ffn_g),
        w_router_hilo=jnp.concatenate([wr_hi, wr_lo], axis=1), w_router_hi=wr_hi, b_router=b_router,
        w_gate_e=w_gate_e.astype(BF16), w_up_e=w_up_e.astype(BF16), w_down_e=w_down_e.astype(BF16),
        norm_final_g=row(norm_final_g))


def _tail(x, osb, omb, gate, mem_k16, mem_v16, rows_per_batch, w, tm_post, tm_moe, **mem_mask):
    x2, xn16, comb, _ = _post(x, osb, omb, gate, mem_k16, mem_v16, rows_per_batch, tm_post, w, **mem_mask)
    return _moe(xn16, comb, x2, w["w_gate_e"], w["w_up_e"], w["w_down_e"], w["norm_final_g"], tm_moe)


def kernel(x_prompt, x_sample, mem_prompt, cache_sb_k, cache_sb_v, cache_mb_k, cache_mb_v, cache_mem_k, cache_mem_v, page_table, norm_mix_g, w_in, b_gate, w_out_sb, w_out_mb, w_out, norm_mem_g, norm_memsrc_g, w_q_mem, w_kv_mem, w_o_mem, norm_ffn_g, w_router_group, b_router_group, w_router_expert, b_router_expert, w_gate_e, w_up_e, w_down_e, norm_final_g):
    w = _prepare_weights(norm_mix_g, w_in, b_gate, w_out_sb, w_out_mb, w_out, norm_mem_g, norm_memsrc_g, w_q_mem,
                         w_kv_mem, w_o_mem, norm_ffn_g, w_router_group, b_router_group, w_router_expert,
                         b_router_expert, w_gate_e, w_up_e, w_down_e, norm_final_g)
    batch, seq, _ = x_prompt.shape
    dec_batch, n_new, _ = x_sample.shape
    n_mem = mem_prompt.shape[1]
    n_pages = page_table.shape[1]
    past_len = n_pages * PAGE_SIZE
    assert seq % (MOBA_BLOCK * MOBA_BLOCKS_PER_ITER) == 0 and seq % (MOBA_BLOCK * MOBA_TILES_PER_STEP) == 0
    assert seq % (SB_TILE * SB_TILES_PER_STEP) == 0 and (batch * seq) % EXPERT_TILE == 0
    assert n_new <= PAGE_SIZE and n_pages // PAGES_PER_BLOCK <= LANES
    assert n_pages % (MOBA_SAMPLE_BLOCKS_PER_STEP * PAGES_PER_BLOCK) == 0
    heads = lambda t, b, s: t.reshape(b, s, N_HEADS, HEAD_DIM)

    xp = x_prompt.reshape(batch * seq, D_MODEL)
    cos_p, sin_p = _rope_tables(jnp.arange(seq, dtype=I32))
    (qsb, qmb, gate, ksb_t, vsb_t, kmb_t, vmb_t, ksb16, vsb16_t, kmb16, vmb16_t, kmean) = _proj(
        xp, w["norm_mix_g"], w["w_in"], w["b_gate"], cos_p, sin_p, PROJ_TILE, rows_per_batch=seq)
    osb = _sb_prompt(qsb, ksb16, vsb16_t, batch, seq, SB_TILE)
    omb = _moba_prompt(qmb, kmb16, vmb16_t, kmean.reshape(batch * seq // MOBA_BLOCK, W_MIX), batch, seq)
    mem_k, mem_v = _mem_kv(mem_prompt.reshape(batch * n_mem, D_MODEL), w["norm_memsrc_g"], w["w_kv_mem"])
    x2, xn16, comb, counts = _post(xp, osb, omb, gate, mem_k.astype(BF16).reshape(batch, n_mem, W_MEM),
                                   mem_v.astype(BF16).reshape(batch, n_mem, W_MEM), seq, EXPERT_TILE, w)
    n_tiles = batch * seq // EXPERT_TILE
    dst, total_rows, tile_group, tile_live = _group_layout(
        counts[:, 0, :N_GROUPS].astype(I32), batch * seq, n_tiles)
    pos, rows_by_group, weights_by_group = _scatter(dst, xn16, comb, total_rows)
    expert_out = _group_experts(rows_by_group, weights_by_group, tile_group, tile_live,
                                w["w_gate_e"], w["w_up_e"], w["w_down_e"])
    y_prompt = _gather_norm(dst, pos, x2, w["norm_final_g"], expert_out)

    rows_s = dec_batch * n_new
    xs = x_sample.reshape(rows_s, D_MODEL)
    cos_s, sin_s = _rope_tables(past_len + (jnp.arange(rows_s, dtype=I32) % n_new))
    (qsb_s, qmb_s, gate_s, ksb_s, vsb_s, kmb_s, vmb_s) = _proj(
        xs, w["norm_mix_g"], w["w_in"], w["b_gate"], cos_s, sin_s, rows_s)

    def block_diagonal(q16):
        q_cols = heads(q16, dec_batch, n_new).transpose(0, 2, 1, 3).reshape(dec_batch, N_HEADS * n_new, HEAD_DIM)
        own = (jnp.arange(N_HEADS * n_new)[:, None] // n_new) == (jnp.arange(W_MIX)[None, :] // HEAD_DIM)
        return jnp.where(own[None], jnp.tile(q_cols, (1, 1, N_HEADS)), jnp.zeros((), q16.dtype))

    def new_page(t):
        return jnp.pad(t.reshape(dec_batch, n_new, W_MIX), ((0, 0), (0, NEW_ROWS - n_new), (0, 0)))

    def token_rows(o):
        o = o.reshape(dec_batch, N_HEADS, n_new, HEAD_DIM).transpose(0, 2, 1, 3)
        return o.reshape(rows_s, W_MIX).astype(BF16)

    pages = lambda c: c.transpose(0, 2, 3, 1)
    osb_s = token_rows(_sb_sample(page_table, block_diagonal(qsb_s), new_page(ksb_s), new_page(vsb_s),
                                  pages(cache_sb_k), pages(cache_sb_v), n_new))
    omb_s = token_rows(_moba_sample(page_table, block_diagonal(qmb_s), new_page(kmb_s), new_page(vmb_s),
                                    pages(cache_mb_k), pages(cache_mb_v), n_new))
    y_sample = _tail(xs, osb_s, omb_s, gate_s,
                     cache_mem_k.reshape(1, dec_batch * n_mem, W_MEM).astype(BF16),
                     cache_mem_v.reshape(1, dec_batch * n_mem, W_MEM).astype(BF16),
                     rows_s, w, rows_s, rows_s, rows_per_mem=n_new, keys_per_mem=n_mem)

    mem_heads = lambda t: t.reshape(batch, n_mem, H_MEM, HD_MEM)
    from_head_major = lambda t: t.reshape(batch, N_HEADS, HEAD_DIM, seq).transpose(0, 3, 1, 2)
    return (y_prompt.reshape(batch, seq, D_MODEL), y_sample.reshape(dec_batch, n_new, D_MODEL),
            from_head_major(ksb_t), from_head_major(vsb_t), from_head_major(kmb_t), from_head_major(vmb_t),
            mem_heads(mem_k), mem_heads(mem_v),
            heads(ksb_s, dec_batch, n_new), heads(vsb_s, dec_batch, n_new),
            heads(kmb_s, dec_batch, n_new), heads(vmb_s, dec_batch, n_new))
```

```python
import functools

import jax
import jax.numpy as jnp
from jax import lax
from jax.experimental import pallas as pl
from jax.experimental.pallas import tpu as pltpu

F32 = jnp.float32
BF16 = jnp.bfloat16
I32 = jnp.int32

D_MODEL = 1024
N_HEADS = 8
HEAD_DIM = 64
W_MIX = N_HEADS * HEAD_DIM
PAGE_SIZE = 128
MOBA_BLOCK = 256
MOBA_TOPK = 3
H_MEM = 4
HD_MEM = 128
W_MEM = H_MEM * HD_MEM
N_GROUPS = 4
EXPERTS_PER_GROUP = 8
N_EXPERTS = N_GROUPS * EXPERTS_PER_GROUP
D_EXPERT = 256
ROPE_THETA = 10000.0
RMS_EPS = 1e-6
W_IN_COLS = 6 * W_MIX + 2 * D_MODEL

LOG2_E = 1.4426950408889634
LANES = 128
BF16_ROWS = 16
HEADS_PER_STEP = 4
QUAD = HEADS_PER_STEP * HEAD_DIM
SB_DEAD = -160.0 * LOG2_E
PROJ_TILE = 512
MOBA_TILES_PER_STEP = 4
MOBA_BLOCKS_PER_ITER = 2
SB_TILE = 128
SB_TILES_PER_STEP = 8
NEG_INF = float("-inf")
MIB = 1024 * 1024


def _cparams(semantics, vmem_mib):
    return pltpu.CompilerParams(dimension_semantics=semantics, vmem_limit_bytes=vmem_mib * MIB)


def _rms(x, g):
    ms = jnp.mean(x * x, axis=-1, keepdims=True)
    return (x * lax.rsqrt(ms + RMS_EPS)) * g


def _dot(a, b):
    return jnp.dot(a, b, preferred_element_type=F32)


def _dot_nt(a, b):
    return lax.dot_general(a, b, (((1,), (1,)), ((), ())), preferred_element_type=F32)


def _split_bf16(x):
    hi = x.astype(BF16)
    lo = (x - hi.astype(F32)).astype(BF16)
    return hi, lo


def _later_keys(n):
    return (lax.broadcasted_iota(I32, (n, n), 0) > lax.broadcasted_iota(I32, (n, n), 1)).astype(BF16)


def _stick_breaking_tile(z, later, carried, valid):
    l1p = jnp.log2(1.0 + jnp.exp2(-jnp.abs(z)))
    log_keep = -(jnp.maximum(z, 0.0) + l1p)
    if valid is not None:
        log_keep = jnp.where(valid, log_keep, 0.0)
    hi, lo = _split_bf16(log_keep)
    both = _dot(jnp.concatenate([hi, lo], axis=0), later)
    inner = both[:z.shape[0]] + both[z.shape[0]:]
    a = jnp.exp2(jnp.minimum(z, 0.0) - l1p + carried + inner)
    if valid is not None:
        a = jnp.where(valid, a, 0.0)
    return a, carried + inner[:, 0:1] + log_keep[:, 0:1]


def _proj_kernel(x_ref, g_ref, w_ref, bg_ref, cos_ref, sin_ref, qsb_ref, qmb_ref, gate_ref, *kv_refs, head_major):
    xb = _rms(x_ref[...], g_ref[...]).astype(BF16)

    def seg(lo, width):
        return _dot(xb, w_ref[:, lo:lo + width])

    lane = lax.broadcasted_iota(I32, (1, W_MIX), 1)
    first_half = (lane % HEAD_DIM) < (HEAD_DIM // 2)
    cos = jnp.concatenate([cos_ref[...]] * (W_MIX // LANES), axis=1)
    sin = jnp.concatenate([sin_ref[...]] * (W_MIX // LANES), axis=1)

    def rope(t):
        partner = jnp.where(first_half, pltpu.roll(t, W_MIX - HEAD_DIM // 2, 1),
                            pltpu.roll(t, HEAD_DIM // 2, 1))
        return t * cos + partner * sin

    scale = HEAD_DIM ** -0.5 * LOG2_E
    qsb_ref[...] = (seg(0, W_MIX) * scale).astype(BF16)
    qmb_ref[...] = (rope(seg(3 * W_MIX, W_MIX)) * scale).astype(BF16)
    gl = seg(6 * W_MIX, 2 * D_MODEL) + bg_ref[...]
    gate_ref[...] = 1.0 / (1.0 + jnp.exp(-gl))
    ksb = seg(W_MIX, W_MIX)
    vsb = seg(2 * W_MIX, W_MIX)
    kmb = rope(seg(4 * W_MIX, W_MIX))
    vmb = seg(5 * W_MIX, W_MIX)
    if not head_major:
        for ref, t in zip(kv_refs, (ksb, vsb, kmb, vmb)):
            ref[...] = t
        return
    ksb_t_ref, vsb_t_ref, kmb_t_ref, vmb_t_ref, ksb16_ref, vsb16_t_ref, kmb16_ref, vmb16_t_ref, kmean_ref = kv_refs
    ksb_t_ref[0] = ksb.T
    vsb_t = vsb.T
    vsb_t_ref[0] = vsb_t
    kmb_t_ref[0] = kmb.T
    vmb_t = vmb.T
    vmb_t_ref[0] = vmb_t
    ksb16_ref[...] = ksb.astype(BF16)
    vsb16_t_ref[0] = vsb_t.astype(BF16)
    kmb16_ref[...] = kmb.astype(BF16)
    vmb16_t_ref[0] = vmb_t.astype(BF16)
    for j in range(kmean_ref.shape[0]):
        kmean_ref[j] = jnp.mean(kmb[j * MOBA_BLOCK:(j + 1) * MOBA_BLOCK], axis=0, keepdims=True)


def _proj(x, g, w_in16, b_gate, cos, sin, tm, rows_per_batch=None):
    n = x.shape[0]
    nt = n // tm
    n_rope = cos.shape[0] // tm
    row = lambda i: (i, 0)
    const = lambda i: (0, 0)
    head_major = rows_per_batch is not None
    out_shape = [jax.ShapeDtypeStruct((n, W_MIX), BF16)] * 2 + [jax.ShapeDtypeStruct((n, 2 * D_MODEL), F32)]
    out_specs = [pl.BlockSpec((tm, W_MIX), row)] * 2 + [pl.BlockSpec((tm, 2 * D_MODEL), row)]
    if head_major:
        tiles = rows_per_batch // tm
        n_batch = n // rows_per_batch
        t_spec = pl.BlockSpec((1, W_MIX, tm), lambda i: (i // tiles, 0, i % tiles))
        rows16 = jax.ShapeDtypeStruct((n, W_MIX), BF16)
        t16 = jax.ShapeDtypeStruct((n_batch, W_MIX, rows_per_batch), BF16)
        out_shape += ([jax.ShapeDtypeStruct((n_batch, W_MIX, rows_per_batch), F32)] * 4
                      + [rows16, t16, rows16, t16, jax.ShapeDtypeStruct((n // MOBA_BLOCK, 1, W_MIX), F32)])
        r_spec = pl.BlockSpec((tm, W_MIX), row)
        out_specs += ([t_spec] * 4 + [r_spec, t_spec, r_spec, t_spec,
                                      pl.BlockSpec((tm // MOBA_BLOCK, 1, W_MIX), lambda i: (i, 0, 0))])
    else:
        out_shape += [jax.ShapeDtypeStruct((n, W_MIX), F32)] * 4
        out_specs += [pl.BlockSpec((tm, W_MIX), row)] * 4
    return pl.pallas_call(
        functools.partial(_proj_kernel, head_major=head_major),
        grid=(nt,),
        in_specs=[pl.BlockSpec((tm, D_MODEL), row),
                  pl.BlockSpec((1, D_MODEL), const),
                  pl.BlockSpec((D_MODEL, W_IN_COLS), const),
                  pl.BlockSpec((1, 2 * D_MODEL), const),
                  pl.BlockSpec((tm, LANES), lambda i: (i % n_rope, 0)),
                  pl.BlockSpec((tm, LANES), lambda i: (i % n_rope, 0))],
        out_specs=out_specs,
        out_shape=out_shape,
        compiler_params=_cparams(("parallel",), 48),
        name="proj",
    )(x, g, w_in16, b_gate, cos, sin)


def _head_masks(width):
    lane_head = lax.broadcasted_iota(I32, (1, width), 1) // HEAD_DIM
    return [lane_head == h for h in range(HEADS_PER_STEP)]


def _sb_prompt_kernel(q_ref, k_ref, vt_ref, o_ref, acc_ref, *, tq):
    i = pl.program_id(2)
    masks = _head_masks(QUAD)
    tiles = range(SB_TILES_PER_STEP)
    chains = [(t, h) for t in tiles for h in range(HEADS_PER_STEP)]
    q = [q_ref[t * tq:(t + 1) * tq, :] for t in tiles]
    qs = {(t, h): jnp.where(masks[h], q[t], jnp.zeros_like(q[t])) for t, h in chains}
    key = lax.broadcasted_iota(I32, (tq, tq), 0)
    qry = lax.broadcasted_iota(I32, (tq, tq), 1)
    after = (qry > key).astype(BF16)

    def block(step, dead, diagonal):
        kblk, vtblk = [], []
        for t in tiles:
            kb = i * SB_TILES_PER_STEP + t - step
            if not diagonal:
                dead = {c: (jnp.where(kb < 0, jnp.inf, d) if c[0] == t else d) for c, d in dead.items()}
                kb = jnp.maximum(kb, 0)
            start = pl.multiple_of(kb * tq, tq)
            kblk.append(k_ref[pl.ds(start, tq), :])
            vtblk.append(vt_ref[0, :, pl.ds(start, tq)])
        scores = {c: _dot_nt(kblk[c[0]], qs[c]) for c in chains}
        softplus, parts = {}, {}
        for c in chains:
            z = scores[c]
            sp = jnp.maximum(z, 0.0) + jnp.log2(1.0 + jnp.exp2(-jnp.abs(z)))
            if diagonal:
                sp = jnp.where(key < qry, sp, 0.0)
            softplus[c] = sp
            parts[c] = _split_bf16(sp)
        both = {c: _dot(after, jnp.concatenate(parts[c], axis=1)) for c in chains}
        inner = {c: both[c][:, :tq] + both[c][:, tq:] for c in chains}
        new_dead = {}
        for c in chains:
            t, h = c
            a = jnp.exp2(scores[c] - softplus[c] - (dead[c] + inner[c]))
            if diagonal:
                a = jnp.where(key < qry, a, 0.0)
            rows, cols = slice(h * HEAD_DIM, (h + 1) * HEAD_DIM), slice(t * tq, (t + 1) * tq)
            acc_ref[rows, cols] += _dot(vtblk[t][rows, :], a.astype(BF16))
            new_dead[c] = dead[c] + inner[c][0:1] + softplus[c][0:1]
        return new_dead

    def alive(dead):
        low = functools.reduce(jnp.minimum, [dead[c] for c in chains])
        return (jnp.min(low) < -SB_DEAD).astype(I32)

    acc_ref[...] = jnp.zeros_like(acc_ref)
    dead = block(0, {c: jnp.zeros((1, tq), F32) for c in chains}, True)
    newest = i * SB_TILES_PER_STEP + SB_TILES_PER_STEP - 1

    def cond(carry):
        return jnp.logical_and(carry[0] <= newest, carry[1] > 0)

    def body(carry):
        dead = block(carry[0], dict(zip(chains, carry[2:])), False)
        return (carry[0] + 1, alive(dead)) + tuple(dead[c] for c in chains)

    lax.while_loop(cond, body, (jnp.int32(1), alive(dead)) + tuple(dead[c] for c in chains))
    o_ref[...] = acc_ref[...].T.astype(o_ref.dtype)


def _sb_prompt(q16, k16, vt16, batch, seq, tq):
    rows = tq * SB_TILES_PER_STEP
    nq = seq // rows
    ng = W_MIX // QUAD
    return pl.pallas_call(
        functools.partial(_sb_prompt_kernel, tq=tq),
        grid=(batch, ng, nq),
        in_specs=[pl.BlockSpec((rows, QUAD), lambda b, g, i: (b * nq + i, g)),
                  pl.BlockSpec((seq, QUAD), lambda b, g, i: (b, g)),
                  pl.BlockSpec((1, QUAD, seq), lambda b, g, i: (b, g, 0))],
        out_specs=pl.BlockSpec((rows, QUAD), lambda b, g, i: (b * nq + i, g)),
        out_shape=jax.ShapeDtypeStruct((batch * seq, W_MIX), BF16),
        scratch_shapes=[pltpu.VMEM((QUAD, rows), F32)],
        compiler_params=_cparams(("parallel", "parallel", "arbitrary"), 48),
        name="sb_prompt",
    )(q16, k16, vt16)


def _first_argmax(x, lane, width):
    mx = jnp.max(x, axis=1, keepdims=True)
    idx = jnp.min(jnp.where(x == mx, lane, width), axis=1, keepdims=True)
    return mx, idx


def _top_blocks(gate, eligible, blk):
    g = jnp.where(eligible, gate, NEG_INF)
    sel = jnp.zeros(gate.shape, jnp.bool_)
    for _ in range(MOBA_TOPK):
        _, first = _first_argmax(g, blk, gate.shape[1])
        pick = blk == first
        sel = jnp.logical_or(sel, jnp.logical_and(pick, eligible))
        g = jnp.where(pick, NEG_INF, g)
    return sel


def _top_block_rows(gate, eligible, blk):
    n = gate.shape[0]
    g = jnp.where(eligible, gate, NEG_INF)
    sel = jnp.zeros(gate.shape, jnp.bool_)
    for _ in range(MOBA_TOPK):
        mx = jnp.max(g, axis=0, keepdims=True)
        first = jnp.min(jnp.where(g == mx, blk, n), axis=0, keepdims=True)
        pick = blk == first
        sel = jnp.logical_or(sel, jnp.logical_and(pick, eligible))
        g = jnp.where(pick, NEG_INF, g)
    return sel


def _moba_prompt_kernel(q_ref, k_ref, vt_ref, km_ref, o_ref, sel_ref, acc_ref, *, tq):
    first_tile = pl.program_id(2) * MOBA_TILES_PER_STEP
    tiles = range(MOBA_TILES_PER_STEP)
    chains = [(t, h) for t in tiles for h in range(HEADS_PER_STEP)]
    masks = _head_masks(QUAD)
    q = [q_ref[t * tq:(t + 1) * tq, :] for t in tiles]
    qs = {(t, h): jnp.where(masks[h], q[t], jnp.zeros_like(q[t])) for t, h in chains}
    nb = km_ref.shape[0]
    km_hi, km_lo = _split_bf16(km_ref[...])
    blk = lax.broadcasted_iota(I32, (nb, 1), 0)
    for n, (t, h) in enumerate(chains):
        gate = _dot_nt(km_hi, qs[t, h]) + _dot_nt(km_lo, qs[t, h])
        sel_ref[n] = _top_block_rows(gate, blk < first_tile + t, blk).astype(F32)
    key = lax.broadcasted_iota(I32, (tq, tq), 0)
    qry = lax.broadcasted_iota(I32, (tq, tq), 1)

    def load(kb):
        start = pl.multiple_of(kb * tq, tq)
        return k_ref[pl.ds(start, tq), :], vt_ref[0, :, pl.ds(start, tq)]

    def head_rows(h):
        return slice(h * HEAD_DIM, (h + 1) * HEAD_DIM)

    def tile_cols(t):
        return slice(t * tq, (t + 1) * tq)

    def weighted_values(vtblk, h, p16):
        ones = jnp.ones((BF16_ROWS, vtblk.shape[1]), BF16)
        out = _dot(jnp.concatenate([vtblk[head_rows(h), :], ones], axis=0), p16)
        return out[:HEAD_DIM], out[HEAD_DIM:HEAD_DIM + 1]

    own = [load(first_tile + t) for t in tiles]
    scores = {c: jnp.where(key <= qry, _dot_nt(own[c[0]][0], qs[c]), NEG_INF) for c in chains}
    m_run = {c: jnp.max(scores[c], axis=0, keepdims=True) for c in chains}
    probs = {c: jnp.exp2(scores[c] - m_run[c]).astype(BF16) for c in chains}
    l_run = {}
    for t, h in chains:
        pv, psum = weighted_values(own[t][1], h, probs[t, h])
        acc_ref[head_rows(h), tile_cols(t)] = pv
        l_run[t, h] = psum

    span = MOBA_BLOCKS_PER_ITER
    n_chains = len(chains)

    def body(it, carry):
        m_run = dict(zip(chains, carry[:n_chains]))
        l_run = dict(zip(chains, carry[n_chains:]))
        start = pl.multiple_of(it * (span * tq), span * tq)
        kblk = k_ref[pl.ds(start, span * tq), :]
        vtblk = vt_ref[0, :, pl.ds(start, span * tq)]
        new_m, new_l, scores, alphas, probs = {}, {}, {}, {}, {}

        def score(c):
            scores[c] = _dot_nt(kblk, qs[c])

        def softmax(c):
            n = chains.index(c)
            parts = [scores[c][j * tq:(j + 1) * tq] for j in range(span)]
            chosen = [sel_ref[n, pl.ds(span * it + j, 1), :] > 0.0 for j in range(span)]
            m = m_run[c]
            for s, ch in zip(parts, chosen):
                m = jnp.where(ch, jnp.maximum(m, jnp.max(s, axis=0, keepdims=True)), m)
            new_m[c] = m
            alphas[c] = jnp.exp2(m_run[c] - m)
            probs[c] = jnp.concatenate(
                [jnp.exp2(s - jnp.where(ch, m, jnp.inf)).astype(BF16) for s, ch in zip(parts, chosen)], axis=0)

        def values(c):
            t, h = c
            pv, psum = weighted_values(vtblk, h, probs[c])
            acc_ref[head_rows(h), tile_cols(t)] = alphas[c] * acc_ref[head_rows(h), tile_cols(t)] + pv
            new_l[c] = alphas[c] * l_run[c] + psum

        for stage in (score, softmax, values):
            for c in chains:
                stage(c)
        return tuple(new_m[c] for c in chains) + tuple(new_l[c] for c in chains)

    n_iter = (first_tile + MOBA_TILES_PER_STEP - 1 + span - 1) // span
    out = lax.fori_loop(0, n_iter, body, tuple(m_run[c] for c in chains) + tuple(l_run[c] for c in chains))
    for n, (t, h) in enumerate(chains):
        acc_ref[head_rows(h), tile_cols(t)] = acc_ref[head_rows(h), tile_cols(t)] * (1.0 / out[n_chains + n])
    o_ref[...] = acc_ref[...].T.astype(o_ref.dtype)


def _moba_prompt(q16, k16, vt16, kmean, batch, seq):
    tq = MOBA_BLOCK
    rows = tq * MOBA_TILES_PER_STEP
    nb = seq // tq
    nq = seq // rows
    ng = W_MIX // QUAD
    return pl.pallas_call(
        functools.partial(_moba_prompt_kernel, tq=tq),
        grid=(batch, ng, nq),
        in_specs=[pl.BlockSpec((rows, QUAD), lambda b, g, i: (b * nq + i, g)),
                  pl.BlockSpec((seq, QUAD), lambda b, g, i: (b, g)),
                  pl.BlockSpec((1, QUAD, seq), lambda b, g, i: (b, g, 0)),
                  pl.BlockSpec((nb, QUAD), lambda b, g, i: (b, g))],
        out_specs=pl.BlockSpec((rows, QUAD), lambda b, g, i: (b * nq + i, g)),
        out_shape=jax.ShapeDtypeStruct((batch * seq, W_MIX), BF16),
        scratch_shapes=[pltpu.VMEM((MOBA_TILES_PER_STEP * HEADS_PER_STEP, nb, tq), F32),
                        pltpu.VMEM((QUAD, rows), F32)],
        compiler_params=_cparams(("parallel", "parallel", "arbitrary"), 48),
        name="moba_prompt",
    )(q16, k16, vt16, kmean)


def _mem_kv_kernel(mem_ref, g_ref, w_ref, k_ref, v_ref):
    kv = _dot(_rms(mem_ref[...], g_ref[...]).astype(BF16), w_ref[...])
    k_ref[...] = kv[:, :W_MEM]
    v_ref[...] = kv[:, W_MEM:]


def _mem_kv(mem, g, w_kv16):
    n = mem.shape[0]
    return pl.pallas_call(
        _mem_kv_kernel,
        out_shape=[jax.ShapeDtypeStruct((n, W_MEM), F32)] * 2,
        name="mem_kv",
    )(mem, g, w_kv16)


def _post_kernel(x_ref, osb_ref, omb_ref, gate_ref, wsb_ref, wmb_ref, wo_ref,
                 gmem_ref, wq_ref, mk_ref, mv_ref, wom_ref, gffn_ref,
                 wr_hilo_ref, wr_hi_ref, br_ref,
                 x2_ref, xn_ref, comb_ref, count_ref, *, rows_per_mem, keys_per_mem):
    gate = gate_ref[...]
    h = gate[:, :D_MODEL] * _dot(osb_ref[...], wsb_ref[...]) + gate[:, D_MODEL:] * _dot(omb_ref[...], wmb_ref[...])
    x1 = x_ref[...] + _dot(h.astype(BF16), wo_ref[...])

    q = (_dot(_rms(x1, gmem_ref[...]).astype(BF16), wq_ref[...]) * (HD_MEM ** -0.5)).astype(BF16)
    heads = []
    if rows_per_mem is not None:
        shape = (x1.shape[0], mk_ref.shape[1])
        same_mem = (lax.broadcasted_iota(I32, shape, 0) // rows_per_mem
                    == lax.broadcasted_iota(I32, shape, 1) // keys_per_mem)
    for hh in range(H_MEM):
        sl = slice(hh * HD_MEM, (hh + 1) * HD_MEM)
        s = _dot_nt(q[:, sl], mk_ref[0, :, sl])
        if rows_per_mem is not None:
            s = jnp.where(same_mem, s, NEG_INF)
        p = jnp.exp(s - jnp.max(s, axis=1, keepdims=True))
        p = p / jnp.sum(p, axis=1, keepdims=True)
        heads.append(_dot(p.astype(BF16), mv_ref[0, :, sl]))
    o = jnp.concatenate(heads, axis=1)
    x2 = x1 + _dot(o.astype(BF16), wom_ref[...])
    x2_ref[...] = x2

    xn = _rms(x2, gffn_ref[...])
    xn16 = xn.astype(BF16)

    lane = lax.broadcasted_iota(I32, (1, LANES), 1)
    x_hi, x_lo = _split_bf16(xn)
    both = _dot(x_hi, wr_hilo_ref[...])
    logits = both[:, :LANES] + both[:, LANES:] + _dot(x_lo, wr_hi_ref[...]) + br_ref[...]
    is_group = jnp.logical_and(lane >= N_EXPERTS, lane < N_EXPERTS + N_GROUPS)
    gl = jnp.where(is_group, logits, NEG_INF)
    g_max, g_lane = _first_argmax(gl, lane, LANES)
    g_idx = g_lane - N_EXPERTS
    g_w = 1.0 / jnp.sum(jnp.exp(gl - g_max), axis=1, keepdims=True)
    el = jnp.where((lane // EXPERTS_PER_GROUP) == g_idx, logits, NEG_INF)
    e_max, i1 = _first_argmax(el, lane, LANES)
    e_sum = jnp.sum(jnp.exp(el - e_max), axis=1, keepdims=True)
    el2 = jnp.where(lane == i1, NEG_INF, el)
    e_max2, i2 = _first_argmax(el2, lane, LANES)
    w1 = 1.0 / e_sum
    w2 = jnp.exp(e_max2 - e_max) / e_sum
    norm = w1 + w2
    comb = jnp.where(lane == i1, g_w * (w1 / norm), 0.0) + jnp.where(lane == i2, g_w * (w2 / norm), 0.0)
    xn_ref[...] = xn16
    comb_ref[...] = jnp.where(lane == GROUP_ID_LANE, g_idx.astype(F32), comb)
    count_ref[0] = jnp.sum((lane == g_idx).astype(F32), axis=0, keepdims=True)


GROUP_ID_LANE = N_EXPERTS
GROUP_RUN_ALIGN = 16
SORT_ROWS_PAD = 128
EXPERT_TILE = 512
WINDOW_ROWS = 256
EXPERTS_PER_STEP = 8


def _run_starts(dst_ref, k):
    starts = [jnp.int32(0)]
    for g in range(N_GROUPS):
        starts.append(starts[-1] + (dst_ref[k + 1, g] - dst_ref[k, g]))
    return starts


def _scatter_kernel(dst_ref, xn_ref, comb_ref, xs_zero, cs_zero, pos_ref, xs_hbm, cs_hbm, local_x, local_c, sem):
    del xs_zero, cs_zero
    k = pl.program_id(0)
    last = pl.num_programs(0) - 1
    tm = xn_ref.shape[0]
    sorted_rows = tm + SORT_ROWS_PAD
    lane = lax.broadcasted_iota(I32, (1, LANES), 1)
    routed = comb_ref[...]
    g_idx = routed[:, GROUP_ID_LANE:GROUP_ID_LANE + 1].astype(I32)
    comb = jnp.where(lane < N_EXPERTS, routed, 0.0)

    def windows(tile, act):
        starts = _run_starts(dst_ref, tile)
        for g in range(N_GROUPS):
            for half in range(EXPERT_TILE // WINDOW_ROWS):
                def go(g=g, half=half):
                    src = pl.ds(pl.multiple_of(starts[g] + half * WINDOW_ROWS, GROUP_RUN_ALIGN), WINDOW_ROWS)
                    dst = pl.ds(pl.multiple_of(dst_ref[tile, g] + half * WINDOW_ROWS, GROUP_RUN_ALIGN),
                                WINDOW_ROWS)
                    for c in (pltpu.make_async_copy(local_x.at[src], xs_hbm.at[dst], sem.at[0, g, half]),
                              pltpu.make_async_copy(local_c.at[src], cs_hbm.at[dst], sem.at[1, g, half])):
                        getattr(c, act)()
                if half == 0:
                    go()
                else:
                    pl.when(starts[g + 1] - starts[g] > half * WINDOW_ROWS)(go)

    @pl.when(k == 0)
    def _():
        local_x[...] = jnp.zeros_like(local_x)
        local_c[...] = jnp.zeros_like(local_c)

    onehot = (lane == g_idx).astype(BF16)
    earlier = (lax.broadcasted_iota(I32, (tm, tm), 0) > lax.broadcasted_iota(I32, (tm, tm), 1)).astype(BF16)
    before = _dot(earlier, onehot)
    rank = jnp.sum(jnp.where(lane == g_idx, before, 0.0), axis=1, keepdims=True).astype(I32)
    run_start = _run_starts(dst_ref, k)
    pos = rank
    for g in range(N_GROUPS):
        pos = pos + jnp.where(g_idx == g, run_start[g], 0)
    pos_ref[...] = jnp.broadcast_to(pos, pos_ref.shape)

    place_t = (lax.broadcasted_iota(I32, (tm, sorted_rows), 1) == pos).astype(F32)
    place = place_t.T.astype(BF16)
    sorted_x = _dot(place, xn_ref[...]).astype(BF16)
    own = jnp.zeros_like(comb)
    for g in range(N_GROUPS):
        shifted = comb if g == 0 else pltpu.roll(comb, LANES - g * EXPERTS_PER_GROUP, 1)
        own = jnp.where(g_idx == g, shifted, own)
    c_hi = own.astype(BF16)
    c_mid, c_lo = _split_bf16(own - c_hi.astype(F32))
    sorted_c = _dot(place, c_hi) + (_dot(place, c_mid) + _dot(place, c_lo))

    @pl.when(k > 0)
    def _():
        windows(k - 1, "wait")

    local_x[0:sorted_rows, :] = sorted_x
    local_c[0:sorted_rows, :] = sorted_c
    windows(k, "start")

    @pl.when(k == last)
    def _():
        windows(k, "wait")


def _group_layout(counts, n, nt):
    rows = (counts + (GROUP_RUN_ALIGN - 1)) // GROUP_RUN_ALIGN * GROUP_RUN_ALIGN
    rel = jnp.concatenate([jnp.zeros((1, N_GROUPS), I32), jnp.cumsum(rows, axis=0)], axis=0)
    tiles = (rel[nt] + WINDOW_ROWS + EXPERT_TILE - 1) // EXPERT_TILE
    ends = jnp.cumsum(tiles)
    dst = (rel + ((ends - tiles) * EXPERT_TILE)[None, :]).astype(I32)
    worst_rows = n + nt * N_GROUPS * (GROUP_RUN_ALIGN - 1) + N_GROUPS * (WINDOW_ROWS + EXPERT_TILE - 1)
    n_steps = -(-worst_rows // EXPERT_TILE)
    step = jnp.arange(n_steps, dtype=I32)
    group = jnp.sum(step[:, None] >= ends[None, :], axis=1).astype(I32)
    live = (group < N_GROUPS).astype(I32)
    return dst, n_steps * EXPERT_TILE, jnp.minimum(group, N_GROUPS - 1), live


def _scatter(dst, xn16, comb, total_rows):
    n = xn16.shape[0]
    tm = EXPERT_TILE
    row = lambda i, dst: (i, 0)
    any_spec = pl.BlockSpec(memory_space=pl.ANY)
    local_rows = tm + SORT_ROWS_PAD + tm
    grid_spec = pltpu.PrefetchScalarGridSpec(
        num_scalar_prefetch=1, grid=(n // tm,),
        in_specs=[pl.BlockSpec((tm, D_MODEL), row), pl.BlockSpec((tm, LANES), row), any_spec, any_spec],
        out_specs=[pl.BlockSpec((tm, LANES), row), any_spec, any_spec],
        scratch_shapes=[pltpu.VMEM((local_rows, D_MODEL), BF16), pltpu.VMEM((local_rows, LANES), F32),
                        pltpu.SemaphoreType.DMA((2, N_GROUPS, EXPERT_TILE // WINDOW_ROWS))])
    return pl.pallas_call(
        _scatter_kernel,
        grid_spec=grid_spec,
        out_shape=[jax.ShapeDtypeStruct((n, LANES), I32), jax.ShapeDtypeStruct((total_rows, D_MODEL), BF16),
                   jax.ShapeDtypeStruct((total_rows, LANES), F32)],
        input_output_aliases={3: 1, 4: 2},
        compiler_params=_cparams(("arbitrary",), 32),
        name="scatter",
    )(dst, xn16, comb, jnp.zeros((total_rows, D_MODEL), BF16), jnp.zeros((total_rows, LANES), F32))


def _post(x, osb, omb, gate, mem_k16, mem_v16, rows_per_batch, tm, w, rows_per_mem=None, keys_per_mem=None):
    n = x.shape[0]
    nt = n // tm
    tiles_per_batch = rows_per_batch // tm
    row = lambda i: (i, 0)
    const = lambda i: (0, 0)
    mem = lambda i: (i // tiles_per_batch, 0, 0)
    n_mem = mem_k16.shape[1]
    full = lambda a: pl.BlockSpec(a.shape, const)
    out_specs = [pl.BlockSpec((tm, D_MODEL), row), pl.BlockSpec((tm, D_MODEL), row),
                 pl.BlockSpec((tm, LANES), row), pl.BlockSpec((1, 1, LANES), lambda i: (i, 0, 0))]
    out_shape = [jax.ShapeDtypeStruct((n, D_MODEL), F32), jax.ShapeDtypeStruct((n, D_MODEL), BF16),
                 jax.ShapeDtypeStruct((n, LANES), F32), jax.ShapeDtypeStruct((nt, 1, LANES), F32)]
    return pl.pallas_call(
        functools.partial(_post_kernel, rows_per_mem=rows_per_mem, keys_per_mem=keys_per_mem),
        grid=(nt,),
        in_specs=[pl.BlockSpec((tm, D_MODEL), row), pl.BlockSpec((tm, W_MIX), row), pl.BlockSpec((tm, W_MIX), row),
                  pl.BlockSpec((tm, 2 * D_MODEL), row),
                  full(w["w_out_sb"]), full(w["w_out_mb"]), full(w["w_out"]),
                  full(w["norm_mem_g"]), full(w["w_q_mem"]),
                  pl.BlockSpec((1, n_mem, W_MEM), mem), pl.BlockSpec((1, n_mem, W_MEM), mem),
                  full(w["w_o_mem"]), full(w["norm_ffn_g"]),
                  full(w["w_router_hilo"]), full(w["w_router_hi"]), full(w["b_router"])],
        out_specs=out_specs,
        out_shape=out_shape,
        compiler_params=_cparams(("parallel",), 48),
        name="post",
    )(x, osb, omb, gate, w["w_out_sb"], w["w_out_mb"], w["w_out"], w["norm_mem_g"], w["w_q_mem"],
      mem_k16, mem_v16, w["w_o_mem"], w["norm_ffn_g"],
      w["w_router_hilo"], w["w_router_hi"], w["b_router"])


def _moe_kernel(xn_ref, comb_ref, x2_ref, wg_ref, wu_ref, wd_ref, gfin_ref, y_ref, acc_ref):
    e = pl.program_id(1)

    @pl.when(e == 0)
    def _():
        acc_ref[...] = jnp.zeros_like(acc_ref)

    xn = xn_ref[...]
    hg = _dot(xn, wg_ref[0])
    hu = _dot(xn, wu_ref[0])
    lane = lax.broadcasted_iota(I32, (1, LANES), 1)
    weight = jnp.sum(jnp.where(lane == e, comb_ref[...], 0.0), axis=1, keepdims=True)
    hidden = (hg / (1.0 + jnp.exp(-hg))) * hu * weight
    acc_ref[...] += _dot(hidden.astype(BF16), wd_ref[0])

    @pl.when(e == pl.num_programs(1) - 1)
    def _():
        y_ref[...] = _rms(x2_ref[...] + acc_ref[...], gfin_ref[...])


def _moe(xn16, comb, x2, wg16, wu16, wd16, gfin, tm):
    n = xn16.shape[0]
    nt = n // tm
    row = lambda i, e: (i, 0)
    exp_w = lambda i, e: (e, 0, 0)
    return pl.pallas_call(
        _moe_kernel,
        grid=(nt, N_EXPERTS),
        in_specs=[pl.BlockSpec((tm, D_MODEL), row), pl.BlockSpec((tm, LANES), row), pl.BlockSpec((tm, D_MODEL), row),
                  pl.BlockSpec((1, D_MODEL, D_EXPERT), exp_w), pl.BlockSpec((1, D_MODEL, D_EXPERT), exp_w),
                  pl.BlockSpec((1, D_EXPERT, D_MODEL), exp_w),
                  pl.BlockSpec((1, D_MODEL), lambda i, e: (0, 0))],
        out_specs=pl.BlockSpec((tm, D_MODEL), row),
        out_shape=jax.ShapeDtypeStruct((n, D_MODEL), F32),
        scratch_shapes=[pltpu.VMEM((tm, D_MODEL), F32)],
        compiler_params=_cparams(("parallel", "arbitrary"), 48),
        name="moe",
    )(xn16, comb, x2, wg16, wu16, wd16, gfin)


def _group_experts_kernel(grp_ref, live_ref, xs_ref, cs_ref, wg_ref, wu_ref, wd_ref, ys_ref, acc_ref):
    j, e = pl.program_id(0), pl.program_id(1)

    @pl.when(live_ref[j] > 0)
    def _():
        @pl.when(e == 0)
        def _():
            acc_ref[...] = jnp.zeros_like(acc_ref)

        xs = xs_ref[...]
        cs = cs_ref[...]
        lane = lax.broadcasted_iota(I32, (1, LANES), 1)
        pairs = [(_dot(xs, wg_ref[0, j]), _dot(xs, wu_ref[0, j])) for j in range(EXPERTS_PER_STEP)]
        hidden = []
        for j, (hg, hu) in enumerate(pairs):
            weight = jnp.sum(jnp.where(lane == e * EXPERTS_PER_STEP + j, cs, 0.0), axis=1, keepdims=True)
            hidden.append(((hg / (1.0 + jnp.exp(-hg))) * hu * weight).astype(BF16))
        down = [_dot(hidden[j], wd_ref[0, j]) for j in range(EXPERTS_PER_STEP)]
        acc_ref[...] += functools.reduce(jnp.add, down)

        @pl.when(e == pl.num_programs(1) - 1)
        def _():
            ys_ref[...] = acc_ref[...]

    @pl.when(jnp.logical_and(live_ref[j] == 0, e == 0))
    def _():
        ys_ref[...] = jnp.zeros_like(ys_ref)


def _group_experts(xs, cs, tile_group, tile_live, wg16, wu16, wd16):
    n_steps = tile_group.shape[0]
    by_group = lambda a: a.reshape(N_GROUPS, EXPERTS_PER_GROUP, *a.shape[1:])
    rows = lambda j, e, grp, live: (j, 0)
    inner = EXPERTS_PER_GROUP // EXPERTS_PER_STEP
    expert = lambda j, e, grp, live: (grp[j], e * live[j] + (inner - 1) * (1 - live[j]), 0, 0)
    grid_spec = pltpu.PrefetchScalarGridSpec(
        num_scalar_prefetch=2, grid=(n_steps, inner),
        in_specs=[pl.BlockSpec((EXPERT_TILE, D_MODEL), rows), pl.BlockSpec((EXPERT_TILE, LANES), rows),
                  pl.BlockSpec((1, EXPERTS_PER_STEP, D_MODEL, D_EXPERT), expert),
                  pl.BlockSpec((1, EXPERTS_PER_STEP, D_MODEL, D_EXPERT), expert),
                  pl.BlockSpec((1, EXPERTS_PER_STEP, D_EXPERT, D_MODEL), expert)],
        out_specs=pl.BlockSpec((EXPERT_TILE, D_MODEL), rows),
        scratch_shapes=[pltpu.VMEM((EXPERT_TILE, D_MODEL), F32)])
    return pl.pallas_call(
        _group_experts_kernel,
        grid_spec=grid_spec,
        out_shape=jax.ShapeDtypeStruct((xs.shape[0], D_MODEL), F32),
        compiler_params=_cparams(("arbitrary", "arbitrary"), 48),
        name="group_experts",
    )(tile_group, tile_live, xs, cs, by_group(wg16), by_group(wu16), by_group(wd16))


def _gather_norm_kernel(dst_ref, pos_ref, x2_ref, gfin_ref, ys_hbm, y_ref, stage, local, sem):
    k = pl.program_id(0)
    tm = x2_ref.shape[0]
    sorted_rows = tm + SORT_ROWS_PAD
    slot = k % 2
    halves = range(EXPERT_TILE // WINDOW_ROWS)

    def per_window(tile, fn):
        starts = _run_starts(dst_ref, tile)
        for g in range(N_GROUPS):
            for half in halves:
                go = functools.partial(fn, g, half, starts[g])
                if half == 0:
                    go()
                else:
                    pl.when(starts[g + 1] - starts[g] > half * WINDOW_ROWS)(go)
        return starts

    def dma(tile, into, act):
        def fn(g, half, _):
            src = pl.ds(pl.multiple_of(dst_ref[tile, g] + half * WINDOW_ROWS, GROUP_RUN_ALIGN), WINDOW_ROWS)
            copy = pltpu.make_async_copy(ys_hbm.at[src], stage.at[into, g, pl.ds(half * WINDOW_ROWS, WINDOW_ROWS)],
                                         sem.at[into, g, half])
            getattr(copy, act)()
        per_window(tile, fn)

    @pl.when(k == 0)
    def _():
        local[...] = jnp.zeros_like(local)
        dma(0, 0, "start")

    @pl.when(k + 1 < pl.num_programs(0))
    def _():
        dma(k + 1, 1 - slot, "start")

    dma(k, slot, "wait")

    def place(g, half, run_start):
        rows = pl.ds(pl.multiple_of(run_start + half * WINDOW_ROWS, GROUP_RUN_ALIGN), WINDOW_ROWS)
        local[rows, :] = stage[slot, g, pl.ds(half * WINDOW_ROWS, WINDOW_ROWS), :]

    run_start = per_window(k, place)
    local[pl.ds(pl.multiple_of(run_start[N_GROUPS], GROUP_RUN_ALIGN), WINDOW_ROWS), :] = jnp.zeros(
        (WINDOW_ROWS, local.shape[1]), F32)

    pick = (lax.broadcasted_iota(I32, (tm, sorted_rows), 1) == pos_ref[:, 0:1]).astype(BF16)
    hi, lo = _split_bf16(local[0:sorted_rows, :])
    y_ref[...] = _rms(x2_ref[...] + (_dot(pick, hi) + _dot(pick, lo)), gfin_ref[...])


def _gather_norm(dst, pos, x2, gfin, ys):
    n = x2.shape[0]
    tm = EXPERT_TILE
    nt = n // tm
    row = lambda i, offs: (i, 0)
    grid_spec = pltpu.PrefetchScalarGridSpec(
        num_scalar_prefetch=1, grid=(nt,),
        in_specs=[pl.BlockSpec((tm, LANES), row), pl.BlockSpec((tm, D_MODEL), row),
                  pl.BlockSpec((1, D_MODEL), lambda i, offs: (0, 0)), pl.BlockSpec(memory_space=pl.ANY)],
        out_specs=pl.BlockSpec((tm, D_MODEL), row),
        scratch_shapes=[pltpu.VMEM((2, N_GROUPS, tm, D_MODEL), F32),
                        pltpu.VMEM((tm + SORT_ROWS_PAD + tm, D_MODEL), F32),
                        pltpu.SemaphoreType.DMA((2, N_GROUPS, EXPERT_TILE // WINDOW_ROWS))])
    return pl.pallas_call(
        _gather_norm_kernel,
        grid_spec=grid_spec,
        out_shape=jax.ShapeDtypeStruct((n, D_MODEL), F32),
        compiler_params=_cparams(("arbitrary",), 56),
        name="gather_norm",
    )(dst, pos, x2, gfin, ys)


NEW_ROWS = 8


def _new_token_page(ref):
    rows = ref[0]
    return jnp.concatenate([rows, jnp.zeros((PAGE_SIZE - rows.shape[0], rows.shape[1]), rows.dtype)],
                           axis=0).astype(BF16)


def _own_head_block(full, n_new):
    row_head = lax.broadcasted_iota(I32, (full.shape[0], 1), 0) // n_new
    out = jnp.zeros((full.shape[0], HEAD_DIM), F32)
    for h in range(N_HEADS):
        out = jnp.where(row_head == h, full[:, h * HEAD_DIM:(h + 1) * HEAD_DIM], out)
    return out


def _sb_sample_kernel(pt_ref, q_ref, kn_ref, vn_ref, kc_hbm, vc_hbm, o_ref, kbuf, vbuf, sem, *, n_new, n_pages):
    b = pl.program_id(0)
    q = q_ref[0]
    rows = q.shape[0]
    later = _later_keys(PAGE_SIZE)

    def page_copies(p, slot, row=b):
        page = pt_ref[row, p]
        return (pltpu.make_async_copy(kc_hbm.at[page], kbuf.at[slot], sem.at[0, slot]),
                pltpu.make_async_copy(vc_hbm.at[page], vbuf.at[slot], sem.at[1, slot]))

    def start(p, slot, row=b):
        for c in page_copies(p, slot, row):
            c.start()

    def wait(p, slot):
        for c in page_copies(p, slot):
            c.wait()

    slot_of = lambda p: (n_pages - 1 - p) % 2

    @pl.when(b == 0)
    def _():
        start(n_pages - 1, 0)

    key_slot = lax.broadcasted_iota(I32, (rows, PAGE_SIZE), 1)
    tok = lax.broadcasted_iota(I32, (rows, PAGE_SIZE), 0) % n_new
    a, run = _stick_breaking_tile(_dot_nt(q, _new_token_page(kn_ref)), later, jnp.zeros((rows, 1), F32),
                                  key_slot < tok)
    acc = _dot(a.astype(BF16), _new_token_page(vn_ref))

    def cond(carry):
        p, alive = carry[0], carry[1]
        return jnp.logical_and(p >= 0, alive > 0)

    def body(carry):
        p, _, run, acc = carry
        slot = slot_of(p)

        @pl.when(p > 0)
        def _():
            start(p - 1, 1 - slot)

        wait(p, slot)
        z = _dot(q, kbuf[slot].reshape(W_MIX, PAGE_SIZE).astype(BF16))
        a, run = _stick_breaking_tile(z, later, run, None)
        acc = acc + _dot_nt(a.astype(BF16), vbuf[slot].reshape(W_MIX, PAGE_SIZE).astype(BF16))
        alive = (jnp.max(run) > SB_DEAD).astype(I32)
        return (p - 1, alive, run, acc)

    p_next, _, _, acc = lax.while_loop(cond, body, (jnp.int32(n_pages - 1), jnp.int32(1), run, acc))

    @pl.when(p_next >= 0)
    def _():
        wait(p_next, slot_of(p_next))

    @pl.when(b + 1 < pl.num_programs(0))
    def _():
        start(n_pages - 1, 0, b + 1)

    o_ref[0] = _own_head_block(acc, n_new)


def _sb_sample(page_table, q_bd, kn_t, vn_t, cache_kt, cache_vt, n_new):
    nb, n_pages = page_table.shape
    rows = N_HEADS * n_new
    per_b = lambda b, pt: (b, 0, 0)
    page_buffers = pltpu.VMEM((2, N_HEADS, HEAD_DIM, PAGE_SIZE), F32)
    grid_spec = pltpu.PrefetchScalarGridSpec(
        num_scalar_prefetch=1, grid=(nb,),
        in_specs=[pl.BlockSpec((1, rows, W_MIX), per_b),
                  pl.BlockSpec((1, NEW_ROWS, W_MIX), per_b),
                  pl.BlockSpec((1, NEW_ROWS, W_MIX), per_b),
                  pl.BlockSpec(memory_space=pl.ANY),
                  pl.BlockSpec(memory_space=pl.ANY)],
        out_specs=pl.BlockSpec((1, rows, HEAD_DIM), per_b),
        scratch_shapes=[page_buffers, page_buffers, pltpu.SemaphoreType.DMA((2, 2))])
    return pl.pallas_call(
        functools.partial(_sb_sample_kernel, n_new=n_new, n_pages=n_pages),
        grid_spec=grid_spec,
        out_shape=jax.ShapeDtypeStruct((nb, rows, HEAD_DIM), F32),
        compiler_params=_cparams(("arbitrary",), 32),
        name="sb_sample",
    )(page_table, q_bd, kn_t, vn_t, cache_kt, cache_vt)


PAGES_PER_BLOCK = MOBA_BLOCK // PAGE_SIZE
MOBA_SAMPLE_BLOCKS_PER_STEP = 16


def _moba_sample_kernel(pt_ref, q_ref, kn_ref, vn_ref, *refs, n_new):
    pages_per_step = MOBA_SAMPLE_BLOCKS_PER_STEP * PAGES_PER_BLOCK
    k_refs, v_refs = refs[:pages_per_step], refs[pages_per_step:2 * pages_per_step]
    o_ref, pm_ref, pl_ref, pg_ref, po_ref = refs[2 * pages_per_step:]
    step = pl.program_id(1)
    n_blocks = pl.num_programs(1) * MOBA_SAMPLE_BLOCKS_PER_STEP
    q = q_ref[0]
    rows = q.shape[0]
    block_lane = lax.broadcasted_iota(I32, (1, LANES), 1)

    @pl.when(step == 0)
    def _():
        pm_ref[...] = jnp.full(pm_ref.shape, NEG_INF, F32)
        pl_ref[...] = jnp.zeros_like(pl_ref)
        pg_ref[...] = jnp.zeros_like(pg_ref)

    def block_pages(page_refs, j):
        pages = [r[0].reshape(W_MIX, PAGE_SIZE).astype(BF16)
                 for r in page_refs[j * PAGES_PER_BLOCK:(j + 1) * PAGES_PER_BLOCK]]
        return jnp.concatenate(pages, axis=1)

    pm, pl_, pg = pm_ref[...], pl_ref[...], pg_ref[...]
    blocks = range(MOBA_SAMPLE_BLOCKS_PER_STEP)
    scores = [_dot(q, block_pages(k_refs, j)) for j in blocks]
    maxes = [jnp.max(z, axis=1, keepdims=True) for z in scores]
    weights = [jnp.exp2(z - m) for z, m in zip(scores, maxes)]
    for j in blocks:
        blk = step * MOBA_SAMPLE_BLOCKS_PER_STEP + j
        here = block_lane == blk
        pm = jnp.where(here, maxes[j], pm)
        pl_ = jnp.where(here, jnp.sum(weights[j], axis=1, keepdims=True), pl_)
        pg = jnp.where(here, jnp.sum(scores[j], axis=1, keepdims=True), pg)
        po_ref[blk] = _dot_nt(weights[j].astype(BF16), block_pages(v_refs, j))
    pm_ref[...], pl_ref[...], pg_ref[...] = pm, pl_, pg

    @pl.when(step == pl.num_programs(1) - 1)
    def _():
        chosen = _top_blocks(pg, block_lane < n_blocks, block_lane)
        slot = lax.broadcasted_iota(I32, (rows, PAGE_SIZE), 1)
        tok = lax.broadcasted_iota(I32, (rows, PAGE_SIZE), 0) % n_new
        zn = jnp.where(slot <= tok, _dot_nt(q, _new_token_page(kn_ref)), NEG_INF)
        m_all = jnp.maximum(jnp.max(jnp.where(chosen, pm, NEG_INF), axis=1, keepdims=True),
                            jnp.max(zn, axis=1, keepdims=True))
        pn = jnp.exp2(zn - m_all)
        w = jnp.where(chosen, jnp.exp2(pm - m_all), 0.0)
        total = jnp.sum(w * pl_, axis=1, keepdims=True) + jnp.sum(pn, axis=1, keepdims=True)
        acc = _dot(pn.astype(BF16), _new_token_page(vn_ref))
        for n in range(po_ref.shape[0]):
            acc = acc + w[:, n:n + 1] * po_ref[n]
        o_ref[0] = _own_head_block(acc, n_new) / total


def _moba_sample(page_table, q_bd, kn_t, vn_t, cache_kt, cache_vt, n_new):
    nb, n_pages = page_table.shape
    rows = N_HEADS * n_new
    pages_per_step = MOBA_SAMPLE_BLOCKS_PER_STEP * PAGES_PER_BLOCK
    n_steps = n_pages // pages_per_step
    per_b = lambda b, s, pt: (b, 0, 0)
    page_spec = lambda j: pl.BlockSpec((1, N_HEADS, HEAD_DIM, PAGE_SIZE),
                                       lambda b, s, pt: (pt[b, s * pages_per_step + j], 0, 0, 0))
    page_specs = [page_spec(j) for j in range(pages_per_step)]
    grid_spec = pltpu.PrefetchScalarGridSpec(
        num_scalar_prefetch=1, grid=(nb, n_steps),
        in_specs=[pl.BlockSpec((1, rows, W_MIX), per_b),
                  pl.BlockSpec((1, NEW_ROWS, W_MIX), per_b),
                  pl.BlockSpec((1, NEW_ROWS, W_MIX), per_b)] + page_specs + page_specs,
        out_specs=pl.BlockSpec((1, rows, HEAD_DIM), per_b),
        scratch_shapes=[pltpu.VMEM((rows, LANES), F32), pltpu.VMEM((rows, LANES), F32),
                        pltpu.VMEM((rows, LANES), F32),
                        pltpu.VMEM((n_pages // PAGES_PER_BLOCK, rows, W_MIX), F32)])
    return pl.pallas_call(
        functools.partial(_moba_sample_kernel, n_new=n_new),
        grid_spec=grid_spec,
        out_shape=jax.ShapeDtypeStruct((nb, rows, HEAD_DIM), F32),
        compiler_params=_cparams(("parallel", "arbitrary"), 48),
        name="moba_sample",
    )(page_table, q_bd, kn_t, vn_t, *([cache_kt] * pages_per_step), *([cache_vt] * pages_per_step))


def _rope_tables(pos):
    half = HEAD_DIM // 2
    inv_freq = ROPE_THETA ** (-jnp.arange(half, dtype=F32) / half)
    ang = pos.astype(F32)[:, None] * inv_freq[None, :]
    cos = jnp.cos(ang)
    sin = jnp.sin(ang)
    heads_per_tile = LANES // HEAD_DIM
    return (jnp.tile(jnp.concatenate([cos, cos], axis=1), (1, heads_per_tile)),
            jnp.tile(jnp.concatenate([-sin, sin], axis=1), (1, heads_per_tile)))


def _pad_lanes(a):
    return jnp.pad(a, ((0, 0), (0, LANES - a.shape[1])))


def _prepare_weights(norm_mix_g, w_in, b_gate, w_out_sb, w_out_mb, w_out, norm_mem_g, norm_memsrc_g, w_q_mem,
                     w_kv_mem, w_o_mem, norm_ffn_g, w_router_group, b_router_group, w_router_expert,
                     b_router_expert, w_gate_e, w_up_e, w_down_e, norm_final_g):
    row = lambda v: v.reshape(1, -1).astype(F32)
    w_re = w_router_expert.transpose(1, 0, 2).reshape(D_MODEL, N_EXPERTS)
    wr_hi, wr_lo = _split_bf16(_pad_lanes(jnp.concatenate([w_re, w_router_group], axis=1)))
    b_router = _pad_lanes(jnp.concatenate([row(b_router_expert), row(b_router_group)], axis=1))
    return dict(
        norm_mix_g=row(norm_mix_g), w_in=w_in.astype(BF16), b_gate=row(b_gate),
        w_out_sb=w_out_sb.astype(BF16), w_out_mb=w_out_mb.astype(BF16), w_out=w_out.astype(BF16),
        norm_mem_g=row(norm_mem_g), norm_memsrc_g=row(norm_memsrc_g), w_q_mem=w_q_mem.astype(BF16),
        w_kv_mem=w_kv_mem.astype(BF16), w_o_mem=w_o_mem.astype(BF16), norm_ffn_g=row(norm_ffn_g),
        w_router_hilo=jnp.concatenate([wr_hi, wr_lo], axis=1), w_router_hi=wr_hi, b_router=b_router,
        w_gate_e=w_gate_e.astype(BF16), w_up_e=w_up_e.astype(BF16), w_down_e=w_down_e.astype(BF16),
        norm_final_g=row(norm_final_g))


def _tail(x, osb, omb, gate, mem_k16, mem_v16, rows_per_batch, w, tm_post, tm_moe, **mem_mask):
    x2, xn16, comb, _ = _post(x, osb, omb, gate, mem_k16, mem_v16, rows_per_batch, tm_post, w, **mem_mask)
    return _moe(xn16, comb, x2, w["w_gate_e"], w["w_up_e"], w["w_down_e"], w["norm_final_g"], tm_moe)


def kernel(x_prompt, x_sample, mem_prompt, cache_sb_k, cache_sb_v, cache_mb_k, cache_mb_v, cache_mem_k, cache_mem_v, page_table, norm_mix_g, w_in, b_gate, w_out_sb, w_out_mb, w_out, norm_mem_g, norm_memsrc_g, w_q_mem, w_kv_mem, w_o_mem, norm_ffn_g, w_router_group, b_router_group, w_router_expert, b_router_expert, w_gate_e, w_up_e, w_down_e, norm_final_g):
    w = _prepare_weights(norm_mix_g, w_in, b_gate, w_out_sb, w_out_mb, w_out, norm_mem_g, norm_memsrc_g, w_q_mem,
                         w_kv_mem, w_o_mem, norm_ffn_g, w_router_group, b_router_group, w_router_expert,
                         b_router_expert, w_gate_e, w_up_e, w_down_e, norm_final_g)
    batch, seq, _ = x_prompt.shape
    dec_batch, n_new, _ = x_sample.shape
    n_mem = mem_prompt.shape[1]
    n_pages = page_table.shape[1]
    past_len = n_pages * PAGE_SIZE
    assert seq % (MOBA_BLOCK * MOBA_BLOCKS_PER_ITER) == 0 and seq % (MOBA_BLOCK * MOBA_TILES_PER_STEP) == 0
    assert seq % (SB_TILE * SB_TILES_PER_STEP) == 0 and (batch * seq) % EXPERT_TILE == 0
    assert n_new <= PAGE_SIZE and n_pages // PAGES_PER_BLOCK <= LANES
    assert n_pages % (MOBA_SAMPLE_BLOCKS_PER_STEP * PAGES_PER_BLOCK) == 0
    heads = lambda t, b, s: t.reshape(b, s, N_HEADS, HEAD_DIM)

    xp = x_prompt.reshape(batch * seq, D_MODEL)
    cos_p, sin_p = _rope_tables(jnp.arange(seq, dtype=I32))
    (qsb, qmb, gate, ksb_t, vsb_t, kmb_t, vmb_t, ksb16, vsb16_t, kmb16, vmb16_t, kmean) = _proj(
        xp, w["norm_mix_g"], w["w_in"], w["b_gate"], cos_p, sin_p, PROJ_TILE, rows_per_batch=seq)
    osb = _sb_prompt(qsb, ksb16, vsb16_t, batch, seq, SB_TILE)
    omb = _moba_prompt(qmb, kmb16, vmb16_t, kmean.reshape(batch * seq // MOBA_BLOCK, W_MIX), batch, seq)
    mem_k, mem_v = _mem_kv(mem_prompt.reshape(batch * n_mem, D_MODEL), w["norm_memsrc_g"], w["w_kv_mem"])
    x2, xn16, comb, counts = _post(xp, osb, omb, gate, mem_k.astype(BF16).reshape(batch, n_mem, W_MEM),
                                   mem_v.astype(BF16).reshape(batch, n_mem, W_MEM), seq, EXPERT_TILE, w)
    n_tiles = batch * seq // EXPERT_TILE
    dst, total_rows, tile_group, tile_live = _group_layout(
        counts[:, 0, :N_GROUPS].astype(I32), batch * seq, n_tiles)
    pos, rows_by_group, weights_by_group = _scatter(dst, xn16, comb, total_rows)
    expert_out = _group_experts(rows_by_group, weights_by_group, tile_group, tile_live,
                                w["w_gate_e"], w["w_up_e"], w["w_down_e"])
    y_prompt = _gather_norm(dst, pos, x2, w["norm_final_g"], expert_out)

    rows_s = dec_batch * n_new
    xs = x_sample.reshape(rows_s, D_MODEL)
    cos_s, sin_s = _rope_tables(past_len + (jnp.arange(rows_s, dtype=I32) % n_new))
    (qsb_s, qmb_s, gate_s, ksb_s, vsb_s, kmb_s, vmb_s) = _proj(
        xs, w["norm_mix_g"], w["w_in"], w["b_gate"], cos_s, sin_s, rows_s)

    def block_diagonal(q16):
        q_cols = heads(q16, dec_batch, n_new).transpose(0, 2, 1, 3).reshape(dec_batch, N_HEADS * n_new, HEAD_DIM)
        own = (jnp.arange(N_HEADS * n_new)[:, None] // n_new) == (jnp.arange(W_MIX)[None, :] // HEAD_DIM)
        return jnp.where(own[None], jnp.tile(q_cols, (1, 1, N_HEADS)), jnp.zeros((), q16.dtype))

    def new_page(t):
        return jnp.pad(t.reshape(dec_batch, n_new, W_MIX), ((0, 0), (0, NEW_ROWS - n_new), (0, 0)))

    def token_rows(o):
        o = o.reshape(dec_batch, N_HEADS, n_new, HEAD_DIM).transpose(0, 2, 1, 3)
        return o.reshape(rows_s, W_MIX).astype(BF16)

    pages = lambda c: c.transpose(0, 2, 3, 1)
    osb_s = token_rows(_sb_sample(page_table, block_diagonal(qsb_s), new_page(ksb_s), new_page(vsb_s),
                                  pages(cache_sb_k), pages(cache_sb_v), n_new))
    omb_s = token_rows(_moba_sample(page_table, block_diagonal(qmb_s), new_page(kmb_s), new_page(vmb_s),
                                    pages(cache_mb_k), pages(cache_mb_v), n_new))
    y_sample = _tail(xs, osb_s, omb_s, gate_s,
                     cache_mem_k.reshape(1, dec_batch * n_mem, W_MEM).astype(BF16),
                     cache_mem_v.reshape(1, dec_batch * n_mem, W_MEM).astype(BF16),
                     rows_s, w, rows_s, rows_s, rows_per_mem=n_new, keys_per_mem=n_mem)

    mem_heads = lambda t: t.reshape(batch, n_mem, H_MEM, HD_MEM)
    from_head_major = lambda t: t.reshape(batch, N_HEADS, HEAD_DIM, seq).transpose(0, 3, 1, 2)
    return (y_prompt.reshape(batch, seq, D_MODEL), y_sample.reshape(dec_batch, n_new, D_MODEL),
            from_head_major(ksb_t), from_head_major(vsb_t), from_head_major(kmb_t), from_head_major(vmb_t),
            mem_heads(mem_k), mem_heads(mem_v),
            heads(ksb_s, dec_batch, n_new), heads(vsb_s, dec_batch, n_new),
            heads(kmb_s, dec_batch, n_new), heads(vmb_s, dec_batch, n_new))
```

```python
import functools

import jax
import jax.numpy as jnp
from jax import lax
from jax.experimental import pallas as pl
from jax.experimental.pallas import tpu as pltpu

F32 = jnp.float32
BF16 = jnp.bfloat16
I32 = jnp.int32

D_MODEL = 1024
N_HEADS = 8
HEAD_DIM = 64
W_MIX = N_HEADS * HEAD_DIM
PAGE_SIZE = 128
MOBA_BLOCK = 256
MOBA_TOPK = 3
H_MEM = 4
HD_MEM = 128
W_MEM = H_MEM * HD_MEM
N_GROUPS = 4
EXPERTS_PER_GROUP = 8
N_EXPERTS = N_GROUPS * EXPERTS_PER_GROUP
D_EXPERT = 256
ROPE_THETA = 10000.0
RMS_EPS = 1e-6
W_IN_COLS = 6 * W_MIX + 2 * D_MODEL

LOG2_E = 1.4426950408889634
LANES = 128
BF16_ROWS = 16
HEADS_PER_STEP = 4
QUAD = HEADS_PER_STEP * HEAD_DIM
SB_DEAD = -160.0 * LOG2_E
PROJ_TILE = 512
MOBA_TILES_PER_STEP = 4
MOBA_BLOCKS_PER_ITER = 2
SB_TILE = 128
SB_TILES_PER_STEP = 8
NEG_INF = float("-inf")
MIB = 1024 * 1024


def _cparams(semantics, vmem_mib):
    return pltpu.CompilerParams(dimension_semantics=semantics, vmem_limit_bytes=vmem_mib * MIB)


def _rms(x, g):
    ms = jnp.mean(x * x, axis=-1, keepdims=True)
    return (x * lax.rsqrt(ms + RMS_EPS)) * g


def _dot(a, b):
    return jnp.dot(a, b, preferred_element_type=F32)


def _dot_nt(a, b):
    return lax.dot_general(a, b, (((1,), (1,)), ((), ())), preferred_element_type=F32)


def _split_bf16(x):
    hi = x.astype(BF16)
    lo = (x - hi.astype(F32)).astype(BF16)
    return hi, lo


def _later_keys(n):
    return (lax.broadcasted_iota(I32, (n, n), 0) > lax.broadcasted_iota(I32, (n, n), 1)).astype(BF16)


def _stick_breaking_tile(z, later, carried, valid):
    l1p = jnp.log2(1.0 + jnp.exp2(-jnp.abs(z)))
    log_keep = -(jnp.maximum(z, 0.0) + l1p)
    if valid is not None:
        log_keep = jnp.where(valid, log_keep, 0.0)
    hi, lo = _split_bf16(log_keep)
    both = _dot(jnp.concatenate([hi, lo], axis=0), later)
    inner = both[:z.shape[0]] + both[z.shape[0]:]
    a = jnp.exp2(jnp.minimum(z, 0.0) - l1p + carried + inner)
    if valid is not None:
        a = jnp.where(valid, a, 0.0)
    return a, carried + inner[:, 0:1] + log_keep[:, 0:1]


def _proj_kernel(x_ref, g_ref, w_ref, bg_ref, cos_ref, sin_ref, qsb_ref, qmb_ref, gate_ref, *kv_refs, head_major):
    xb = _rms(x_ref[...], g_ref[...]).astype(BF16)

    def seg(lo, width):
        return _dot(xb, w_ref[:, lo:lo + width])

    lane = lax.broadcasted_iota(I32, (1, W_MIX), 1)
    first_half = (lane % HEAD_DIM) < (HEAD_DIM // 2)
    cos = jnp.concatenate([cos_ref[...]] * (W_MIX // LANES), axis=1)
    sin = jnp.concatenate([sin_ref[...]] * (W_MIX // LANES), axis=1)

    def rope(t):
        partner = jnp.where(first_half, pltpu.roll(t, W_MIX - HEAD_DIM // 2, 1),
                            pltpu.roll(t, HEAD_DIM // 2, 1))
        return t * cos + partner * sin

    scale = HEAD_DIM ** -0.5 * LOG2_E
    qsb_ref[...] = (seg(0, W_MIX) * scale).astype(BF16)
    qmb_ref[...] = (rope(seg(3 * W_MIX, W_MIX)) * scale).astype(BF16)
    gl = seg(6 * W_MIX, 2 * D_MODEL) + bg_ref[...]
    gate_ref[...] = 1.0 / (1.0 + jnp.exp(-gl))
    ksb = seg(W_MIX, W_MIX)
    vsb = seg(2 * W_MIX, W_MIX)
    kmb = rope(seg(4 * W_MIX, W_MIX))
    vmb = seg(5 * W_MIX, W_MIX)
    if not head_major:
        for ref, t in zip(kv_refs, (ksb, vsb, kmb, vmb)):
            ref[...] = t
        return
    ksb_t_ref, vsb_t_ref, kmb_t_ref, vmb_t_ref, ksb16_ref, vsb16_t_ref, kmb16_ref, vmb16_t_ref, kmean_ref = kv_refs
    ksb_t_ref[0] = ksb.T
    vsb_t = vsb.T
    vsb_t_ref[0] = vsb_t
    kmb_t_ref[0] = kmb.T
    vmb_t = vmb.T
    vmb_t_ref[0] = vmb_t
    ksb16_ref[...] = ksb.astype(BF16)
    vsb16_t_ref[0] = vsb_t.astype(BF16)
    kmb16_ref[...] = kmb.astype(BF16)
    vmb16_t_ref[0] = vmb_t.astype(BF16)
    for j in range(kmean_ref.shape[0]):
        kmean_ref[j] = jnp.mean(kmb[j * MOBA_BLOCK:(j + 1) * MOBA_BLOCK], axis=0, keepdims=True)


def _proj(x, g, w_in16, b_gate, cos, sin, tm, rows_per_batch=None):
    n = x.shape[0]
    nt = n // tm
    n_rope = cos.shape[0] // tm
    row = lambda i: (i, 0)
    const = lambda i: (0, 0)
    head_major = rows_per_batch is not None
    out_shape = [jax.ShapeDtypeStruct((n, W_MIX), BF16)] * 2 + [jax.ShapeDtypeStruct((n, 2 * D_MODEL), F32)]
    out_specs = [pl.BlockSpec((tm, W_MIX), row)] * 2 + [pl.BlockSpec((tm, 2 * D_MODEL), row)]
    if head_major:
        tiles = rows_per_batch // tm
        n_batch = n // rows_per_batch
        t_spec = pl.BlockSpec((1, W_MIX, tm), lambda i: (i // tiles, 0, i % tiles))
        rows16 = jax.ShapeDtypeStruct((n, W_MIX), BF16)
        t16 = jax.ShapeDtypeStruct((n_batch, W_MIX, rows_per_batch), BF16)
        out_shape += ([jax.ShapeDtypeStruct((n_batch, W_MIX, rows_per_batch), F32)] * 4
                      + [rows16, t16, rows16, t16, jax.ShapeDtypeStruct((n // MOBA_BLOCK, 1, W_MIX), F32)])
        r_spec = pl.BlockSpec((tm, W_MIX), row)
        out_specs += ([t_spec] * 4 + [r_spec, t_spec, r_spec, t_spec,
                                      pl.BlockSpec((tm // MOBA_BLOCK, 1, W_MIX), lambda i: (i, 0, 0))])
    else:
        out_shape += [jax.ShapeDtypeStruct((n, W_MIX), F32)] * 4
        out_specs += [pl.BlockSpec((tm, W_MIX), row)] * 4
    return pl.pallas_call(
        functools.partial(_proj_kernel, head_major=head_major),
        grid=(nt,),
        in_specs=[pl.BlockSpec((tm, D_MODEL), row),
                  pl.BlockSpec((1, D_MODEL), const),
                  pl.BlockSpec((D_MODEL, W_IN_COLS), const),
                  pl.BlockSpec((1, 2 * D_MODEL), const),
                  pl.BlockSpec((tm, LANES), lambda i: (i % n_rope, 0)),
                  pl.BlockSpec((tm, LANES), lambda i: (i % n_rope, 0))],
        out_specs=out_specs,
        out_shape=out_shape,
        compiler_params=_cparams(("parallel",), 48),
        name="proj",
    )(x, g, w_in16, b_gate, cos, sin)


def _head_masks(width):
    lane_head = lax.broadcasted_iota(I32, (1, width), 1) // HEAD_DIM
    return [lane_head == h for h in range(HEADS_PER_STEP)]


def _sb_prompt_kernel(q_ref, k_ref, vt_ref, o_ref, acc_ref, *, tq):
    i = pl.program_id(2)
    masks = _head_masks(QUAD)
    tiles = range(SB_TILES_PER_STEP)
    chains = [(t, h) for t in tiles for h in range(HEADS_PER_STEP)]
    q = [q_ref[t * tq:(t + 1) * tq, :] for t in tiles]
    qs = {(t, h): jnp.where(masks[h], q[t], jnp.zeros_like(q[t])) for t, h in chains}
    key = lax.broadcasted_iota(I32, (tq, tq), 0)
    qry = lax.broadcasted_iota(I32, (tq, tq), 1)
    after = (qry > key).astype(BF16)

    def block(step, dead, diagonal):
        kblk, vtblk = [], []
        for t in tiles:
            kb = i * SB_TILES_PER_STEP + t - step
            if not diagonal:
                dead = {c: (jnp.where(kb < 0, jnp.inf, d) if c[0] == t else d) for c, d in dead.items()}
                kb = jnp.maximum(kb, 0)
            start = pl.multiple_of(kb * tq, tq)
            kblk.append(k_ref[pl.ds(start, tq), :])
            vtblk.append(vt_ref[0, :, pl.ds(start, tq)])
        scores = {c: _dot_nt(kblk[c[0]], qs[c]) for c in chains}
        softplus, parts = {}, {}
        for c in chains:
            z = scores[c]
            sp = jnp.maximum(z, 0.0) + jnp.log2(1.0 + jnp.exp2(-jnp.abs(z)))
            if diagonal:
                sp = jnp.where(key < qry, sp, 0.0)
            softplus[c] = sp
            parts[c] = _split_bf16(sp)
        both = {c: _dot(after, jnp.concatenate(parts[c], axis=1)) for c in chains}
        inner = {c: both[c][:, :tq] + both[c][:, tq:] for c in chains}
        new_dead = {}
        for c in chains:
            t, h = c
            a = jnp.exp2(scores[c] - softplus[c] - (dead[c] + inner[c]))
            if diagonal:
                a = jnp.where(key < qry, a, 0.0)
            rows, cols = slice(h * HEAD_DIM, (h + 1) * HEAD_DIM), slice(t * tq, (t + 1) * tq)
            acc_ref[rows, cols] += _dot(vtblk[t][rows, :], a.astype(BF16))
            new_dead[c] = dead[c] + inner[c][0:1] + softplus[c][0:1]
        return new_dead

    def alive(dead):
        low = functools.reduce(jnp.minimum, [dead[c] for c in chains])
        return (jnp.min(low) < -SB_DEAD).astype(I32)

    acc_ref[...] = jnp.zeros_like(acc_ref)
    dead = block(0, {c: jnp.zeros((1, tq), F32) for c in chains}, True)
    newest = i * SB_TILES_PER_STEP + SB_TILES_PER_STEP - 1

    def cond(carry):
        return jnp.logical_and(carry[0] <= newest, carry[1] > 0)

    def body(carry):
        dead = block(carry[0], dict(zip(chains, carry[2:])), False)
        return (carry[0] + 1, alive(dead)) + tuple(dead[c] for c in chains)

    lax.while_loop(cond, body, (jnp.int32(1), alive(dead)) + tuple(dead[c] for c in chains))
    o_ref[...] = acc_ref[...].T.astype(o_ref.dtype)


def _sb_prompt(q16, k16, vt16, batch, seq, tq):
    rows = tq * SB_TILES_PER_STEP
    nq = seq // rows
    ng = W_MIX // QUAD
    return pl.pallas_call(
        functools.partial(_sb_prompt_kernel, tq=tq),
        grid=(batch, ng, nq),
        in_specs=[pl.BlockSpec((rows, QUAD), lambda b, g, i: (b * nq + i, g)),
                  pl.BlockSpec((seq, QUAD), lambda b, g, i: (b, g)),
                  pl.BlockSpec((1, QUAD, seq), lambda b, g, i: (b, g, 0))],
        out_specs=pl.BlockSpec((rows, QUAD), lambda b, g, i: (b * nq + i, g)),
        out_shape=jax.ShapeDtypeStruct((batch * seq, W_MIX), BF16),
        scratch_shapes=[pltpu.VMEM((QUAD, rows), F32)],
        compiler_params=_cparams(("parallel", "parallel", "arbitrary"), 48),
        name="sb_prompt",
    )(q16, k16, vt16)


def _first_argmax(x, lane, width):
    mx = jnp.max(x, axis=1, keepdims=True)
    idx = jnp.min(jnp.where(x == mx, lane, width), axis=1, keepdims=True)
    return mx, idx


def _top_blocks(gate, eligible, blk):
    g = jnp.where(eligible, gate, NEG_INF)
    sel = jnp.zeros(gate.shape, jnp.bool_)
    for _ in range(MOBA_TOPK):
        _, first = _first_argmax(g, blk, gate.shape[1])
        pick = blk == first
        sel = jnp.logical_or(sel, jnp.logical_and(pick, eligible))
        g = jnp.where(pick, NEG_INF, g)
    return sel


def _top_block_rows(gate, eligible, blk):
    n = gate.shape[0]
    g = jnp.where(eligible, gate, NEG_INF)
    sel = jnp.zeros(gate.shape, jnp.bool_)
    for _ in range(MOBA_TOPK):
        mx = jnp.max(g, axis=0, keepdims=True)
        first = jnp.min(jnp.where(g == mx, blk, n), axis=0, keepdims=True)
        pick = blk == first
        sel = jnp.logical_or(sel, jnp.logical_and(pick, eligible))
        g = jnp.where(pick, NEG_INF, g)
    return sel


def _moba_prompt_kernel(q_ref, k_ref, vt_ref, km_ref, o_ref, sel_ref, acc_ref, *, tq):
    first_tile = pl.program_id(2) * MOBA_TILES_PER_STEP
    tiles = range(MOBA_TILES_PER_STEP)
    chains = [(t, h) for t in tiles for h in range(HEADS_PER_STEP)]
    masks = _head_masks(QUAD)
    q = [q_ref[t * tq:(t + 1) * tq, :] for t in tiles]
    qs = {(t, h): jnp.where(masks[h], q[t], jnp.zeros_like(q[t])) for t, h in chains}
    nb = km_ref.shape[0]
    km_hi, km_lo = _split_bf16(km_ref[...])
    blk = lax.broadcasted_iota(I32, (nb, 1), 0)
    for n, (t, h) in enumerate(chains):
        gate = _dot_nt(km_hi, qs[t, h]) + _dot_nt(km_lo, qs[t, h])
        sel_ref[n] = _top_block_rows(gate, blk < first_tile + t, blk).astype(F32)
    key = lax.broadcasted_iota(I32, (tq, tq), 0)
    qry = lax.broadcasted_iota(I32, (tq, tq), 1)

    def load(kb):
        start = pl.multiple_of(kb * tq, tq)
        return k_ref[pl.ds(start, tq), :], vt_ref[0, :, pl.ds(start, tq)]

    def head_rows(h):
        return slice(h * HEAD_DIM, (h + 1) * HEAD_DIM)

    def tile_cols(t):
        return slice(t * tq, (t + 1) * tq)

    def weighted_values(vtblk, h, p16):
        ones = jnp.ones((BF16_ROWS, vtblk.shape[1]), BF16)
        out = _dot(jnp.concatenate([vtblk[head_rows(h), :], ones], axis=0), p16)
        return out[:HEAD_DIM], out[HEAD_DIM:HEAD_DIM + 1]

    own = [load(first_tile + t) for t in tiles]
    scores = {c: jnp.where(key <= qry, _dot_nt(own[c[0]][0], qs[c]), NEG_INF) for c in chains}
    m_run = {c: jnp.max(scores[c], axis=0, keepdims=True) for c in chains}
    probs = {c: jnp.exp2(scores[c] - m_run[c]).astype(BF16) for c in chains}
    l_run = {}
    for t, h in chains:
        pv, psum = weighted_values(own[t][1], h, probs[t, h])
        acc_ref[head_rows(h), tile_cols(t)] = pv
        l_run[t, h] = psum

    span = MOBA_BLOCKS_PER_ITER
    n_chains = len(chains)

    def body(it, carry):
        m_run = dict(zip(chains, carry[:n_chains]))
        l_run = dict(zip(chains, carry[n_chains:]))
        start = pl.multiple_of(it * (span * tq), span * tq)
        kblk = k_ref[pl.ds(start, span * tq), :]
        vtblk = vt_ref[0, :, pl.ds(start, span * tq)]
        new_m, new_l, scores, alphas, probs = {}, {}, {}, {}, {}

        def score(c):
            scores[c] = _dot_nt(kblk, qs[c])

        def softmax(c):
            n = chains.index(c)
            parts = [scores[c][j * tq:(j + 1) * tq] for j in range(span)]
            chosen = [sel_ref[n, pl.ds(span * it + j, 1), :] > 0.0 for j in range(span)]
            m = m_run[c]
            for s, ch in zip(parts, chosen):
                m = jnp.where(ch, jnp.maximum(m, jnp.max(s, axis=0, keepdims=True)), m)
            new_m[c] = m
            alphas[c] = jnp.exp2(m_run[c] - m)
            probs[c] = jnp.concatenate(
                [jnp.exp2(s - jnp.where(ch, m, jnp.inf)).astype(BF16) for s, ch in zip(parts, chosen)], axis=0)

        def values(c):
            t, h = c
            pv, psum = weighted_values(vtblk, h, probs[c])
            acc_ref[head_rows(h), tile_cols(t)] = alphas[c] * acc_ref[head_rows(h), tile_cols(t)] + pv
            new_l[c] = alphas[c] * l_run[c] + psum

        for stage in (score, softmax, values):
            for c in chains:
                stage(c)
        return tuple(new_m[c] for c in chains) + tuple(new_l[c] for c in chains)

    n_iter = (first_tile + MOBA_TILES_PER_STEP - 1 + span - 1) // span
    out = lax.fori_loop(0, n_iter, body, tuple(m_run[c] for c in chains) + tuple(l_run[c] for c in chains))
    for n, (t, h) in enumerate(chains):
        acc_ref[head_rows(h), tile_cols(t)] = acc_ref[head_rows(h), tile_cols(t)] * (1.0 / out[n_chains + n])
    o_ref[...] = acc_ref[...].T.astype(o_ref.dtype)


def _moba_prompt(q16, k16, vt16, kmean, batch, seq):
    tq = MOBA_BLOCK
    rows = tq * MOBA_TILES_PER_STEP
    nb = seq // tq
    nq = seq // rows
    ng = W_MIX // QUAD
    return pl.pallas_call(
        functools.partial(_moba_prompt_kernel, tq=tq),
        grid=(batch, ng, nq),
        in_specs=[pl.BlockSpec((rows, QUAD), lambda b, g, i: (b * nq + i, g)),
                  pl.BlockSpec((seq, QUAD), lambda b, g, i: (b, g)),
                  pl.BlockSpec((1, QUAD, seq), lambda b, g, i: (b, g, 0)),
                  pl.BlockSpec((nb, QUAD), lambda b, g, i: (b, g))],
        out_specs=pl.BlockSpec((rows, QUAD), lambda b, g, i: (b * nq + i, g)),
        out_shape=jax.ShapeDtypeStruct((batch * seq, W_MIX), BF16),
        scratch_shapes=[pltpu.VMEM((MOBA_TILES_PER_STEP * HEADS_PER_STEP, nb, tq), F32),
                        pltpu.VMEM((QUAD, rows), F32)],
        compiler_params=_cparams(("parallel", "parallel", "arbitrary"), 48),
        name="moba_prompt",
    )(q16, k16, vt16, kmean)


def _mem_kv_kernel(mem_ref, g_ref, w_ref, k_ref, v_ref):
    kv = _dot(_rms(mem_ref[...], g_ref[...]).astype(BF16), w_ref[...])
    k_ref[...] = kv[:, :W_MEM]
    v_ref[...] = kv[:, W_MEM:]


def _mem_kv(mem, g, w_kv16):
    n = mem.shape[0]
    return pl.pallas_call(
        _mem_kv_kernel,
        out_shape=[jax.ShapeDtypeStruct((n, W_MEM), F32)] * 2,
        name="mem_kv",
    )(mem, g, w_kv16)


def _post_kernel(x_ref, osb_ref, omb_ref, gate_ref, wsb_ref, wmb_ref, wo_ref,
                 gmem_ref, wq_ref, mk_ref, mv_ref, wom_ref, gffn_ref,
                 wr_hilo_ref, wr_hi_ref, br_ref,
                 x2_ref, xn_ref, comb_ref, count_ref, *, rows_per_mem, keys_per_mem):
    gate = gate_ref[...]
    h = gate[:, :D_MODEL] * _dot(osb_ref[...], wsb_ref[...]) + gate[:, D_MODEL:] * _dot(omb_ref[...], wmb_ref[...])
    x1 = x_ref[...] + _dot(h.astype(BF16), wo_ref[...])

    q = (_dot(_rms(x1, gmem_ref[...]).astype(BF16), wq_ref[...]) * (HD_MEM ** -0.5)).astype(BF16)
    heads = []
    if rows_per_mem is not None:
        shape = (x1.shape[0], mk_ref.shape[1])
        same_mem = (lax.broadcasted_iota(I32, shape, 0) // rows_per_mem
                    == lax.broadcasted_iota(I32, shape, 1) // keys_per_mem)
    for hh in range(H_MEM):
        sl = slice(hh * HD_MEM, (hh + 1) * HD_MEM)
        s = _dot_nt(q[:, sl], mk_ref[0, :, sl])
        if rows_per_mem is not None:
            s = jnp.where(same_mem, s, NEG_INF)
        p = jnp.exp(s - jnp.max(s, axis=1, keepdims=True))
        p = p / jnp.sum(p, axis=1, keepdims=True)
        heads.append(_dot(p.astype(BF16), mv_ref[0, :, sl]))
    o = jnp.concatenate(heads, axis=1)
    x2 = x1 + _dot(o.astype(BF16), wom_ref[...])
    x2_ref[...] = x2

    xn = _rms(x2, gffn_ref[...])
    xn16 = xn.astype(BF16)

    lane = lax.broadcasted_iota(I32, (1, LANES), 1)
    x_hi, x_lo = _split_bf16(xn)
    both = _dot(x_hi, wr_hilo_ref[...])
    logits = both[:, :LANES] + both[:, LANES:] + _dot(x_lo, wr_hi_ref[...]) + br_ref[...]
    is_group = jnp.logical_and(lane >= N_EXPERTS, lane < N_EXPERTS + N_GROUPS)
    gl = jnp.where(is_group, logits, NEG_INF)
    g_max, g_lane = _first_argmax(gl, lane, LANES)
    g_idx = g_lane - N_EXPERTS
    g_w = 1.0 / jnp.sum(jnp.exp(gl - g_max), axis=1, keepdims=True)
    el = jnp.where((lane // EXPERTS_PER_GROUP) == g_idx, logits, NEG_INF)
    e_max, i1 = _first_argmax(el, lane, LANES)
    e_sum = jnp.sum(jnp.exp(el - e_max), axis=1, keepdims=True)
    el2 = jnp.where(lane == i1, NEG_INF, el)
    e_max2, i2 = _first_argmax(el2, lane, LANES)
    w1 = 1.0 / e_sum
    w2 = jnp.exp(e_max2 - e_max) / e_sum
    norm = w1 + w2
    comb = jnp.where(lane == i1, g_w * (w1 / norm), 0.0) + jnp.where(lane == i2, g_w * (w2 / norm), 0.0)
    xn_ref[...] = xn16
    comb_ref[...] = jnp.where(lane == GROUP_ID_LANE, g_idx.astype(F32), comb)
    count_ref[0] = jnp.sum((lane == g_idx).astype(F32), axis=0, keepdims=True)


GROUP_ID_LANE = N_EXPERTS
GROUP_RUN_ALIGN = 16
SORT_ROWS_PAD = 128
EXPERT_TILE = 512
WINDOW_ROWS = 256
EXPERTS_PER_STEP = 8


def _run_starts(dst_ref, k):
    starts = [jnp.int32(0)]
    for g in range(N_GROUPS):
        starts.append(starts[-1] + (dst_ref[k + 1, g] - dst_ref[k, g]))
    return starts


def _scatter_kernel(dst_ref, xn_ref, comb_ref, xs_zero, cs_zero, pos_ref, xs_hbm, cs_hbm, local_x, local_c, sem):
    del xs_zero, cs_zero
    k = pl.program_id(0)
    last = pl.num_programs(0) - 1
    tm = xn_ref.shape[0]
    sorted_rows = tm + SORT_ROWS_PAD
    lane = lax.broadcasted_iota(I32, (1, LANES), 1)
    routed = comb_ref[...]
    g_idx = routed[:, GROUP_ID_LANE:GROUP_ID_LANE + 1].astype(I32)
    comb = jnp.where(lane < N_EXPERTS, routed, 0.0)

    def windows(tile, act):
        starts = _run_starts(dst_ref, tile)
        for g in range(N_GROUPS):
            for half in range(EXPERT_TILE // WINDOW_ROWS):
                def go(g=g, half=half):
                    src = pl.ds(pl.multiple_of(starts[g] + half * WINDOW_ROWS, GROUP_RUN_ALIGN), WINDOW_ROWS)
                    dst = pl.ds(pl.multiple_of(dst_ref[tile, g] + half * WINDOW_ROWS, GROUP_RUN_ALIGN),
                                WINDOW_ROWS)
                    for c in (pltpu.make_async_copy(local_x.at[src], xs_hbm.at[dst], sem.at[0, g, half]),
                              pltpu.make_async_copy(local_c.at[src], cs_hbm.at[dst], sem.at[1, g, half])):
                        getattr(c, act)()
                if half == 0:
                    go()
                else:
                    pl.when(starts[g + 1] - starts[g] > half * WINDOW_ROWS)(go)

    @pl.when(k == 0)
    def _():
        local_x[...] = jnp.zeros_like(local_x)
        local_c[...] = jnp.zeros_like(local_c)

    onehot = (lane == g_idx).astype(BF16)
    earlier = (lax.broadcasted_iota(I32, (tm, tm), 0) > lax.broadcasted_iota(I32, (tm, tm), 1)).astype(BF16)
    before = _dot(earlier, onehot)
    rank = jnp.sum(jnp.where(lane == g_idx, before, 0.0), axis=1, keepdims=True).astype(I32)
    run_start = _run_starts(dst_ref, k)
    pos = rank
    for g in range(N_GROUPS):
        pos = pos + jnp.where(g_idx == g, run_start[g], 0)
    pos_ref[...] = jnp.broadcast_to(pos, pos_ref.shape)

    place_t = (lax.broadcasted_iota(I32, (tm, sorted_rows), 1) == pos).astype(F32)
    place = place_t.T.astype(BF16)
    sorted_x = _dot(place, xn_ref[...]).astype(BF16)
    own = jnp.zeros_like(comb)
    for g in range(N_GROUPS):
        shifted = comb if g == 0 else pltpu.roll(comb, LANES - g * EXPERTS_PER_GROUP, 1)
        own = jnp.where(g_idx == g, shifted, own)
    c_hi = own.astype(BF16)
    c_mid, c_lo = _split_bf16(own - c_hi.astype(F32))
    sorted_c = _dot(place, c_hi) + (_dot(place, c_mid) + _dot(place, c_lo))

    @pl.when(k > 0)
    def _():
        windows(k - 1, "wait")

    local_x[0:sorted_rows, :] = sorted_x
    local_c[0:sorted_rows, :] = sorted_c
    windows(k, "start")

    @pl.when(k == last)
    def _():
        windows(k, "wait")


def _group_layout(counts, n, nt):
    rows = (counts + (GROUP_RUN_ALIGN - 1)) // GROUP_RUN_ALIGN * GROUP_RUN_ALIGN
    rel = jnp.concatenate([jnp.zeros((1, N_GROUPS), I32), jnp.cumsum(rows, axis=0)], axis=0)
    tiles = (rel[nt] + WINDOW_ROWS + EXPERT_TILE - 1) // EXPERT_TILE
    ends = jnp.cumsum(tiles)
    dst = (rel + ((ends - tiles) * EXPERT_TILE)[None, :]).astype(I32)
    worst_rows = n + nt * N_GROUPS * (GROUP_RUN_ALIGN - 1) + N_GROUPS * (WINDOW_ROWS + EXPERT_TILE - 1)
    n_steps = -(-worst_rows // EXPERT_TILE)
    step = jnp.arange(n_steps, dtype=I32)
    group = jnp.sum(step[:, None] >= ends[None, :], axis=1).astype(I32)
    live = (group < N_GROUPS).astype(I32)
    return dst, n_steps * EXPERT_TILE, jnp.minimum(group, N_GROUPS - 1), live


def _scatter(dst, xn16, comb, total_rows):
    n = xn16.shape[0]
    tm = EXPERT_TILE
    row = lambda i, dst: (i, 0)
    any_spec = pl.BlockSpec(memory_space=pl.ANY)
    local_rows = tm + SORT_ROWS_PAD + tm
    grid_spec = pltpu.PrefetchScalarGridSpec(
        num_scalar_prefetch=1, grid=(n // tm,),
        in_specs=[pl.BlockSpec((tm, D_MODEL), row), pl.BlockSpec((tm, LANES), row), any_spec, any_spec],
        out_specs=[pl.BlockSpec((tm, LANES), row), any_spec, any_spec],
        scratch_shapes=[pltpu.VMEM((local_rows, D_MODEL), BF16), pltpu.VMEM((local_rows, LANES), F32),
                        pltpu.SemaphoreType.DMA((2, N_GROUPS, EXPERT_TILE // WINDOW_ROWS))])
    return pl.pallas_call(
        _scatter_kernel,
        grid_spec=grid_spec,
        out_shape=[jax.ShapeDtypeStruct((n, LANES), I32), jax.ShapeDtypeStruct((total_rows, D_MODEL), BF16),
                   jax.ShapeDtypeStruct((total_rows, LANES), F32)],
        input_output_aliases={3: 1, 4: 2},
        compiler_params=_cparams(("arbitrary",), 32),
        name="scatter",
    )(dst, xn16, comb, jnp.zeros((total_rows, D_MODEL), BF16), jnp.zeros((total_rows, LANES), F32))


def _post(x, osb, omb, gate, mem_k16, mem_v16, rows_per_batch, tm, w, rows_per_mem=None, keys_per_mem=None):
    n = x.shape[0]
    nt = n // tm
    tiles_per_batch = rows_per_batch // tm
    row = lambda i: (i, 0)
    const = lambda i: (0, 0)
    mem = lambda i: (i // tiles_per_batch, 0, 0)
    n_mem = mem_k16.shape[1]
    full = lambda a: pl.BlockSpec(a.shape, const)
    out_specs = [pl.BlockSpec((tm, D_MODEL), row), pl.BlockSpec((tm, D_MODEL), row),
                 pl.BlockSpec((tm, LANES), row), pl.BlockSpec((1, 1, LANES), lambda i: (i, 0, 0))]
    out_shape = [jax.ShapeDtypeStruct((n, D_MODEL), F32), jax.ShapeDtypeStruct((n, D_MODEL), BF16),
                 jax.ShapeDtypeStruct((n, LANES), F32), jax.ShapeDtypeStruct((nt, 1, LANES), F32)]
    return pl.pallas_call(
        functools.partial(_post_kernel, rows_per_mem=rows_per_mem, keys_per_mem=keys_per_mem),
        grid=(nt,),
        in_specs=[pl.BlockSpec((tm, D_MODEL), row), pl.BlockSpec((tm, W_MIX), row), pl.BlockSpec((tm, W_MIX), row),
                  pl.BlockSpec((tm, 2 * D_MODEL), row),
                  full(w["w_out_sb"]), full(w["w_out_mb"]), full(w["w_out"]),
                  full(w["norm_mem_g"]), full(w["w_q_mem"]),
                  pl.BlockSpec((1, n_mem, W_MEM), mem), pl.BlockSpec((1, n_mem, W_MEM), mem),
                  full(w["w_o_mem"]), full(w["norm_ffn_g"]),
                  full(w["w_router_hilo"]), full(w["w_router_hi"]), full(w["b_router"])],
        out_specs=out_specs,
        out_shape=out_shape,
        compiler_params=_cparams(("parallel",), 48),
        name="post",
    )(x, osb, omb, gate, w["w_out_sb"], w["w_out_mb"], w["w_out"], w["norm_mem_g"], w["w_q_mem"],
      mem_k16, mem_v16, w["w_o_mem"], w["norm_ffn_g"],
      w["w_router_hilo"], w["w_router_hi"], w["b_router"])


def _moe_kernel(xn_ref, comb_ref, x2_ref, wg_ref, wu_ref, wd_ref, gfin_ref, y_ref, acc_ref):
    e = pl.program_id(1)

    @pl.when(e == 0)
    def _():
        acc_ref[...] = jnp.zeros_like(acc_ref)

    xn = xn_ref[...]
    hg = _dot(xn, wg_ref[0])
    hu = _dot(xn, wu_ref[0])
    lane = lax.broadcasted_iota(I32, (1, LANES), 1)
    weight = jnp.sum(jnp.where(lane == e, comb_ref[...], 0.0), axis=1, keepdims=True)
    hidden = (hg / (1.0 + jnp.exp(-hg))) * hu * weight
    acc_ref[...] += _dot(hidden.astype(BF16), wd_ref[0])

    @pl.when(e == pl.num_programs(1) - 1)
    def _():
        y_ref[...] = _rms(x2_ref[...] + acc_ref[...], gfin_ref[...])


def _moe(xn16, comb, x2, wg16, wu16, wd16, gfin, tm):
    n = xn16.shape[0]
    nt = n // tm
    row = lambda i, e: (i, 0)
    exp_w = lambda i, e: (e, 0, 0)
    return pl.pallas_call(
        _moe_kernel,
        grid=(nt, N_EXPERTS),
        in_specs=[pl.BlockSpec((tm, D_MODEL), row), pl.BlockSpec((tm, LANES), row), pl.BlockSpec((tm, D_MODEL), row),
                  pl.BlockSpec((1, D_MODEL, D_EXPERT), exp_w), pl.BlockSpec((1, D_MODEL, D_EXPERT), exp_w),
                  pl.BlockSpec((1, D_EXPERT, D_MODEL), exp_w),
                  pl.BlockSpec((1, D_MODEL), lambda i, e: (0, 0))],
        out_specs=pl.BlockSpec((tm, D_MODEL), row),
        out_shape=jax.ShapeDtypeStruct((n, D_MODEL), F32),
        scratch_shapes=[pltpu.VMEM((tm, D_MODEL), F32)],
        compiler_params=_cparams(("parallel", "arbitrary"), 48),
        name="moe",
    )(xn16, comb, x2, wg16, wu16, wd16, gfin)


def _group_experts_kernel(grp_ref, live_ref, xs_ref, cs_ref, wg_ref, wu_ref, wd_ref, ys_ref, acc_ref):
    j, e = pl.program_id(0), pl.program_id(1)

    @pl.when(live_ref[j] > 0)
    def _():
        @pl.when(e == 0)
        def _():
            acc_ref[...] = jnp.zeros_like(acc_ref)

        xs = xs_ref[...]
        cs = cs_ref[...]
        lane = lax.broadcasted_iota(I32, (1, LANES), 1)
        pairs = [(_dot(xs, wg_ref[0, j]), _dot(xs, wu_ref[0, j])) for j in range(EXPERTS_PER_STEP)]
        hidden = []
        for j, (hg, hu) in enumerate(pairs):
            weight = jnp.sum(jnp.where(lane == e * EXPERTS_PER_STEP + j, cs, 0.0), axis=1, keepdims=True)
            hidden.append(((hg / (1.0 + jnp.exp(-hg))) * hu * weight).astype(BF16))
        down = [_dot(hidden[j], wd_ref[0, j]) for j in range(EXPERTS_PER_STEP)]
        acc_ref[...] += functools.reduce(jnp.add, down)

        @pl.when(e == pl.num_programs(1) - 1)
        def _():
            ys_ref[...] = acc_ref[...]

    @pl.when(jnp.logical_and(live_ref[j] == 0, e == 0))
    def _():
        ys_ref[...] = jnp.zeros_like(ys_ref)


def _group_experts(xs, cs, tile_group, tile_live, wg16, wu16, wd16):
    n_steps = tile_group.shape[0]
    by_group = lambda a: a.reshape(N_GROUPS, EXPERTS_PER_GROUP, *a.shape[1:])
    rows = lambda j, e, grp, live: (j, 0)
    inner = EXPERTS_PER_GROUP // EXPERTS_PER_STEP
    expert = lambda j, e, grp, live: (grp[j], e * live[j] + (inner - 1) * (1 - live[j]), 0, 0)
    grid_spec = pltpu.PrefetchScalarGridSpec(
        num_scalar_prefetch=2, grid=(n_steps, inner),
        in_specs=[pl.BlockSpec((EXPERT_TILE, D_MODEL), rows), pl.BlockSpec((EXPERT_TILE, LANES), rows),
                  pl.BlockSpec((1, EXPERTS_PER_STEP, D_MODEL, D_EXPERT), expert),
                  pl.BlockSpec((1, EXPERTS_PER_STEP, D_MODEL, D_EXPERT), expert),
                  pl.BlockSpec((1, EXPERTS_PER_STEP, D_EXPERT, D_MODEL), expert)],
        out_specs=pl.BlockSpec((EXPERT_TILE, D_MODEL), rows),
        scratch_shapes=[pltpu.VMEM((EXPERT_TILE, D_MODEL), F32)])
    return pl.pallas_call(
        _group_experts_kernel,
        grid_spec=grid_spec,
        out_shape=jax.ShapeDtypeStruct((xs.shape[0], D_MODEL), F32),
        compiler_params=_cparams(("arbitrary", "arbitrary"), 48),
        name="group_experts",
    )(tile_group, tile_live, xs, cs, by_group(wg16), by_group(wu16), by_group(wd16))


def _gather_norm_kernel(dst_ref, pos_ref, x2_ref, gfin_ref, ys_hbm, y_ref, stage, local, sem):
    k = pl.program_id(0)
    tm = x2_ref.shape[0]
    sorted_rows = tm + SORT_ROWS_PAD
    slot = k % 2
    halves = range(EXPERT_TILE // WINDOW_ROWS)

    def per_window(tile, fn):
        starts = _run_starts(dst_ref, tile)
        for g in range(N_GROUPS):
            for half in halves:
                go = functools.partial(fn, g, half, starts[g])
                if half == 0:
                    go()
                else:
                    pl.when(starts[g + 1] - starts[g] > half * WINDOW_ROWS)(go)
        return starts

    def dma(tile, into, act):
        def fn(g, half, _):
            src = pl.ds(pl.multiple_of(dst_ref[tile, g] + half * WINDOW_ROWS, GROUP_RUN_ALIGN), WINDOW_ROWS)
            copy = pltpu.make_async_copy(ys_hbm.at[src], stage.at[into, g, pl.ds(half * WINDOW_ROWS, WINDOW_ROWS)],
                                         sem.at[into, g, half])
            getattr(copy, act)()
        per_window(tile, fn)

    @pl.when(k == 0)
    def _():
        local[...] = jnp.zeros_like(local)
        dma(0, 0, "start")

    @pl.when(k + 1 < pl.num_programs(0))
    def _():
        dma(k + 1, 1 - slot, "start")

    dma(k, slot, "wait")

    def place(g, half, run_start):
        rows = pl.ds(pl.multiple_of(run_start + half * WINDOW_ROWS, GROUP_RUN_ALIGN), WINDOW_ROWS)
        local[rows, :] = stage[slot, g, pl.ds(half * WINDOW_ROWS, WINDOW_ROWS), :]

    run_start = per_window(k, place)
    local[pl.ds(pl.multiple_of(run_start[N_GROUPS], GROUP_RUN_ALIGN), WINDOW_ROWS), :] = jnp.zeros(
        (WINDOW_ROWS, local.shape[1]), F32)

    pick = (lax.broadcasted_iota(I32, (tm, sorted_rows), 1) == pos_ref[:, 0:1]).astype(BF16)
    hi, lo = _split_bf16(local[0:sorted_rows, :])
    y_ref[...] = _rms(x2_ref[...] + (_dot(pick, hi) + _dot(pick, lo)), gfin_ref[...])


def _gather_norm(dst, pos, x2, gfin, ys):
    n = x2.shape[0]
    tm = EXPERT_TILE
    nt = n // tm
    row = lambda i, offs: (i, 0)
    grid_spec = pltpu.PrefetchScalarGridSpec(
        num_scalar_prefetch=1, grid=(nt,),
        in_specs=[pl.BlockSpec((tm, LANES), row), pl.BlockSpec((tm, D_MODEL), row),
                  pl.BlockSpec((1, D_MODEL), lambda i, offs: (0, 0)), pl.BlockSpec(memory_space=pl.ANY)],
        out_specs=pl.BlockSpec((tm, D_MODEL), row),
        scratch_shapes=[pltpu.VMEM((2, N_GROUPS, tm, D_MODEL), F32),
                        pltpu.VMEM((tm + SORT_ROWS_PAD + tm, D_MODEL), F32),
                        pltpu.SemaphoreType.DMA((2, N_GROUPS, EXPERT_TILE // WINDOW_ROWS))])
    return pl.pallas_call(
        _gather_norm_kernel,
        grid_spec=grid_spec,
        out_shape=jax.ShapeDtypeStruct((n, D_MODEL), F32),
        compiler_params=_cparams(("arbitrary",), 56),
        name="gather_norm",
    )(dst, pos, x2, gfin, ys)


SB_SAMPLE_PAGES = 2
NEW_ROWS = 8


def _new_token_page(ref):
    rows = ref[0]
    return jnp.concatenate([rows, jnp.zeros((PAGE_SIZE - rows.shape[0], rows.shape[1]), rows.dtype)],
                           axis=0).astype(BF16)


def _own_head_block(full, n_new):
    row_head = lax.broadcasted_iota(I32, (full.shape[0], 1), 0) // n_new
    out = jnp.zeros((full.shape[0], HEAD_DIM), F32)
    for h in range(N_HEADS):
        out = jnp.where(row_head == h, full[:, h * HEAD_DIM:(h + 1) * HEAD_DIM], out)
    return out


def _sb_sample_kernel(pt_ref, q_ref, kn_ref, vn_ref, kc_hbm, vc_hbm, o_ref, kbuf, vbuf, sem, *, n_new, n_pages):
    b = pl.program_id(0)
    q = q_ref[0]
    rows = q.shape[0]
    n_units = n_pages // SB_SAMPLE_PAGES
    unit_keys = SB_SAMPLE_PAGES * PAGE_SIZE

    def unit_copies(u, slot, row=b):
        copies = []
        for j in range(SB_SAMPLE_PAGES):
            page = pt_ref[row, u * SB_SAMPLE_PAGES + j]
            copies += [pltpu.make_async_copy(kc_hbm.at[page], kbuf.at[slot, j], sem.at[0, slot, j]),
                       pltpu.make_async_copy(vc_hbm.at[page], vbuf.at[slot, j], sem.at[1, slot, j])]
        return copies

    def start(u, slot, row=b):
        for c in unit_copies(u, slot, row):
            c.start()

    def wait(u, slot):
        for c in unit_copies(u, slot):
            c.wait()

    def unit_keys_on_lanes(buf, slot):
        return jnp.concatenate([buf[slot, j].reshape(W_MIX, PAGE_SIZE).astype(BF16)
                                for j in range(SB_SAMPLE_PAGES)], axis=1)

    slot_of = lambda u: (n_units - 1 - u) % 2

    @pl.when(b == 0)
    def _():
        start(n_units - 1, 0)

    key_slot = lax.broadcasted_iota(I32, (rows, PAGE_SIZE), 1)
    tok = lax.broadcasted_iota(I32, (rows, PAGE_SIZE), 0) % n_new
    a, run = _stick_breaking_tile(_dot_nt(q, _new_token_page(kn_ref)), _later_keys(PAGE_SIZE),
                                  jnp.zeros((rows, 1), F32), key_slot < tok)
    acc = _dot(a.astype(BF16), _new_token_page(vn_ref))
    later = _later_keys(unit_keys)

    def cond(carry):
        u, alive = carry[0], carry[1]
        return jnp.logical_and(u >= 0, alive > 0)

    def body(carry):
        u, _, run, acc = carry
        slot = slot_of(u)

        @pl.when(u > 0)
        def _():
            start(u - 1, 1 - slot)

        wait(u, slot)
        z = _dot(q, unit_keys_on_lanes(kbuf, slot))
        a, run = _stick_breaking_tile(z, later, run, None)
        acc = acc + _dot_nt(a.astype(BF16), unit_keys_on_lanes(vbuf, slot))
        alive = (jnp.max(run) > SB_DEAD).astype(I32)
        return (u - 1, alive, run, acc)

    u_next, _, _, acc = lax.while_loop(cond, body, (jnp.int32(n_units - 1), jnp.int32(1), run, acc))

    @pl.when(u_next >= 0)
    def _():
        wait(u_next, slot_of(u_next))

    @pl.when(b + 1 < pl.num_programs(0))
    def _():
        start(n_units - 1, 0, b + 1)

    o_ref[0] = _own_head_block(acc, n_new)


def _sb_sample(page_table, q_bd, kn_t, vn_t, cache_kt, cache_vt, n_new):
    nb, n_pages = page_table.shape
    rows = N_HEADS * n_new
    per_b = lambda b, pt: (b, 0, 0)
    assert n_pages % SB_SAMPLE_PAGES == 0
    page_buffers = pltpu.VMEM((2, SB_SAMPLE_PAGES, N_HEADS, HEAD_DIM, PAGE_SIZE), F32)
    grid_spec = pltpu.PrefetchScalarGridSpec(
        num_scalar_prefetch=1, grid=(nb,),
        in_specs=[pl.BlockSpec((1, rows, W_MIX), per_b),
                  pl.BlockSpec((1, NEW_ROWS, W_MIX), per_b),
                  pl.BlockSpec((1, NEW_ROWS, W_MIX), per_b),
                  pl.BlockSpec(memory_space=pl.ANY),
                  pl.BlockSpec(memory_space=pl.ANY)],
        out_specs=pl.BlockSpec((1, rows, HEAD_DIM), per_b),
        scratch_shapes=[page_buffers, page_buffers, pltpu.SemaphoreType.DMA((2, 2, SB_SAMPLE_PAGES))])
    return pl.pallas_call(
        functools.partial(_sb_sample_kernel, n_new=n_new, n_pages=n_pages),
        grid_spec=grid_spec,
        out_shape=jax.ShapeDtypeStruct((nb, rows, HEAD_DIM), F32),
        compiler_params=_cparams(("arbitrary",), 32),
        name="sb_sample",
    )(page_table, q_bd, kn_t, vn_t, cache_kt, cache_vt)


PAGES_PER_BLOCK = MOBA_BLOCK // PAGE_SIZE
MOBA_SAMPLE_BLOCKS_PER_STEP = 16


def _moba_sample_kernel(pt_ref, q_ref, kn_ref, vn_ref, *refs, n_new):
    pages_per_step = MOBA_SAMPLE_BLOCKS_PER_STEP * PAGES_PER_BLOCK
    k_refs, v_refs = refs[:pages_per_step], refs[pages_per_step:2 * pages_per_step]
    o_ref, pm_ref, pl_ref, pg_ref, po_ref = refs[2 * pages_per_step:]
    step = pl.program_id(1)
    n_blocks = pl.num_programs(1) * MOBA_SAMPLE_BLOCKS_PER_STEP
    q = q_ref[0]
    rows = q.shape[0]
    block_lane = lax.broadcasted_iota(I32, (1, LANES), 1)

    @pl.when(step == 0)
    def _():
        pm_ref[...] = jnp.full(pm_ref.shape, NEG_INF, F32)
        pl_ref[...] = jnp.zeros_like(pl_ref)
        pg_ref[...] = jnp.zeros_like(pg_ref)

    def block_pages(page_refs, j):
        pages = [r[0].reshape(W_MIX, PAGE_SIZE).astype(BF16)
                 for r in page_refs[j * PAGES_PER_BLOCK:(j + 1) * PAGES_PER_BLOCK]]
        return jnp.concatenate(pages, axis=1)

    pm, pl_, pg = pm_ref[...], pl_ref[...], pg_ref[...]
    blocks = range(MOBA_SAMPLE_BLOCKS_PER_STEP)
    scores = [_dot(q, block_pages(k_refs, j)) for j in blocks]
    maxes = [jnp.max(z, axis=1, keepdims=True) for z in scores]
    weights = [jnp.exp2(z - m) for z, m in zip(scores, maxes)]
    for j in blocks:
        blk = step * MOBA_SAMPLE_BLOCKS_PER_STEP + j
        here = block_lane == blk
        pm = jnp.where(here, maxes[j], pm)
        pl_ = jnp.where(here, jnp.sum(weights[j], axis=1, keepdims=True), pl_)
        pg = jnp.where(here, jnp.sum(scores[j], axis=1, keepdims=True), pg)
        po_ref[blk] = _dot_nt(weights[j].astype(BF16), block_pages(v_refs, j))
    pm_ref[...], pl_ref[...], pg_ref[...] = pm, pl_, pg

    @pl.when(step == pl.num_programs(1) - 1)
    def _():
        chosen = _top_blocks(pg, block_lane < n_blocks, block_lane)
        slot = lax.broadcasted_iota(I32, (rows, PAGE_SIZE), 1)
        tok = lax.broadcasted_iota(I32, (rows, PAGE_SIZE), 0) % n_new
        zn = jnp.where(slot <= tok, _dot_nt(q, _new_token_page(kn_ref)), NEG_INF)
        m_all = jnp.maximum(jnp.max(jnp.where(chosen, pm, NEG_INF), axis=1, keepdims=True),
                            jnp.max(zn, axis=1, keepdims=True))
        pn = jnp.exp2(zn - m_all)
        w = jnp.where(chosen, jnp.exp2(pm - m_all), 0.0)
        total = jnp.sum(w * pl_, axis=1, keepdims=True) + jnp.sum(pn, axis=1, keepdims=True)
        acc = _dot(pn.astype(BF16), _new_token_page(vn_ref))
        for n in range(po_ref.shape[0]):
            acc = acc + w[:, n:n + 1] * po_ref[n]
        o_ref[0] = _own_head_block(acc, n_new) / total


def _moba_sample(page_table, q_bd, kn_t, vn_t, cache_kt, cache_vt, n_new):
    nb, n_pages = page_table.shape
    rows = N_HEADS * n_new
    pages_per_step = MOBA_SAMPLE_BLOCKS_PER_STEP * PAGES_PER_BLOCK
    n_steps = n_pages // pages_per_step
    per_b = lambda b, s, pt: (b, 0, 0)
    page_spec = lambda j: pl.BlockSpec((1, N_HEADS, HEAD_DIM, PAGE_SIZE),
                                       lambda b, s, pt: (pt[b, s * pages_per_step + j], 0, 0, 0))
    page_specs = [page_spec(j) for j in range(pages_per_step)]
    grid_spec = pltpu.PrefetchScalarGridSpec(
        num_scalar_prefetch=1, grid=(nb, n_steps),
        in_specs=[pl.BlockSpec((1, rows, W_MIX), per_b),
                  pl.BlockSpec((1, NEW_ROWS, W_MIX), per_b),
                  pl.BlockSpec((1, NEW_ROWS, W_MIX), per_b)] + page_specs + page_specs,
        out_specs=pl.BlockSpec((1, rows, HEAD_DIM), per_b),
        scratch_shapes=[pltpu.VMEM((rows, LANES), F32), pltpu.VMEM((rows, LANES), F32),
                        pltpu.VMEM((rows, LANES), F32),
                        pltpu.VMEM((n_pages // PAGES_PER_BLOCK, rows, W_MIX), F32)])
    return pl.pallas_call(
        functools.partial(_moba_sample_kernel, n_new=n_new),
        grid_spec=grid_spec,
        out_shape=jax.ShapeDtypeStruct((nb, rows, HEAD_DIM), F32),
        compiler_params=_cparams(("parallel", "arbitrary"), 48),
        name="moba_sample",
    )(page_table, q_bd, kn_t, vn_t, *([cache_kt] * pages_per_step), *([cache_vt] * pages_per_step))


def _rope_tables(pos):
    half = HEAD_DIM // 2
    inv_freq = ROPE_THETA ** (-jnp.arange(half, dtype=F32) / half)
    ang = pos.astype(F32)[:, None] * inv_freq[None, :]
    cos = jnp.cos(ang)
    sin = jnp.sin(ang)
    heads_per_tile = LANES // HEAD_DIM
    return (jnp.tile(jnp.concatenate([cos, cos], axis=1), (1, heads_per_tile)),
            jnp.tile(jnp.concatenate([-sin, sin], axis=1), (1, heads_per_tile)))


def _pad_lanes(a):
    return jnp.pad(a, ((0, 0), (0, LANES - a.shape[1])))


def _prepare_weights(norm_mix_g, w_in, b_gate, w_out_sb, w_out_mb, w_out, norm_mem_g, norm_memsrc_g, w_q_mem,
                     w_kv_mem, w_o_mem, norm_ffn_g, w_router_group, b_router_group, w_router_expert,
                     b_router_expert, w_gate_e, w_up_e, w_down_e, norm_final_g):
    row = lambda v: v.reshape(1, -1).astype(F32)
    w_re = w_router_expert.transpose(1, 0, 2).reshape(D_MODEL, N_EXPERTS)
    wr_hi, wr_lo = _split_bf16(_pad_lanes(jnp.concatenate([w_re, w_router_group], axis=1)))
    b_router = _pad_lanes(jnp.concatenate([row(b_router_expert), row(b_router_group)], axis=1))
    return dict(
        norm_mix_g=row(norm_mix_g), w_in=w_in.astype(BF16), b_gate=row(b_gate),
        w_out_sb=w_out_sb.astype(BF16), w_out_mb=w_out_mb.astype(BF16), w_out=w_out.astype(BF16),
        norm_mem_g=row(norm_mem_g), norm_memsrc_g=row(norm_memsrc_g), w_q_mem=w_q_mem.astype(BF16),
        w_kv_mem=w_kv_mem.astype(BF16), w_o_mem=w_o_mem.astype(BF16), norm_ffn_g=row(norm_ffn_g),
        w_router_hilo=jnp.concatenate([wr_hi, wr_lo], axis=1), w_router_hi=wr_hi, b_router=b_router,
        w_gate_e=w_gate_e.astype(BF16), w_up_e=w_up_e.astype(BF16), w_down_e=w_down_e.astype(BF16),
        norm_final_g=row(norm_final_g))


def _tail(x, osb, omb, gate, mem_k16, mem_v16, rows_per_batch, w, tm_post, tm_moe, **mem_mask):
    x2, xn16, comb, _ = _post(x, osb, omb, gate, mem_k16, mem_v16, rows_per_batch, tm_post, w, **mem_mask)
    return _moe(xn16, comb, x2, w["w_gate_e"], w["w_up_e"], w["w_down_e"], w["norm_final_g"], tm_moe)


def kernel(x_prompt, x_sample, mem_prompt, cache_sb_k, cache_sb_v, cache_mb_k, cache_mb_v, cache_mem_k, cache_mem_v, page_table, norm_mix_g, w_in, b_gate, w_out_sb, w_out_mb, w_out, norm_mem_g, norm_memsrc_g, w_q_mem, w_kv_mem, w_o_mem, norm_ffn_g, w_router_group, b_router_group, w_router_expert, b_router_expert, w_gate_e, w_up_e, w_down_e, norm_final_g):
    w = _prepare_weights(norm_mix_g, w_in, b_gate, w_out_sb, w_out_mb, w_out, norm_mem_g, norm_memsrc_g, w_q_mem,
                         w_kv_mem, w_o_mem, norm_ffn_g, w_router_group, b_router_group, w_router_expert,
                         b_router_expert, w_gate_e, w_up_e, w_down_e, norm_final_g)
    batch, seq, _ = x_prompt.shape
    dec_batch, n_new, _ = x_sample.shape
    n_mem = mem_prompt.shape[1]
    n_pages = page_table.shape[1]
    past_len = n_pages * PAGE_SIZE
    assert seq % (MOBA_BLOCK * MOBA_BLOCKS_PER_ITER) == 0 and seq % (MOBA_BLOCK * MOBA_TILES_PER_STEP) == 0
    assert seq % (SB_TILE * SB_TILES_PER_STEP) == 0 and (batch * seq) % EXPERT_TILE == 0
    assert n_new <= PAGE_SIZE and n_pages // PAGES_PER_BLOCK <= LANES
    assert n_pages % (MOBA_SAMPLE_BLOCKS_PER_STEP * PAGES_PER_BLOCK) == 0
    heads = lambda t, b, s: t.reshape(b, s, N_HEADS, HEAD_DIM)

    xp = x_prompt.reshape(batch * seq, D_MODEL)
    cos_p, sin_p = _rope_tables(jnp.arange(seq, dtype=I32))
    (qsb, qmb, gate, ksb_t, vsb_t, kmb_t, vmb_t, ksb16, vsb16_t, kmb16, vmb16_t, kmean) = _proj(
        xp, w["norm_mix_g"], w["w_in"], w["b_gate"], cos_p, sin_p, PROJ_TILE, rows_per_batch=seq)
    osb = _sb_prompt(qsb, ksb16, vsb16_t, batch, seq, SB_TILE)
    omb = _moba_prompt(qmb, kmb16, vmb16_t, kmean.reshape(batch * seq // MOBA_BLOCK, W_MIX), batch, seq)
    mem_k, mem_v = _mem_kv(mem_prompt.reshape(batch * n_mem, D_MODEL), w["norm_memsrc_g"], w["w_kv_mem"])
    x2, xn16, comb, counts = _post(xp, osb, omb, gate, mem_k.astype(BF16).reshape(batch, n_mem, W_MEM),
                                   mem_v.astype(BF16).reshape(batch, n_mem, W_MEM), seq, EXPERT_TILE, w)
    n_tiles = batch * seq // EXPERT_TILE
    dst, total_rows, tile_group, tile_live = _group_layout(
        counts[:, 0, :N_GROUPS].astype(I32), batch * seq, n_tiles)
    pos, rows_by_group, weights_by_group = _scatter(dst, xn16, comb, total_rows)
    expert_out = _group_experts(rows_by_group, weights_by_group, tile_group, tile_live,
                                w["w_gate_e"], w["w_up_e"], w["w_down_e"])
    y_prompt = _gather_norm(dst, pos, x2, w["norm_final_g"], expert_out)

    rows_s = dec_batch * n_new
    xs = x_sample.reshape(rows_s, D_MODEL)
    cos_s, sin_s = _rope_tables(past_len + (jnp.arange(rows_s, dtype=I32) % n_new))
    (qsb_s, qmb_s, gate_s, ksb_s, vsb_s, kmb_s, vmb_s) = _proj(
        xs, w["norm_mix_g"], w["w_in"], w["b_gate"], cos_s, sin_s, rows_s)

    def block_diagonal(q16):
        q_cols = heads(q16, dec_batch, n_new).transpose(0, 2, 1, 3).reshape(dec_batch, N_HEADS * n_new, HEAD_DIM)
        own = (jnp.arange(N_HEADS * n_new)[:, None] // n_new) == (jnp.arange(W_MIX)[None, :] // HEAD_DIM)
        return jnp.where(own[None], jnp.tile(q_cols, (1, 1, N_HEADS)), jnp.zeros((), q16.dtype))

    def new_page(t):
        return jnp.pad(t.reshape(dec_batch, n_new, W_MIX), ((0, 0), (0, NEW_ROWS - n_new), (0, 0)))

    def token_rows(o):
        o = o.reshape(dec_batch, N_HEADS, n_new, HEAD_DIM).transpose(0, 2, 1, 3)
        return o.reshape(rows_s, W_MIX).astype(BF16)

    pages = lambda c: c.transpose(0, 2, 3, 1)
    osb_s = token_rows(_sb_sample(page_table, block_diagonal(qsb_s), new_page(ksb_s), new_page(vsb_s),
                                  pages(cache_sb_k), pages(cache_sb_v), n_new))
    omb_s = token_rows(_moba_sample(page_table, block_diagonal(qmb_s), new_page(kmb_s), new_page(vmb_s),
                                    pages(cache_mb_k), pages(cache_mb_v), n_new))
    y_sample = _tail(xs, osb_s, omb_s, gate_s,
                     cache_mem_k.reshape(1, dec_batch * n_mem, W_MEM).astype(BF16),
                     cache_mem_v.reshape(1, dec_batch * n_mem, W_MEM).astype(BF16),
                     rows_s, w, rows_s, rows_s, rows_per_mem=n_new, keys_per_mem=n_mem)

    mem_heads = lambda t: t.reshape(batch, n_mem, H_MEM, HD_MEM)
    from_head_major = lambda t: t.reshape(batch, N_HEADS, HEAD_DIM, seq).transpose(0, 3, 1, 2)
    return (y_prompt.reshape(batch, seq, D_MODEL), y_sample.reshape(dec_batch, n_new, D_MODEL),
            from_head_major(ksb_t), from_head_major(vsb_t), from_head_major(kmb_t), from_head_major(vmb_t),
            mem_heads(mem_k), mem_heads(mem_v),
            heads(ksb_s, dec_batch, n_new), heads(vsb_s, dec_batch, n_new),
            heads(kmb_s, dec_batch, n_new), heads(vmb_s, dec_batch, n_new))
```

```python
import functools

import jax
import jax.numpy as jnp
from jax import lax
from jax.experimental import pallas as pl
from jax.experimental.pallas import tpu as pltpu

F32 = jnp.float32
BF16 = jnp.bfloat16
I32 = jnp.int32

D_MODEL = 1024
N_HEADS = 8
HEAD_DIM = 64
W_MIX = N_HEADS * HEAD_DIM
PAGE_SIZE = 128
MOBA_BLOCK = 256
MOBA_TOPK = 3
H_MEM = 4
HD_MEM = 128
W_MEM = H_MEM * HD_MEM
N_GROUPS = 4
EXPERTS_PER_GROUP = 8
N_EXPERTS = N_GROUPS * EXPERTS_PER_GROUP
D_EXPERT = 256
ROPE_THETA = 10000.0
RMS_EPS = 1e-6
W_IN_COLS = 6 * W_MIX + 2 * D_MODEL

LOG2_E = 1.4426950408889634
LANES = 128
BF16_ROWS = 16
HEADS_PER_STEP = 4
QUAD = HEADS_PER_STEP * HEAD_DIM
SB_DEAD = -160.0 * LOG2_E
PROJ_TILE = 512
MOBA_TILES_PER_STEP = 4
MOBA_BLOCKS_PER_ITER = 2
SB_TILE = 128
SB_TILES_PER_STEP = 8
NEG_INF = float("-inf")
MIB = 1024 * 1024


def _cparams(semantics, vmem_mib):
    return pltpu.CompilerParams(dimension_semantics=semantics, vmem_limit_bytes=vmem_mib * MIB)


def _rms(x, g):
    ms = jnp.mean(x * x, axis=-1, keepdims=True)
    return (x * lax.rsqrt(ms + RMS_EPS)) * g


def _dot(a, b):
    return jnp.dot(a, b, preferred_element_type=F32)


def _dot_nt(a, b):
    return lax.dot_general(a, b, (((1,), (1,)), ((), ())), preferred_element_type=F32)


def _split_bf16(x):
    hi = x.astype(BF16)
    lo = (x - hi.astype(F32)).astype(BF16)
    return hi, lo


def _later_keys(n):
    return (lax.broadcasted_iota(I32, (n, n), 0) > lax.broadcasted_iota(I32, (n, n), 1)).astype(BF16)


def _stick_breaking_tile(z, later, carried, valid):
    l1p = jnp.log2(1.0 + jnp.exp2(-jnp.abs(z)))
    log_keep = -(jnp.maximum(z, 0.0) + l1p)
    if valid is not None:
        log_keep = jnp.where(valid, log_keep, 0.0)
    hi, lo = _split_bf16(log_keep)
    both = _dot(jnp.concatenate([hi, lo], axis=0), later)
    inner = both[:z.shape[0]] + both[z.shape[0]:]
    a = jnp.exp2(jnp.minimum(z, 0.0) - l1p + carried + inner)
    if valid is not None:
        a = jnp.where(valid, a, 0.0)
    return a, carried + inner[:, 0:1] + log_keep[:, 0:1]


def _proj_kernel(x_ref, g_ref, w_ref, bg_ref, cos_ref, sin_ref, qsb_ref, qmb_ref, gate_ref, *kv_refs, head_major):
    xb = _rms(x_ref[...], g_ref[...]).astype(BF16)

    def seg(lo, width):
        return _dot(xb, w_ref[:, lo:lo + width])

    lane = lax.broadcasted_iota(I32, (1, W_MIX), 1)
    first_half = (lane % HEAD_DIM) < (HEAD_DIM // 2)
    cos = jnp.concatenate([cos_ref[...]] * (W_MIX // LANES), axis=1)
    sin = jnp.concatenate([sin_ref[...]] * (W_MIX // LANES), axis=1)

    def rope(t):
        partner = jnp.where(first_half, pltpu.roll(t, W_MIX - HEAD_DIM // 2, 1),
                            pltpu.roll(t, HEAD_DIM // 2, 1))
        return t * cos + partner * sin

    scale = HEAD_DIM ** -0.5 * LOG2_E
    qsb_ref[...] = (seg(0, W_MIX) * scale).astype(BF16)
    qmb_ref[...] = (rope(seg(3 * W_MIX, W_MIX)) * scale).astype(BF16)
    gl = seg(6 * W_MIX, 2 * D_MODEL) + bg_ref[...]
    gate_ref[...] = 1.0 / (1.0 + jnp.exp(-gl))
    ksb = seg(W_MIX, W_MIX)
    vsb = seg(2 * W_MIX, W_MIX)
    kmb = rope(seg(4 * W_MIX, W_MIX))
    vmb = seg(5 * W_MIX, W_MIX)
    if not head_major:
        for ref, t in zip(kv_refs, (ksb, vsb, kmb, vmb)):
            ref[...] = t
        return
    ksb_t_ref, vsb_t_ref, kmb_t_ref, vmb_t_ref, ksb16_ref, vsb16_t_ref, kmb16_ref, vmb16_t_ref, kmean_ref = kv_refs
    ksb_t_ref[0] = ksb.T
    vsb_t = vsb.T
    vsb_t_ref[0] = vsb_t
    kmb_t_ref[0] = kmb.T
    vmb_t = vmb.T
    vmb_t_ref[0] = vmb_t
    ksb16_ref[...] = ksb.astype(BF16)
    vsb16_t_ref[0] = vsb_t.astype(BF16)
    kmb16_ref[...] = kmb.astype(BF16)
    vmb16_t_ref[0] = vmb_t.astype(BF16)
    for j in range(kmean_ref.shape[0]):
        kmean_ref[j] = jnp.mean(kmb[j * MOBA_BLOCK:(j + 1) * MOBA_BLOCK], axis=0, keepdims=True)


def _proj(x, g, w_in16, b_gate, cos, sin, tm, rows_per_batch=None):
    n = x.shape[0]
    nt = n // tm
    n_rope = cos.shape[0] // tm
    row = lambda i: (i, 0)
    const = lambda i: (0, 0)
    head_major = rows_per_batch is not None
    out_shape = [jax.ShapeDtypeStruct((n, W_MIX), BF16)] * 2 + [jax.ShapeDtypeStruct((n, 2 * D_MODEL), F32)]
    out_specs = [pl.BlockSpec((tm, W_MIX), row)] * 2 + [pl.BlockSpec((tm, 2 * D_MODEL), row)]
    if head_major:
        tiles = rows_per_batch // tm
        n_batch = n // rows_per_batch
        t_spec = pl.BlockSpec((1, W_MIX, tm), lambda i: (i // tiles, 0, i % tiles))
        rows16 = jax.ShapeDtypeStruct((n, W_MIX), BF16)
        t16 = jax.ShapeDtypeStruct((n_batch, W_MIX, rows_per_batch), BF16)
        out_shape += ([jax.ShapeDtypeStruct((n_batch, W_MIX, rows_per_batch), F32)] * 4
                      + [rows16, t16, rows16, t16, jax.ShapeDtypeStruct((n // MOBA_BLOCK, 1, W_MIX), F32)])
        r_spec = pl.BlockSpec((tm, W_MIX), row)
        out_specs += ([t_spec] * 4 + [r_spec, t_spec, r_spec, t_spec,
                                      pl.BlockSpec((tm // MOBA_BLOCK, 1, W_MIX), lambda i: (i, 0, 0))])
    else:
        out_shape += [jax.ShapeDtypeStruct((n, W_MIX), F32)] * 4
        out_specs += [pl.BlockSpec((tm, W_MIX), row)] * 4
    return pl.pallas_call(
        functools.partial(_proj_kernel, head_major=head_major),
        grid=(nt,),
        in_specs=[pl.BlockSpec((tm, D_MODEL), row),
                  pl.BlockSpec((1, D_MODEL), const),
                  pl.BlockSpec((D_MODEL, W_IN_COLS), const),
                  pl.BlockSpec((1, 2 * D_MODEL), const),
                  pl.BlockSpec((tm, LANES), lambda i: (i % n_rope, 0)),
                  pl.BlockSpec((tm, LANES), lambda i: (i % n_rope, 0))],
        out_specs=out_specs,
        out_shape=out_shape,
        compiler_params=_cparams(("parallel",), 48),
        name="proj",
    )(x, g, w_in16, b_gate, cos, sin)


def _head_masks(width):
    lane_head = lax.broadcasted_iota(I32, (1, width), 1) // HEAD_DIM
    return [lane_head == h for h in range(HEADS_PER_STEP)]


def _sb_prompt_kernel(q_ref, k_ref, vt_ref, o_ref, acc_ref, *, tq):
    i = pl.program_id(2)
    masks = _head_masks(QUAD)
    tiles = range(SB_TILES_PER_STEP)
    chains = [(t, h) for t in tiles for h in range(HEADS_PER_STEP)]
    q = [q_ref[t * tq:(t + 1) * tq, :] for t in tiles]
    qs = {(t, h): jnp.where(masks[h], q[t], jnp.zeros_like(q[t])) for t, h in chains}
    key = lax.broadcasted_iota(I32, (tq, tq), 0)
    qry = lax.broadcasted_iota(I32, (tq, tq), 1)
    after = (qry > key).astype(BF16)

    def block(step, dead, diagonal):
        kblk, vtblk = [], []
        for t in tiles:
            kb = i * SB_TILES_PER_STEP + t - step
            if not diagonal:
                dead = {c: (jnp.where(kb < 0, jnp.inf, d) if c[0] == t else d) for c, d in dead.items()}
                kb = jnp.maximum(kb, 0)
            start = pl.multiple_of(kb * tq, tq)
            kblk.append(k_ref[pl.ds(start, tq), :])
            vtblk.append(vt_ref[0, :, pl.ds(start, tq)])
        scores = {c: _dot_nt(kblk[c[0]], qs[c]) for c in chains}
        softplus, parts = {}, {}
        for c in chains:
            z = scores[c]
            sp = jnp.maximum(z, 0.0) + jnp.log2(1.0 + jnp.exp2(-jnp.abs(z)))
            if diagonal:
                sp = jnp.where(key < qry, sp, 0.0)
            softplus[c] = sp
            parts[c] = _split_bf16(sp)
        both = {c: _dot(after, jnp.concatenate(parts[c], axis=1)) for c in chains}
        inner = {c: both[c][:, :tq] + both[c][:, tq:] for c in chains}
        new_dead = {}
        for c in chains:
            t, h = c
            a = jnp.exp2(scores[c] - softplus[c] - (dead[c] + inner[c]))
            if diagonal:
                a = jnp.where(key < qry, a, 0.0)
            rows, cols = slice(h * HEAD_DIM, (h + 1) * HEAD_DIM), slice(t * tq, (t + 1) * tq)
            acc_ref[rows, cols] += _dot(vtblk[t][rows, :], a.astype(BF16))
            new_dead[c] = dead[c] + inner[c][0:1] + softplus[c][0:1]
        return new_dead

    def alive(dead):
        low = functools.reduce(jnp.minimum, [dead[c] for c in chains])
        return (jnp.min(low) < -SB_DEAD).astype(I32)

    acc_ref[...] = jnp.zeros_like(acc_ref)
    dead = block(0, {c: jnp.zeros((1, tq), F32) for c in chains}, True)
    newest = i * SB_TILES_PER_STEP + SB_TILES_PER_STEP - 1

    def cond(carry):
        return jnp.logical_and(carry[0] <= newest, carry[1] > 0)

    def body(carry):
        dead = block(carry[0], dict(zip(chains, carry[2:])), False)
        return (carry[0] + 1, alive(dead)) + tuple(dead[c] for c in chains)

    lax.while_loop(cond, body, (jnp.int32(1), alive(dead)) + tuple(dead[c] for c in chains))
    o_ref[...] = acc_ref[...].T.astype(o_ref.dtype)


def _sb_prompt(q16, k16, vt16, batch, seq, tq):
    rows = tq * SB_TILES_PER_STEP
    nq = seq // rows
    ng = W_MIX // QUAD
    return pl.pallas_call(
        functools.partial(_sb_prompt_kernel, tq=tq),
        grid=(batch, ng, nq),
        in_specs=[pl.BlockSpec((rows, QUAD), lambda b, g, i: (b * nq + i, g)),
                  pl.BlockSpec((seq, QUAD), lambda b, g, i: (b, g)),
                  pl.BlockSpec((1, QUAD, seq), lambda b, g, i: (b, g, 0))],
        out_specs=pl.BlockSpec((rows, QUAD), lambda b, g, i: (b * nq + i, g)),
        out_shape=jax.ShapeDtypeStruct((batch * seq, W_MIX), BF16),
        scratch_shapes=[pltpu.VMEM((QUAD, rows), F32)],
        compiler_params=_cparams(("parallel", "parallel", "arbitrary"), 48),
        name="sb_prompt",
    )(q16, k16, vt16)


def _first_argmax(x, lane, width):
    mx = jnp.max(x, axis=1, keepdims=True)
    idx = jnp.min(jnp.where(x == mx, lane, width), axis=1, keepdims=True)
    return mx, idx


def _top_blocks(gate, eligible, blk):
    g = jnp.where(eligible, gate, NEG_INF)
    sel = jnp.zeros(gate.shape, jnp.bool_)
    for _ in range(MOBA_TOPK):
        _, first = _first_argmax(g, blk, gate.shape[1])
        pick = blk == first
        sel = jnp.logical_or(sel, jnp.logical_and(pick, eligible))
        g = jnp.where(pick, NEG_INF, g)
    return sel


def _top_block_rows(gate, eligible, blk):
    n = gate.shape[0]
    g = jnp.where(eligible, gate, NEG_INF)
    sel = jnp.zeros(gate.shape, jnp.bool_)
    for _ in range(MOBA_TOPK):
        mx = jnp.max(g, axis=0, keepdims=True)
        first = jnp.min(jnp.where(g == mx, blk, n), axis=0, keepdims=True)
        pick = blk == first
        sel = jnp.logical_or(sel, jnp.logical_and(pick, eligible))
        g = jnp.where(pick, NEG_INF, g)
    return sel


def _moba_prompt_kernel(q_ref, k_ref, vt_ref, km_ref, o_ref, sel_ref, acc_ref, *, tq):
    first_tile = pl.program_id(2) * MOBA_TILES_PER_STEP
    tiles = range(MOBA_TILES_PER_STEP)
    chains = [(t, h) for t in tiles for h in range(HEADS_PER_STEP)]
    masks = _head_masks(QUAD)
    q = [q_ref[t * tq:(t + 1) * tq, :] for t in tiles]
    qs = {(t, h): jnp.where(masks[h], q[t], jnp.zeros_like(q[t])) for t, h in chains}
    nb = km_ref.shape[0]
    km_hi, km_lo = _split_bf16(km_ref[...])
    blk = lax.broadcasted_iota(I32, (nb, 1), 0)
    for n, (t, h) in enumerate(chains):
        gate = _dot_nt(km_hi, qs[t, h]) + _dot_nt(km_lo, qs[t, h])
        sel_ref[n] = _top_block_rows(gate, blk < first_tile + t, blk).astype(F32)
    key = lax.broadcasted_iota(I32, (tq, tq), 0)
    qry = lax.broadcasted_iota(I32, (tq, tq), 1)

    def load(kb):
        start = pl.multiple_of(kb * tq, tq)
        return k_ref[pl.ds(start, tq), :], vt_ref[0, :, pl.ds(start, tq)]

    def head_rows(h):
        return slice(h * HEAD_DIM, (h + 1) * HEAD_DIM)

    def tile_cols(t):
        return slice(t * tq, (t + 1) * tq)

    def weighted_values(vtblk, h, p16):
        ones = jnp.ones((BF16_ROWS, vtblk.shape[1]), BF16)
        out = _dot(jnp.concatenate([vtblk[head_rows(h), :], ones], axis=0), p16)
        return out[:HEAD_DIM], out[HEAD_DIM:HEAD_DIM + 1]

    own = [load(first_tile + t) for t in tiles]
    scores = {c: jnp.where(key <= qry, _dot_nt(own[c[0]][0], qs[c]), NEG_INF) for c in chains}
    m_run = {c: jnp.max(scores[c], axis=0, keepdims=True) for c in chains}
    probs = {c: jnp.exp2(scores[c] - m_run[c]).astype(BF16) for c in chains}
    l_run = {}
    for t, h in chains:
        pv, psum = weighted_values(own[t][1], h, probs[t, h])
        acc_ref[head_rows(h), tile_cols(t)] = pv
        l_run[t, h] = psum

    span = MOBA_BLOCKS_PER_ITER
    n_chains = len(chains)

    def body(it, carry):
        m_run = dict(zip(chains, carry[:n_chains]))
        l_run = dict(zip(chains, carry[n_chains:]))
        start = pl.multiple_of(it * (span * tq), span * tq)
        kblk = k_ref[pl.ds(start, span * tq), :]
        vtblk = vt_ref[0, :, pl.ds(start, span * tq)]
        new_m, new_l, scores, alphas, probs = {}, {}, {}, {}, {}

        def score(c):
            scores[c] = _dot_nt(kblk, qs[c])

        def softmax(c):
            n = chains.index(c)
            parts = [scores[c][j * tq:(j + 1) * tq] for j in range(span)]
            chosen = [sel_ref[n, pl.ds(span * it + j, 1), :] > 0.0 for j in range(span)]
            m = m_run[c]
            for s, ch in zip(parts, chosen):
                m = jnp.where(ch, jnp.maximum(m, jnp.max(s, axis=0, keepdims=True)), m)
            new_m[c] = m
            alphas[c] = jnp.exp2(m_run[c] - m)
            probs[c] = jnp.concatenate(
                [jnp.exp2(s - jnp.where(ch, m, jnp.inf)).astype(BF16) for s, ch in zip(parts, chosen)], axis=0)

        def values(c):
            t, h = c
            pv, psum = weighted_values(vtblk, h, probs[c])
            acc_ref[head_rows(h), tile_cols(t)] = alphas[c] * acc_ref[head_rows(h), tile_cols(t)] + pv
            new_l[c] = alphas[c] * l_run[c] + psum

        for stage in (score, softmax, values):
            for c in chains:
                stage(c)
        return tuple(new_m[c] for c in chains) + tuple(new_l[c] for c in chains)

    n_iter = (first_tile + MOBA_TILES_PER_STEP - 1 + span - 1) // span
    out = lax.fori_loop(0, n_iter, body, tuple(m_run[c] for c in chains) + tuple(l_run[c] for c in chains))
    for n, (t, h) in enumerate(chains):
        acc_ref[head_rows(h), tile_cols(t)] = acc_ref[head_rows(h), tile_cols(t)] * (1.0 / out[n_chains + n])
    o_ref[...] = acc_ref[...].T.astype(o_ref.dtype)


def _moba_prompt(q16, k16, vt16, kmean, batch, seq):
    tq = MOBA_BLOCK
    rows = tq * MOBA_TILES_PER_STEP
    nb = seq // tq
    nq = seq // rows
    ng = W_MIX // QUAD
    return pl.pallas_call(
        functools.partial(_moba_prompt_kernel, tq=tq),
        grid=(batch, ng, nq),
        in_specs=[pl.BlockSpec((rows, QUAD), lambda b, g, i: (b * nq + i, g)),
                  pl.BlockSpec((seq, QUAD), lambda b, g, i: (b, g)),
                  pl.BlockSpec((1, QUAD, seq), lambda b, g, i: (b, g, 0)),
                  pl.BlockSpec((nb, QUAD), lambda b, g, i: (b, g))],
        out_specs=pl.BlockSpec((rows, QUAD), lambda b, g, i: (b * nq + i, g)),
        out_shape=jax.ShapeDtypeStruct((batch * seq, W_MIX), BF16),
        scratch_shapes=[pltpu.VMEM((MOBA_TILES_PER_STEP * HEADS_PER_STEP, nb, tq), F32),
                        pltpu.VMEM((QUAD, rows), F32)],
        compiler_params=_cparams(("parallel", "parallel", "arbitrary"), 48),
        name="moba_prompt",
    )(q16, k16, vt16, kmean)


def _mem_kv_kernel(mem_ref, g_ref, w_ref, k_ref, v_ref):
    kv = _dot(_rms(mem_ref[...], g_ref[...]).astype(BF16), w_ref[...])
    k_ref[...] = kv[:, :W_MEM]
    v_ref[...] = kv[:, W_MEM:]


def _mem_kv(mem, g, w_kv16):
    n = mem.shape[0]
    return pl.pallas_call(
        _mem_kv_kernel,
        out_shape=[jax.ShapeDtypeStruct((n, W_MEM), F32)] * 2,
        name="mem_kv",
    )(mem, g, w_kv16)


def _post_kernel(x_ref, osb_ref, omb_ref, gate_ref, wsb_ref, wmb_ref, wo_ref,
                 gmem_ref, wq_ref, mk_ref, mv_ref, wom_ref, gffn_ref,
                 wr_hilo_ref, wr_hi_ref, br_ref,
                 x2_ref, xn_ref, comb_ref, count_ref, *, rows_per_mem, keys_per_mem):
    gate = gate_ref[...]
    h = gate[:, :D_MODEL] * _dot(osb_ref[...], wsb_ref[...]) + gate[:, D_MODEL:] * _dot(omb_ref[...], wmb_ref[...])
    x1 = x_ref[...] + _dot(h.astype(BF16), wo_ref[...])

    q = (_dot(_rms(x1, gmem_ref[...]).astype(BF16), wq_ref[...]) * (HD_MEM ** -0.5)).astype(BF16)
    heads = []
    if rows_per_mem is not None:
        shape = (x1.shape[0], mk_ref.shape[1])
        same_mem = (lax.broadcasted_iota(I32, shape, 0) // rows_per_mem
                    == lax.broadcasted_iota(I32, shape, 1) // keys_per_mem)
    for hh in range(H_MEM):
        sl = slice(hh * HD_MEM, (hh + 1) * HD_MEM)
        s = _dot_nt(q[:, sl], mk_ref[0, :, sl])
        if rows_per_mem is not None:
            s = jnp.where(same_mem, s, NEG_INF)
        p = jnp.exp(s - jnp.max(s, axis=1, keepdims=True))
        p = p / jnp.sum(p, axis=1, keepdims=True)
        heads.append(_dot(p.astype(BF16), mv_ref[0, :, sl]))
    o = jnp.concatenate(heads, axis=1)
    x2 = x1 + _dot(o.astype(BF16), wom_ref[...])
    x2_ref[...] = x2

    xn = _rms(x2, gffn_ref[...])
    xn16 = xn.astype(BF16)

    lane = lax.broadcasted_iota(I32, (1, LANES), 1)
    x_hi, x_lo = _split_bf16(xn)
    both = _dot(x_hi, wr_hilo_ref[...])
    logits = both[:, :LANES] + both[:, LANES:] + _dot(x_lo, wr_hi_ref[...]) + br_ref[...]
    is_group = jnp.logical_and(lane >= N_EXPERTS, lane < N_EXPERTS + N_GROUPS)
    gl = jnp.where(is_group, logits, NEG_INF)
    g_max, g_lane = _first_argmax(gl, lane, LANES)
    g_idx = g_lane - N_EXPERTS
    g_w = 1.0 / jnp.sum(jnp.exp(gl - g_max), axis=1, keepdims=True)
    el = jnp.where((lane // EXPERTS_PER_GROUP) == g_idx, logits, NEG_INF)
    e_max, i1 = _first_argmax(el, lane, LANES)
    e_sum = jnp.sum(jnp.exp(el - e_max), axis=1, keepdims=True)
    el2 = jnp.where(lane == i1, NEG_INF, el)
    e_max2, i2 = _first_argmax(el2, lane, LANES)
    w1 = 1.0 / e_sum
    w2 = jnp.exp(e_max2 - e_max) / e_sum
    norm = w1 + w2
    comb = jnp.where(lane == i1, g_w * (w1 / norm), 0.0) + jnp.where(lane == i2, g_w * (w2 / norm), 0.0)
    xn_ref[...] = xn16
    comb_ref[...] = jnp.where(lane == GROUP_ID_LANE, g_idx.astype(F32), comb)
    count_ref[0] = jnp.sum((lane == g_idx).astype(F32), axis=0, keepdims=True)


GROUP_ID_LANE = N_EXPERTS
GROUP_RUN_ALIGN = 16
SORT_ROWS_PAD = 128
EXPERT_TILE = 512
WINDOW_ROWS = 256
EXPERTS_PER_STEP = 8


def _run_starts(dst_ref, k):
    starts = [jnp.int32(0)]
    for g in range(N_GROUPS):
        starts.append(starts[-1] + (dst_ref[k + 1, g] - dst_ref[k, g]))
    return starts


def _scatter_kernel(dst_ref, xn_ref, comb_ref, xs_zero, cs_zero, pos_ref, xs_hbm, cs_hbm, local_x, local_c, sem):
    del xs_zero, cs_zero
    k = pl.program_id(0)
    last = pl.num_programs(0) - 1
    tm = xn_ref.shape[0]
    sorted_rows = tm + SORT_ROWS_PAD
    lane = lax.broadcasted_iota(I32, (1, LANES), 1)
    routed = comb_ref[...]
    g_idx = routed[:, GROUP_ID_LANE:GROUP_ID_LANE + 1].astype(I32)
    comb = jnp.where(lane < N_EXPERTS, routed, 0.0)

    def windows(tile, act):
        starts = _run_starts(dst_ref, tile)
        for g in range(N_GROUPS):
            for half in range(EXPERT_TILE // WINDOW_ROWS):
                def go(g=g, half=half):
                    src = pl.ds(pl.multiple_of(starts[g] + half * WINDOW_ROWS, GROUP_RUN_ALIGN), WINDOW_ROWS)
                    dst = pl.ds(pl.multiple_of(dst_ref[tile, g] + half * WINDOW_ROWS, GROUP_RUN_ALIGN),
                                WINDOW_ROWS)
                    for c in (pltpu.make_async_copy(local_x.at[src], xs_hbm.at[dst], sem.at[0, g, half]),
                              pltpu.make_async_copy(local_c.at[src], cs_hbm.at[dst], sem.at[1, g, half])):
                        getattr(c, act)()
                if half == 0:
                    go()
                else:
                    pl.when(starts[g + 1] - starts[g] > half * WINDOW_ROWS)(go)

    @pl.when(k == 0)
    def _():
        local_x[...] = jnp.zeros_like(local_x)
        local_c[...] = jnp.zeros_like(local_c)

    onehot = (lane == g_idx).astype(BF16)
    earlier = (lax.broadcasted_iota(I32, (tm, tm), 0) > lax.broadcasted_iota(I32, (tm, tm), 1)).astype(BF16)
    before = _dot(earlier, onehot)
    rank = jnp.sum(jnp.where(lane == g_idx, before, 0.0), axis=1, keepdims=True).astype(I32)
    run_start = _run_starts(dst_ref, k)
    pos = rank
    for g in range(N_GROUPS):
        pos = pos + jnp.where(g_idx == g, run_start[g], 0)
    pos_ref[...] = jnp.broadcast_to(pos, pos_ref.shape)

    place_t = (lax.broadcasted_iota(I32, (tm, sorted_rows), 1) == pos).astype(F32)
    place = place_t.T.astype(BF16)
    sorted_x = _dot(place, xn_ref[...]).astype(BF16)
    own = jnp.zeros_like(comb)
    for g in range(N_GROUPS):
        shifted = comb if g == 0 else pltpu.roll(comb, LANES - g * EXPERTS_PER_GROUP, 1)
        own = jnp.where(g_idx == g, shifted, own)
    c_hi = own.astype(BF16)
    c_mid, c_lo = _split_bf16(own - c_hi.astype(F32))
    sorted_c = _dot(place, c_hi) + (_dot(place, c_mid) + _dot(place, c_lo))

    @pl.when(k > 0)
    def _():
        windows(k - 1, "wait")

    local_x[0:sorted_rows, :] = sorted_x
    local_c[0:sorted_rows, :] = sorted_c
    windows(k, "start")

    @pl.when(k == last)
    def _():
        windows(k, "wait")


def _group_layout(counts, n, nt):
    rows = (counts + (GROUP_RUN_ALIGN - 1)) // GROUP_RUN_ALIGN * GROUP_RUN_ALIGN
    rel = jnp.concatenate([jnp.zeros((1, N_GROUPS), I32), jnp.cumsum(rows, axis=0)], axis=0)
    tiles = (rel[nt] + WINDOW_ROWS + EXPERT_TILE - 1) // EXPERT_TILE
    ends = jnp.cumsum(tiles)
    dst = (rel + ((ends - tiles) * EXPERT_TILE)[None, :]).astype(I32)
    worst_rows = n + nt * N_GROUPS * (GROUP_RUN_ALIGN - 1) + N_GROUPS * (WINDOW_ROWS + EXPERT_TILE - 1)
    n_steps = -(-worst_rows // EXPERT_TILE)
    step = jnp.arange(n_steps, dtype=I32)
    group = jnp.sum(step[:, None] >= ends[None, :], axis=1).astype(I32)
    live = (group < N_GROUPS).astype(I32)
    return dst, n_steps * EXPERT_TILE, jnp.minimum(group, N_GROUPS - 1), live


def _scatter(dst, xn16, comb, total_rows):
    n = xn16.shape[0]
    tm = EXPERT_TILE
    row = lambda i, dst: (i, 0)
    any_spec = pl.BlockSpec(memory_space=pl.ANY)
    local_rows = tm + SORT_ROWS_PAD + tm
    grid_spec = pltpu.PrefetchScalarGridSpec(
        num_scalar_prefetch=1, grid=(n // tm,),
        in_specs=[pl.BlockSpec((tm, D_MODEL), row), pl.BlockSpec((tm, LANES), row), any_spec, any_spec],
        out_specs=[pl.BlockSpec((tm, LANES), row), any_spec, any_spec],
        scratch_shapes=[pltpu.VMEM((local_rows, D_MODEL), BF16), pltpu.VMEM((local_rows, LANES), F32),
                        pltpu.SemaphoreType.DMA((2, N_GROUPS, EXPERT_TILE // WINDOW_ROWS))])
    return pl.pallas_call(
        _scatter_kernel,
        grid_spec=grid_spec,
        out_shape=[jax.ShapeDtypeStruct((n, LANES), I32), jax.ShapeDtypeStruct((total_rows, D_MODEL), BF16),
                   jax.ShapeDtypeStruct((total_rows, LANES), F32)],
        input_output_aliases={3: 1, 4: 2},
        compiler_params=_cparams(("arbitrary",), 32),
        name="scatter",
    )(dst, xn16, comb, jnp.zeros((total_rows, D_MODEL), BF16), jnp.zeros((total_rows, LANES), F32))


def _post(x, osb, omb, gate, mem_k16, mem_v16, rows_per_batch, tm, w, rows_per_mem=None, keys_per_mem=None):
    n = x.shape[0]
    nt = n // tm
    tiles_per_batch = rows_per_batch // tm
    row = lambda i: (i, 0)
    const = lambda i: (0, 0)
    mem = lambda i: (i // tiles_per_batch, 0, 0)
    n_mem = mem_k16.shape[1]
    full = lambda a: pl.BlockSpec(a.shape, const)
    out_specs = [pl.BlockSpec((tm, D_MODEL), row), pl.BlockSpec((tm, D_MODEL), row),
                 pl.BlockSpec((tm, LANES), row), pl.BlockSpec((1, 1, LANES), lambda i: (i, 0, 0))]
    out_shape = [jax.ShapeDtypeStruct((n, D_MODEL), F32), jax.ShapeDtypeStruct((n, D_MODEL), BF16),
                 jax.ShapeDtypeStruct((n, LANES), F32), jax.ShapeDtypeStruct((nt, 1, LANES), F32)]
    return pl.pallas_call(
        functools.partial(_post_kernel, rows_per_mem=rows_per_mem, keys_per_mem=keys_per_mem),
        grid=(nt,),
        in_specs=[pl.BlockSpec((tm, D_MODEL), row), pl.BlockSpec((tm, W_MIX), row), pl.BlockSpec((tm, W_MIX), row),
                  pl.BlockSpec((tm, 2 * D_MODEL), row),
                  full(w["w_out_sb"]), full(w["w_out_mb"]), full(w["w_out"]),
                  full(w["norm_mem_g"]), full(w["w_q_mem"]),
                  pl.BlockSpec((1, n_mem, W_MEM), mem), pl.BlockSpec((1, n_mem, W_MEM), mem),
                  full(w["w_o_mem"]), full(w["norm_ffn_g"]),
                  full(w["w_router_hilo"]), full(w["w_router_hi"]), full(w["b_router"])],
        out_specs=out_specs,
        out_shape=out_shape,
        compiler_params=_cparams(("parallel",), 48),
        name="post",
    )(x, osb, omb, gate, w["w_out_sb"], w["w_out_mb"], w["w_out"], w["norm_mem_g"], w["w_q_mem"],
      mem_k16, mem_v16, w["w_o_mem"], w["norm_ffn_g"],
      w["w_router_hilo"], w["w_router_hi"], w["b_router"])


DENSE_EXPERTS_PER_STEP = 4


def _moe_kernel(xn_ref, comb_ref, x2_ref, wg_ref, wu_ref, wd_ref, gfin_ref, y_ref, acc_ref):
    e = pl.program_id(1)

    @pl.when(e == 0)
    def _():
        acc_ref[...] = jnp.zeros_like(acc_ref)

    xn = xn_ref[...]
    comb = comb_ref[...]
    lane = lax.broadcasted_iota(I32, (1, LANES), 1)
    pairs = [(_dot(xn, wg_ref[j]), _dot(xn, wu_ref[j])) for j in range(DENSE_EXPERTS_PER_STEP)]
    hidden = []
    for j, (hg, hu) in enumerate(pairs):
        weight = jnp.sum(jnp.where(lane == e * DENSE_EXPERTS_PER_STEP + j, comb, 0.0), axis=1, keepdims=True)
        hidden.append(((hg / (1.0 + jnp.exp(-hg))) * hu * weight).astype(BF16))
    acc_ref[...] += functools.reduce(jnp.add, [_dot(hidden[j], wd_ref[j]) for j in range(DENSE_EXPERTS_PER_STEP)])

    @pl.when(e == pl.num_programs(1) - 1)
    def _():
        y_ref[...] = _rms(x2_ref[...] + acc_ref[...], gfin_ref[...])


def _moe(xn16, comb, x2, wg16, wu16, wd16, gfin, tm):
    n = xn16.shape[0]
    nt = n // tm
    row = lambda i, e: (i, 0)
    exp_w = lambda i, e: (e, 0, 0)
    return pl.pallas_call(
        _moe_kernel,
        grid=(nt, N_EXPERTS // DENSE_EXPERTS_PER_STEP),
        in_specs=[pl.BlockSpec((tm, D_MODEL), row), pl.BlockSpec((tm, LANES), row), pl.BlockSpec((tm, D_MODEL), row),
                  pl.BlockSpec((DENSE_EXPERTS_PER_STEP, D_MODEL, D_EXPERT), exp_w),
                  pl.BlockSpec((DENSE_EXPERTS_PER_STEP, D_MODEL, D_EXPERT), exp_w),
                  pl.BlockSpec((DENSE_EXPERTS_PER_STEP, D_EXPERT, D_MODEL), exp_w),
                  pl.BlockSpec((1, D_MODEL), lambda i, e: (0, 0))],
        out_specs=pl.BlockSpec((tm, D_MODEL), row),
        out_shape=jax.ShapeDtypeStruct((n, D_MODEL), F32),
        scratch_shapes=[pltpu.VMEM((tm, D_MODEL), F32)],
        compiler_params=_cparams(("parallel", "arbitrary"), 48),
        name="moe",
    )(xn16, comb, x2, wg16, wu16, wd16, gfin)


def _group_experts_kernel(grp_ref, live_ref, xs_ref, cs_ref, wg_ref, wu_ref, wd_ref, ys_ref, acc_ref):
    j, e = pl.program_id(0), pl.program_id(1)

    @pl.when(live_ref[j] > 0)
    def _():
        @pl.when(e == 0)
        def _():
            acc_ref[...] = jnp.zeros_like(acc_ref)

        xs = xs_ref[...]
        cs = cs_ref[...]
        lane = lax.broadcasted_iota(I32, (1, LANES), 1)
        pairs = [(_dot(xs, wg_ref[0, j]), _dot(xs, wu_ref[0, j])) for j in range(EXPERTS_PER_STEP)]
        hidden = []
        for j, (hg, hu) in enumerate(pairs):
            weight = jnp.sum(jnp.where(lane == e * EXPERTS_PER_STEP + j, cs, 0.0), axis=1, keepdims=True)
            hidden.append(((hg / (1.0 + jnp.exp(-hg))) * hu * weight).astype(BF16))
        down = [_dot(hidden[j], wd_ref[0, j]) for j in range(EXPERTS_PER_STEP)]
        acc_ref[...] += functools.reduce(jnp.add, down)

        @pl.when(e == pl.num_programs(1) - 1)
        def _():
            ys_ref[...] = acc_ref[...]

    @pl.when(jnp.logical_and(live_ref[j] == 0, e == 0))
    def _():
        ys_ref[...] = jnp.zeros_like(ys_ref)


def _group_experts(xs, cs, tile_group, tile_live, wg16, wu16, wd16):
    n_steps = tile_group.shape[0]
    by_group = lambda a: a.reshape(N_GROUPS, EXPERTS_PER_GROUP, *a.shape[1:])
    rows = lambda j, e, grp, live: (j, 0)
    inner = EXPERTS_PER_GROUP // EXPERTS_PER_STEP
    expert = lambda j, e, grp, live: (grp[j], e * live[j] + (inner - 1) * (1 - live[j]), 0, 0)
    grid_spec = pltpu.PrefetchScalarGridSpec(
        num_scalar_prefetch=2, grid=(n_steps, inner),
        in_specs=[pl.BlockSpec((EXPERT_TILE, D_MODEL), rows), pl.BlockSpec((EXPERT_TILE, LANES), rows),
                  pl.BlockSpec((1, EXPERTS_PER_STEP, D_MODEL, D_EXPERT), expert),
                  pl.BlockSpec((1, EXPERTS_PER_STEP, D_MODEL, D_EXPERT), expert),
                  pl.BlockSpec((1, EXPERTS_PER_STEP, D_EXPERT, D_MODEL), expert)],
        out_specs=pl.BlockSpec((EXPERT_TILE, D_MODEL), rows),
        scratch_shapes=[pltpu.VMEM((EXPERT_TILE, D_MODEL), F32)])
    return pl.pallas_call(
        _group_experts_kernel,
        grid_spec=grid_spec,
        out_shape=jax.ShapeDtypeStruct((xs.shape[0], D_MODEL), F32),
        compiler_params=_cparams(("arbitrary", "arbitrary"), 48),
        name="group_experts",
    )(tile_group, tile_live, xs, cs, by_group(wg16), by_group(wu16), by_group(wd16))


def _gather_norm_kernel(dst_ref, pos_ref, x2_ref, gfin_ref, ys_hbm, y_ref, stage, local, sem):
    k = pl.program_id(0)
    tm = x2_ref.shape[0]
    sorted_rows = tm + SORT_ROWS_PAD
    slot = k % 2
    halves = range(EXPERT_TILE // WINDOW_ROWS)

    def per_window(tile, fn):
        starts = _run_starts(dst_ref, tile)
        for g in range(N_GROUPS):
            for half in halves:
                go = functools.partial(fn, g, half, starts[g])
                if half == 0:
                    go()
                else:
                    pl.when(starts[g + 1] - starts[g] > half * WINDOW_ROWS)(go)
        return starts

    def dma(tile, into, act):
        def fn(g, half, _):
            src = pl.ds(pl.multiple_of(dst_ref[tile, g] + half * WINDOW_ROWS, GROUP_RUN_ALIGN), WINDOW_ROWS)
            copy = pltpu.make_async_copy(ys_hbm.at[src], stage.at[into, g, pl.ds(half * WINDOW_ROWS, WINDOW_ROWS)],
                                         sem.at[into, g, half])
            getattr(copy, act)()
        per_window(tile, fn)

    @pl.when(k == 0)
    def _():
        local[...] = jnp.zeros_like(local)
        dma(0, 0, "start")

    @pl.when(k + 1 < pl.num_programs(0))
    def _():
        dma(k + 1, 1 - slot, "start")

    dma(k, slot, "wait")

    def place(g, half, run_start):
        rows = pl.ds(pl.multiple_of(run_start + half * WINDOW_ROWS, GROUP_RUN_ALIGN), WINDOW_ROWS)
        local[rows, :] = stage[slot, g, pl.ds(half * WINDOW_ROWS, WINDOW_ROWS), :]

    run_start = per_window(k, place)
    local[pl.ds(pl.multiple_of(run_start[N_GROUPS], GROUP_RUN_ALIGN), WINDOW_ROWS), :] = jnp.zeros(
        (WINDOW_ROWS, local.shape[1]), F32)

    pick = (lax.broadcasted_iota(I32, (tm, sorted_rows), 1) == pos_ref[:, 0:1]).astype(BF16)
    hi, lo = _split_bf16(local[0:sorted_rows, :])
    y_ref[...] = _rms(x2_ref[...] + (_dot(pick, hi) + _dot(pick, lo)), gfin_ref[...])


def _gather_norm(dst, pos, x2, gfin, ys):
    n = x2.shape[0]
    tm = EXPERT_TILE
    nt = n // tm
    row = lambda i, offs: (i, 0)
    grid_spec = pltpu.PrefetchScalarGridSpec(
        num_scalar_prefetch=1, grid=(nt,),
        in_specs=[pl.BlockSpec((tm, LANES), row), pl.BlockSpec((tm, D_MODEL), row),
                  pl.BlockSpec((1, D_MODEL), lambda i, offs: (0, 0)), pl.BlockSpec(memory_space=pl.ANY)],
        out_specs=pl.BlockSpec((tm, D_MODEL), row),
        scratch_shapes=[pltpu.VMEM((2, N_GROUPS, tm, D_MODEL), F32),
                        pltpu.VMEM((tm + SORT_ROWS_PAD + tm, D_MODEL), F32),
                        pltpu.SemaphoreType.DMA((2, N_GROUPS, EXPERT_TILE // WINDOW_ROWS))])
    return pl.pallas_call(
        _gather_norm_kernel,
        grid_spec=grid_spec,
        out_shape=jax.ShapeDtypeStruct((n, D_MODEL), F32),
        compiler_params=_cparams(("arbitrary",), 56),
        name="gather_norm",
    )(dst, pos, x2, gfin, ys)


SB_SAMPLE_PAGES = 2
NEW_ROWS = 8


def _new_token_page(ref):
    rows = ref[0]
    return jnp.concatenate([rows, jnp.zeros((PAGE_SIZE - rows.shape[0], rows.shape[1]), rows.dtype)],
                           axis=0).astype(BF16)


def _own_head_block(full, n_new):
    row_head = lax.broadcasted_iota(I32, (full.shape[0], 1), 0) // n_new
    out = jnp.zeros((full.shape[0], HEAD_DIM), F32)
    for h in range(N_HEADS):
        out = jnp.where(row_head == h, full[:, h * HEAD_DIM:(h + 1) * HEAD_DIM], out)
    return out


def _sb_sample_kernel(pt_ref, q_ref, kn_ref, vn_ref, kc_hbm, vc_hbm, o_ref, kbuf, vbuf, sem, *, n_new, n_pages):
    b = pl.program_id(0)
    q = q_ref[0]
    rows = q.shape[0]
    n_units = n_pages // SB_SAMPLE_PAGES
    unit_keys = SB_SAMPLE_PAGES * PAGE_SIZE

    def unit_copies(u, slot, row=b):
        copies = []
        for j in range(SB_SAMPLE_PAGES):
            page = pt_ref[row, u * SB_SAMPLE_PAGES + j]
            copies += [pltpu.make_async_copy(kc_hbm.at[page], kbuf.at[slot, j], sem.at[0, slot, j]),
                       pltpu.make_async_copy(vc_hbm.at[page], vbuf.at[slot, j], sem.at[1, slot, j])]
        return copies

    def start(u, slot, row=b):
        for c in unit_copies(u, slot, row):
            c.start()

    def wait(u, slot):
        for c in unit_copies(u, slot):
            c.wait()

    def unit_keys_on_lanes(buf, slot):
        return jnp.concatenate([buf[slot, j].reshape(W_MIX, PAGE_SIZE).astype(BF16)
                                for j in range(SB_SAMPLE_PAGES)], axis=1)

    slot_of = lambda u: (n_units - 1 - u) % 2

    @pl.when(b == 0)
    def _():
        start(n_units - 1, 0)

    key_slot = lax.broadcasted_iota(I32, (rows, PAGE_SIZE), 1)
    tok = lax.broadcasted_iota(I32, (rows, PAGE_SIZE), 0) % n_new
    a, run = _stick_breaking_tile(_dot_nt(q, _new_token_page(kn_ref)), _later_keys(PAGE_SIZE),
                                  jnp.zeros((rows, 1), F32), key_slot < tok)
    acc = _dot(a.astype(BF16), _new_token_page(vn_ref))
    later = _later_keys(unit_keys)

    def cond(carry):
        u, alive = carry[0], carry[1]
        return jnp.logical_and(u >= 0, alive > 0)

    def body(carry):
        u, _, run, acc = carry
        slot = slot_of(u)

        @pl.when(u > 0)
        def _():
            start(u - 1, 1 - slot)

        wait(u, slot)
        z = _dot(q, unit_keys_on_lanes(kbuf, slot))
        a, run = _stick_breaking_tile(z, later, run, None)
        acc = acc + _dot_nt(a.astype(BF16), unit_keys_on_lanes(vbuf, slot))
        alive = (jnp.max(run) > SB_DEAD).astype(I32)
        return (u - 1, alive, run, acc)

    u_next, _, _, acc = lax.while_loop(cond, body, (jnp.int32(n_units - 1), jnp.int32(1), run, acc))

    @pl.when(u_next >= 0)
    def _():
        wait(u_next, slot_of(u_next))

    @pl.when(b + 1 < pl.num_programs(0))
    def _():
        start(n_units - 1, 0, b + 1)

    o_ref[0] = _own_head_block(acc, n_new)


def _sb_sample(page_table, q_bd, kn_t, vn_t, cache_kt, cache_vt, n_new):
    nb, n_pages = page_table.shape
    rows = N_HEADS * n_new
    per_b = lambda b, pt: (b, 0, 0)
    assert n_pages % SB_SAMPLE_PAGES == 0
    page_buffers = pltpu.VMEM((2, SB_SAMPLE_PAGES, N_HEADS, HEAD_DIM, PAGE_SIZE), F32)
    grid_spec = pltpu.PrefetchScalarGridSpec(
        num_scalar_prefetch=1, grid=(nb,),
        in_specs=[pl.BlockSpec((1, rows, W_MIX), per_b),
                  pl.BlockSpec((1, NEW_ROWS, W_MIX), per_b),
                  pl.BlockSpec((1, NEW_ROWS, W_MIX), per_b),
                  pl.BlockSpec(memory_space=pl.ANY),
                  pl.BlockSpec(memory_space=pl.ANY)],
        out_specs=pl.BlockSpec((1, rows, HEAD_DIM), per_b),
        scratch_shapes=[page_buffers, page_buffers, pltpu.SemaphoreType.DMA((2, 2, SB_SAMPLE_PAGES))])
    return pl.pallas_call(
        functools.partial(_sb_sample_kernel, n_new=n_new, n_pages=n_pages),
        grid_spec=grid_spec,
        out_shape=jax.ShapeDtypeStruct((nb, rows, HEAD_DIM), F32),
        compiler_params=_cparams(("arbitrary",), 32),
        name="sb_sample",
    )(page_table, q_bd, kn_t, vn_t, cache_kt, cache_vt)


PAGES_PER_BLOCK = MOBA_BLOCK // PAGE_SIZE
MOBA_SAMPLE_BLOCKS_PER_STEP = 16


def _moba_sample_kernel(pt_ref, q_ref, kn_ref, vn_ref, *refs, n_new):
    pages_per_step = MOBA_SAMPLE_BLOCKS_PER_STEP * PAGES_PER_BLOCK
    k_refs, v_refs = refs[:pages_per_step], refs[pages_per_step:2 * pages_per_step]
    o_ref, pm_ref, pl_ref, pg_ref, po_ref = refs[2 * pages_per_step:]
    step = pl.program_id(1)
    n_blocks = pl.num_programs(1) * MOBA_SAMPLE_BLOCKS_PER_STEP
    q = q_ref[0]
    rows = q.shape[0]
    block_lane = lax.broadcasted_iota(I32, (1, LANES), 1)

    @pl.when(step == 0)
    def _():
        pm_ref[...] = jnp.full(pm_ref.shape, NEG_INF, F32)
        pl_ref[...] = jnp.zeros_like(pl_ref)
        pg_ref[...] = jnp.zeros_like(pg_ref)

    def block_pages(page_refs, j):
        pages = [r[0].reshape(W_MIX, PAGE_SIZE).astype(BF16)
                 for r in page_refs[j * PAGES_PER_BLOCK:(j + 1) * PAGES_PER_BLOCK]]
        return jnp.concatenate(pages, axis=1)

    pm, pl_, pg = pm_ref[...], pl_ref[...], pg_ref[...]
    blocks = range(MOBA_SAMPLE_BLOCKS_PER_STEP)
    scores = [_dot(q, block_pages(k_refs, j)) for j in blocks]
    maxes = [jnp.max(z, axis=1, keepdims=True) for z in scores]
    weights = [jnp.exp2(z - m) for z, m in zip(scores, maxes)]
    for j in blocks:
        blk = step * MOBA_SAMPLE_BLOCKS_PER_STEP + j
        here = block_lane == blk
        pm = jnp.where(here, maxes[j], pm)
        pl_ = jnp.where(here, jnp.sum(weights[j], axis=1, keepdims=True), pl_)
        pg = jnp.where(here, jnp.sum(scores[j], axis=1, keepdims=True), pg)
        po_ref[blk] = _dot_nt(weights[j].astype(BF16), block_pages(v_refs, j))
    pm_ref[...], pl_ref[...], pg_ref[...] = pm, pl_, pg

    @pl.when(step == pl.num_programs(1) - 1)
    def _():
        chosen = _top_blocks(pg, block_lane < n_blocks, block_lane)
        slot = lax.broadcasted_iota(I32, (rows, PAGE_SIZE), 1)
        tok = lax.broadcasted_iota(I32, (rows, PAGE_SIZE), 0) % n_new
        zn = jnp.where(slot <= tok, _dot_nt(q, _new_token_page(kn_ref)), NEG_INF)
        m_all = jnp.maximum(jnp.max(jnp.where(chosen, pm, NEG_INF), axis=1, keepdims=True),
                            jnp.max(zn, axis=1, keepdims=True))
        pn = jnp.exp2(zn - m_all)
        w = jnp.where(chosen, jnp.exp2(pm - m_all), 0.0)
        total = jnp.sum(w * pl_, axis=1, keepdims=True) + jnp.sum(pn, axis=1, keepdims=True)
        acc = _dot(pn.astype(BF16), _new_token_page(vn_ref))
        for n in range(po_ref.shape[0]):
            acc = acc + w[:, n:n + 1] * po_ref[n]
        o_ref[0] = _own_head_block(acc, n_new) / total


def _moba_sample(page_table, q_bd, kn_t, vn_t, cache_kt, cache_vt, n_new):
    nb, n_pages = page_table.shape
    rows = N_HEADS * n_new
    pages_per_step = MOBA_SAMPLE_BLOCKS_PER_STEP * PAGES_PER_BLOCK
    n_steps = n_pages // pages_per_step
    per_b = lambda b, s, pt: (b, 0, 0)
    page_spec = lambda j: pl.BlockSpec((1, N_HEADS, HEAD_DIM, PAGE_SIZE),
                                       lambda b, s, pt: (pt[b, s * pages_per_step + j], 0, 0, 0))
    page_specs = [page_spec(j) for j in range(pages_per_step)]
    grid_spec = pltpu.PrefetchScalarGridSpec(
        num_scalar_prefetch=1, grid=(nb, n_steps),
        in_specs=[pl.BlockSpec((1, rows, W_MIX), per_b),
                  pl.BlockSpec((1, NEW_ROWS, W_MIX), per_b),
                  pl.BlockSpec((1, NEW_ROWS, W_MIX), per_b)] + page_specs + page_specs,
        out_specs=pl.BlockSpec((1, rows, HEAD_DIM), per_b),
        scratch_shapes=[pltpu.VMEM((rows, LANES), F32), pltpu.VMEM((rows, LANES), F32),
                        pltpu.VMEM((rows, LANES), F32),
                        pltpu.VMEM((n_pages // PAGES_PER_BLOCK, rows, W_MIX), F32)])
    return pl.pallas_call(
        functools.partial(_moba_sample_kernel, n_new=n_new),
        grid_spec=grid_spec,
        out_shape=jax.ShapeDtypeStruct((nb, rows, HEAD_DIM), F32),
        compiler_params=_cparams(("parallel", "arbitrary"), 48),
        name="moba_sample",
    )(page_table, q_bd, kn_t, vn_t, *([cache_kt] * pages_per_step), *([cache_vt] * pages_per_step))


def _rope_tables(pos):
    half = HEAD_DIM // 2
    inv_freq = ROPE_THETA ** (-jnp.arange(half, dtype=F32) / half)
    ang = pos.astype(F32)[:, None] * inv_freq[None, :]
    cos = jnp.cos(ang)
    sin = jnp.sin(ang)
    heads_per_tile = LANES // HEAD_DIM
    return (jnp.tile(jnp.concatenate([cos, cos], axis=1), (1, heads_per_tile)),
            jnp.tile(jnp.concatenate([-sin, sin], axis=1), (1, heads_per_tile)))


def _pad_lanes(a):
    return jnp.pad(a, ((0, 0), (0, LANES - a.shape[1])))


def _prepare_weights(norm_mix_g, w_in, b_gate, w_out_sb, w_out_mb, w_out, norm_mem_g, norm_memsrc_g, w_q_mem,
                     w_kv_mem, w_o_mem, norm_ffn_g, w_router_group, b_router_group, w_router_expert,
                     b_router_expert, w_gate_e, w_up_e, w_down_e, norm_final_g):
    row = lambda v: v.reshape(1, -1).astype(F32)
    w_re = w_router_expert.transpose(1, 0, 2).reshape(D_MODEL, N_EXPERTS)
    wr_hi, wr_lo = _split_bf16(_pad_lanes(jnp.concatenate([w_re, w_router_group], axis=1)))
    b_router = _pad_lanes(jnp.concatenate([row(b_router_expert), row(b_router_group)], axis=1))
    return dict(
        norm_mix_g=row(norm_mix_g), w_in=w_in.astype(BF16), b_gate=row(b_gate),
        w_out_sb=w_out_sb.astype(BF16), w_out_mb=w_out_mb.astype(BF16), w_out=w_out.astype(BF16),
        norm_mem_g=row(norm_mem_g), norm_memsrc_g=row(norm_memsrc_g), w_q_mem=w_q_mem.astype(BF16),
        w_kv_mem=w_kv_mem.astype(BF16), w_o_mem=w_o_mem.astype(BF16), norm_ffn_g=row(norm_ffn_g),
        w_router_hilo=jnp.concatenate([wr_hi, wr_lo], axis=1), w_router_hi=wr_hi, b_router=b_router,
        w_gate_e=w_gate_e.astype(BF16), w_up_e=w_up_e.astype(BF16), w_down_e=w_down_e.astype(BF16),
        norm_final_g=row(norm_final_g))


def _tail(x, osb, omb, gate, mem_k16, mem_v16, rows_per_batch, w, tm_post, tm_moe, **mem_mask):
    x2, xn16, comb, _ = _post(x, osb, omb, gate, mem_k16, mem_v16, rows_per_batch, tm_post, w, **mem_mask)
    return _moe(xn16, comb, x2, w["w_gate_e"], w["w_up_e"], w["w_down_e"], w["norm_final_g"], tm_moe)


def kernel(x_prompt, x_sample, mem_prompt, cache_sb_k, cache_sb_v, cache_mb_k, cache_mb_v, cache_mem_k, cache_mem_v, page_table, norm_mix_g, w_in, b_gate, w_out_sb, w_out_mb, w_out, norm_mem_g, norm_memsrc_g, w_q_mem, w_kv_mem, w_o_mem, norm_ffn_g, w_router_group, b_router_group, w_router_expert, b_router_expert, w_gate_e, w_up_e, w_down_e, norm_final_g):
    w = _prepare_weights(norm_mix_g, w_in, b_gate, w_out_sb, w_out_mb, w_out, norm_mem_g, norm_memsrc_g, w_q_mem,
                         w_kv_mem, w_o_mem, norm_ffn_g, w_router_group, b_router_group, w_router_expert,
                         b_router_expert, w_gate_e, w_up_e, w_down_e, norm_final_g)
    batch, seq, _ = x_prompt.shape
    dec_batch, n_new, _ = x_sample.shape
    n_mem = mem_prompt.shape[1]
    n_pages = page_table.shape[1]
    past_len = n_pages * PAGE_SIZE
    assert seq % (MOBA_BLOCK * MOBA_BLOCKS_PER_ITER) == 0 and seq % (MOBA_BLOCK * MOBA_TILES_PER_STEP) == 0
    assert seq % (SB_TILE * SB_TILES_PER_STEP) == 0 and (batch * seq) % EXPERT_TILE == 0
    assert n_new <= PAGE_SIZE and n_pages // PAGES_PER_BLOCK <= LANES
    assert n_pages % (MOBA_SAMPLE_BLOCKS_PER_STEP * PAGES_PER_BLOCK) == 0
    heads = lambda t, b, s: t.reshape(b, s, N_HEADS, HEAD_DIM)

    xp = x_prompt.reshape(batch * seq, D_MODEL)
    cos_p, sin_p = _rope_tables(jnp.arange(seq, dtype=I32))
    (qsb, qmb, gate, ksb_t, vsb_t, kmb_t, vmb_t, ksb16, vsb16_t, kmb16, vmb16_t, kmean) = _proj(
        xp, w["norm_mix_g"], w["w_in"], w["b_gate"], cos_p, sin_p, PROJ_TILE, rows_per_batch=seq)
    osb = _sb_prompt(qsb, ksb16, vsb16_t, batch, seq, SB_TILE)
    omb = _moba_prompt(qmb, kmb16, vmb16_t, kmean.reshape(batch * seq // MOBA_BLOCK, W_MIX), batch, seq)
    mem_k, mem_v = _mem_kv(mem_prompt.reshape(batch * n_mem, D_MODEL), w["norm_memsrc_g"], w["w_kv_mem"])
    x2, xn16, comb, counts = _post(xp, osb, omb, gate, mem_k.astype(BF16).reshape(batch, n_mem, W_MEM),
                                   mem_v.astype(BF16).reshape(batch, n_mem, W_MEM), seq, EXPERT_TILE, w)
    n_tiles = batch * seq // EXPERT_TILE
    dst, total_rows, tile_group, tile_live = _group_layout(
        counts[:, 0, :N_GROUPS].astype(I32), batch * seq, n_tiles)
    pos, rows_by_group, weights_by_group = _scatter(dst, xn16, comb, total_rows)
    expert_out = _group_experts(rows_by_group, weights_by_group, tile_group, tile_live,
                                w["w_gate_e"], w["w_up_e"], w["w_down_e"])
    y_prompt = _gather_norm(dst, pos, x2, w["norm_final_g"], expert_out)

    rows_s = dec_batch * n_new
    xs = x_sample.reshape(rows_s, D_MODEL)
    cos_s, sin_s = _rope_tables(past_len + (jnp.arange(rows_s, dtype=I32) % n_new))
    (qsb_s, qmb_s, gate_s, ksb_s, vsb_s, kmb_s, vmb_s) = _proj(
        xs, w["norm_mix_g"], w["w_in"], w["b_gate"], cos_s, sin_s, rows_s)

    def block_diagonal(q16):
        q_cols = heads(q16, dec_batch, n_new).transpose(0, 2, 1, 3).reshape(dec_batch, N_HEADS * n_new, HEAD_DIM)
        own = (jnp.arange(N_HEADS * n_new)[:, None] // n_new) == (jnp.arange(W_MIX)[None, :] // HEAD_DIM)
        return jnp.where(own[None], jnp.tile(q_cols, (1, 1, N_HEADS)), jnp.zeros((), q16.dtype))

    def new_page(t):
        return jnp.pad(t.reshape(dec_batch, n_new, W_MIX), ((0, 0), (0, NEW_ROWS - n_new), (0, 0)))

    def token_rows(o):
        o = o.reshape(dec_batch, N_HEADS, n_new, HEAD_DIM).transpose(0, 2, 1, 3)
        return o.reshape(rows_s, W_MIX).astype(BF16)

    pages = lambda c: c.transpose(0, 2, 3, 1)
    osb_s = token_rows(_sb_sample(page_table, block_diagonal(qsb_s), new_page(ksb_s), new_page(vsb_s),
                                  pages(cache_sb_k), pages(cache_sb_v), n_new))
    omb_s = token_rows(_moba_sample(page_table, block_diagonal(qmb_s), new_page(kmb_s), new_page(vmb_s),
                                    pages(cache_mb_k), pages(cache_mb_v), n_new))
    y_sample = _tail(xs, osb_s, omb_s, gate_s,
                     cache_mem_k.reshape(1, dec_batch * n_mem, W_MEM).astype(BF16),
                     cache_mem_v.reshape(1, dec_batch * n_mem, W_MEM).astype(BF16),
                     rows_s, w, rows_s, rows_s, rows_per_mem=n_new, keys_per_mem=n_mem)

    mem_heads = lambda t: t.reshape(batch, n_mem, H_MEM, HD_MEM)
    from_head_major = lambda t: t.reshape(batch, N_HEADS, HEAD_DIM, seq).transpose(0, 3, 1, 2)
    return (y_prompt.reshape(batch, seq, D_MODEL), y_sample.reshape(dec_batch, n_new, D_MODEL),
            from_head_major(ksb_t), from_head_major(vsb_t), from_head_major(kmb_t), from_head_major(vmb_t),
            mem_heads(mem_k), mem_heads(mem_v),
            heads(ksb_s, dec_batch, n_new), heads(vsb_s, dec_batch, n_new),
            heads(kmb_s, dec_batch, n_new), heads(vmb_s, dec_batch, n_new))
```
